```python
import math
import jax
import jax.numpy as jnp
from jax import lax
import numpy as np

D_MODEL = 1024
BATCH = 2
SEQ = 8192
DEPTH = 4

GRID_W = 64
CTX_LEN = 256
HEAD_DIM = 64
ATTN_SCALE = HEAD_DIM ** -0.5
BLOCK = 128
N_MIX_HEADS = D_MODEL // HEAD_DIM
A_HEADS = N_MIX_HEADS // 2
A_KV_HEADS = A_HEADS // 4
A_WINDOW = 128
B_HEADS = N_MIX_HEADS - A_HEADS
NA_KH_MAX = 8
NA_KW = 16
NA_QCOLS = 16
NA_KCOLS = 32
C_HEADS = D_MODEL // (2 * HEAD_DIM)
C_VDIM = 2 * HEAD_DIM
ROPE_THETA = 10000.0
ROPE_AXIS_DIM = HEAD_DIM // 2
FF_DIM = 256 * math.ceil(8 * D_MODEL / (3 * 256))
N_EXPERTS = 8
TOP_K = 2
EXPERT_FF = 7 * D_MODEL // 2
LN_EPS = 1e-5
N_EVEN = (DEPTH + 1) // 2
N_ODD = DEPTH // 2
DN_ALPHA = (2 * DEPTH) ** 0.25
DN_BETA = (8 * DEPTH) ** -0.25
MASK_VALUE = -1e30
A_Q = A_HEADS * HEAD_DIM
A_KV = A_KV_HEADS * HEAD_DIM
B_W = B_HEADS * HEAD_DIM
AB_SPLITS = [A_Q, A_Q + A_KV, A_Q + 2 * A_KV, A_Q + 2 * A_KV + B_W, A_Q + 2 * A_KV + 2 * B_W]
AB_IN = A_Q + 2 * A_KV + 3 * B_W
AB_OUT = A_Q + B_W
C_QK = C_HEADS * 2 * HEAD_DIM
C_OUT = C_HEADS * C_VDIM
C_IN = 2 * C_QK + C_OUT

kernel_name = 'hybrid_diffusion_trunk'


def _layernorm(x, g, b):
    xf = x.astype(jnp.float32)
    mu = jnp.mean(xf, axis=-1, keepdims=True)
    var = jnp.mean(jnp.square(xf - mu), axis=-1, keepdims=True)
    y = (xf - mu) * lax.rsqrt(var + LN_EPS) * g.astype(jnp.float32) + b.astype(jnp.float32)
    return y.astype(x.dtype)


def _rms_norm(x, g):
    xf = x.astype(jnp.float32)
    y = xf * lax.rsqrt(jnp.mean(jnp.square(xf), axis=-1, keepdims=True) + LN_EPS) * g.astype(jnp.float32)
    return y.astype(x.dtype)


def _rope_tables(n):
    t = jnp.arange(n, dtype=jnp.int32)
    row = (t // GRID_W).astype(jnp.float32)
    col = (t % GRID_W).astype(jnp.float32)
    inv = ROPE_THETA ** (-jnp.arange(0, ROPE_AXIS_DIM, 2, dtype=jnp.float32) / ROPE_AXIS_DIM)
    ar = row[:, None] * inv[None, :]
    ac = col[:, None] * inv[None, :]
    ang = jnp.concatenate([ar, ar, ac, ac], axis=-1)
    return jnp.cos(ang), jnp.sin(ang)


def _rope2d(x, cos, sin):
    x1, x2, x3, x4 = jnp.split(x, 4, axis=-1)
    rot = jnp.concatenate([-x2, x1, -x4, x3], axis=-1)
    shape = (1, cos.shape[0]) + (1,) * (x.ndim - 3) + (HEAD_DIM,)
    return (x * cos.reshape(shape) + rot * sin.reshape(shape)).astype(x.dtype)


def _window_gqa(q, k, v, kc, vc, sink):
    B, N, HA, Dh = q.shape
    KV = k.shape[2]
    G = HA // KV
    NB = N // BLOCK
    C = kc.shape[1]
    L = 3 * BLOCK
    qb = q.reshape(B, NB, BLOCK, KV, G, Dh)

    def band(t):
        tp = jnp.pad(t, ((0, 0), (BLOCK, BLOCK), (0, 0), (0, 0))).reshape(B, NB + 2, BLOCK, KV, Dh)
        return jnp.concatenate([tp[:, :-2], tp[:, 1:-1], tp[:, 2:]], axis=2)

    kb, vb = band(k), band(v)
    s_loc = jnp.einsum('bnqkgd,bnskd->bnkgqs', qb, kb, preferred_element_type=jnp.float32) * ATTN_SCALE
    qi = jnp.arange(BLOCK)[:, None]
    si = jnp.arange(L)[None, :]
    kpos = jnp.arange(NB)[:, None, None] * BLOCK - BLOCK + si[None]
    valid = (jnp.abs(si - BLOCK - qi) <= A_WINDOW)[None] & (kpos >= 0) & (kpos < N)
    s_loc = jnp.where(valid[None, :, None, None], s_loc, MASK_VALUE)
    s_ctx = jnp.einsum('bnqkgd,bckd->bnkgqc', qb, kc, preferred_element_type=jnp.float32) * ATTN_SCALE
    s_sink = jnp.broadcast_to(sink.reshape(KV, G, 1, 1).astype(jnp.float32), s_loc.shape[:-1] + (1,))
    p = jax.nn.softmax(jnp.concatenate([s_loc, s_ctx, s_sink], axis=-1), axis=-1).astype(v.dtype)
    o = (jnp.einsum('bnkgqs,bnskd->bnqkgd', p[..., :L], vb)
         + jnp.einsum('bnkgqc,bckd->bnqkgd', p[..., L:L + C], vc))
    return o.reshape(B, N, HA, Dh)


def _ctx_gqa(q, k, v, sink):
    B, C, HA, Dh = q.shape
    KV = k.shape[2]
    G = HA // KV
    qg = q.reshape(B, C, KV, G, Dh)
    s = jnp.einsum('bqkgd,bskd->bkgqs', qg, k, preferred_element_type=jnp.float32) * ATTN_SCALE
    s_sink = jnp.broadcast_to(sink.reshape(KV, G, 1, 1).astype(jnp.float32), s.shape[:-1] + (1,))
    p = jax.nn.softmax(jnp.concatenate([s, s_sink], axis=-1), axis=-1)[..., :C].astype(v.dtype)
    return jnp.einsum('bkgqs,bskd->bqkgd', p, v).reshape(B, C, HA, Dh)


def _ctx_mha(q, k, v):
    s = jnp.einsum('bqhd,bshd->bhqs', q, k, preferred_element_type=jnp.float32) * ATTN_SCALE
    p = jax.nn.softmax(s, axis=-1).astype(v.dtype)
    return jnp.einsum('bhqs,bshd->bqhd', p, v)


def _neighbourhood_attn(q, k, v, kc, vc, rpb, rows):
    B, N, H, Dh = q.shape
    KH = min(NA_KH_MAX, rows)
    NCB = GRID_W // NA_QCOLS
    S = KH * NA_KCOLS
    r = jnp.arange(rows)
    rs = jnp.clip(r - KH // 2, 0, rows - KH)
    key_rows = rs[:, None] + jnp.arange(KH)[None, :]
    qcols = jnp.arange(GRID_W).reshape(NCB, NA_QCOLS)
    kc0 = jnp.clip(jnp.arange(NCB) * NA_QCOLS - NA_KW // 2, 0, GRID_W - NA_KCOLS)
    key_cols = kc0[:, None] + jnp.arange(NA_KCOLS)[None, :]
    idx = (key_rows[:, None, :, None] * GRID_W + key_cols[None, :, None, :]).reshape(rows, NCB, S)
    kg = k[:, idx]
    vg = v[:, idx]
    qb = q.reshape(B, rows, NCB, NA_QCOLS, H, Dh)
    s = jnp.einsum('brjqhd,brjshd->brjhqs', qb, kg, preferred_element_type=jnp.float32) * ATTN_SCALE
    cs = jnp.clip(qcols - NA_KW // 2, 0, GRID_W - NA_KW)
    col_ok = (key_cols[:, None, :] >= cs[:, :, None]) & (key_cols[:, None, :] < cs[:, :, None] + NA_KW)
    mask = jnp.broadcast_to(col_ok[:, :, None, :], (NCB, NA_QCOLS, KH, NA_KCOLS)).reshape(NCB, NA_QCOLS, S)
    dr = key_rows - r[:, None]
    dc = key_cols[:, None, :] - qcols[:, :, None]
    dr_i = dr + NA_KH_MAX - 1
    dc_i = jnp.clip(dc + NA_KW - 1, 0, 2 * NA_KW - 2)
    bias = rpb[:, dr_i[:, None, None, :, None], dc_i[None, :, :, None, :]]
    bias = jnp.transpose(bias, (1, 2, 0, 3, 4, 5)).reshape(rows, NCB, H, NA_QCOLS, S).astype(jnp.float32)
    s = jnp.where(mask[None, None, :, None], s + bias[None], MASK_VALUE)
    s_ctx = jnp.einsum('brjqhd,bchd->brjhqc', qb, kc, preferred_element_type=jnp.float32) * ATTN_SCALE
    p = jax.nn.softmax(jnp.concatenate([s, s_ctx], axis=-1), axis=-1).astype(v.dtype)
    o = (jnp.einsum('brjhqs,brjshd->brjqhd', p[..., :S], vg)
         + jnp.einsum('brjhqc,bchd->brjqhd', p[..., S:], vc))
    return o.reshape(B, N, H, Dh)


def _mixer_ab(hx, hc, w_in, w_out, sink, rpb, cos, sin, with_ctx):
    B, N, _ = hx.shape
    rows = N // GRID_W

    def proj(h):
        T = h.shape[1]
        qa, ka, va, qb, kb, vb = jnp.split(h @ w_in, AB_SPLITS, axis=-1)
        return (qa.reshape(B, T, A_HEADS, HEAD_DIM), ka.reshape(B, T, A_KV_HEADS, HEAD_DIM),
                va.reshape(B, T, A_KV_HEADS, HEAD_DIM), qb.reshape(B, T, B_HEADS, HEAD_DIM),
                kb.reshape(B, T, B_HEADS, HEAD_DIM), vb.reshape(B, T, B_HEADS, HEAD_DIM))

    qa, ka, va, qb, kb, vb = proj(hx)
    qa_c, ka_c, va_c, qb_c, kb_c, vb_c = proj(hc)
    qa = _rope2d(qa, cos, sin)
    ka = _rope2d(ka, cos, sin)
    oa = _window_gqa(qa, ka, va, ka_c, va_c, sink)
    ob = _neighbourhood_attn(qb, kb, vb, kb_c, vb_c, rpb, rows)
    yx = jnp.concatenate([oa.reshape(B, N, A_Q), ob.reshape(B, N, B_W)], axis=-1) @ w_out
    if not with_ctx:
        return yx, None
    C = hc.shape[1]
    oa_c = _ctx_gqa(qa_c, ka_c, va_c, sink)
    ob_c = _ctx_mha(qb_c, kb_c, vb_c)
    yc = jnp.concatenate([oa_c.reshape(B, C, A_Q), ob_c.reshape(B, C, B_W)], axis=-1) @ w_out
    return yx, yc


def _diff_attn_block(q, k_all, v_all, lam):
    s = jnp.einsum('bqhid,bshid->bhiqs', q, k_all, preferred_element_type=jnp.float32) * ATTN_SCALE
    p = jax.nn.softmax(s, axis=-1)
    a = (p[:, :, 0] - lam * p[:, :, 1]).astype(v_all.dtype)
    return jnp.einsum('bhqs,bshe->bqhe', a, v_all)


def _mixer_c(hx, hc, w_in, w_out, lam_p, subln, lam_init, cos, sin, with_ctx):
    B, N, _ = hx.shape
    C = hc.shape[1]
    lp = lam_p.astype(jnp.float32)
    lam = jnp.exp(jnp.sum(lp[0] * lp[1])) - jnp.exp(jnp.sum(lp[2] * lp[3])) + lam_init

    def proj(h):
        T = h.shape[1]
        q, k, v = jnp.split(h @ w_in, [C_QK, 2 * C_QK], axis=-1)
        return (q.reshape(B, T, C_HEADS, 2, HEAD_DIM), k.reshape(B, T, C_HEADS, 2, HEAD_DIM),
                v.reshape(B, T, C_HEADS, C_VDIM))

    def finish(o, T):
        o = _rms_norm(o, subln) * (1.0 - lam_init)
        return o.reshape(B, T, C_OUT) @ w_out

    qx, kx, vx = proj(hx)
    qc, kc, vc = proj(hc)
    qx = _rope2d(qx, cos, sin)
    kx = _rope2d(kx, cos, sin)
    k_all = jnp.concatenate([kx, kc], axis=1)
    v_all = jnp.concatenate([vx, vc], axis=1)
    NB = N // BLOCK
    qblocks = jnp.moveaxis(qx.reshape(B, NB, BLOCK, C_HEADS, 2, HEAD_DIM), 1, 0)

    def one_block(qb):
        return _diff_attn_block(qb, k_all, v_all, lam)

    o = lax.map(one_block, qblocks)
    o = jnp.moveaxis(o, 0, 1).reshape(B, N, C_HEADS, C_VDIM)
    yx = finish(o, N)
    if not with_ctx:
        return yx, None
    yc = finish(_diff_attn_block(qc, kc, vc, lam), C)
    return yx, yc


def _swiglu(h, w_up, w_down):
    g, u = jnp.split(h @ w_up, 2, axis=-1)
    return (jax.nn.silu(g) * u) @ w_down


def _moe(h, w_router, w_up, w_down):
    B, T, D = h.shape
    hf = h.reshape(B * T, D)
    logits = (hf @ w_router).astype(jnp.float32)
    top_val, top_idx = lax.top_k(logits, TOP_K)
    top_w = jax.nn.softmax(top_val, axis=-1)
    gates = jnp.sum(jax.nn.one_hot(top_idx, N_EXPERTS, dtype=jnp.float32) * top_w[..., None], axis=1)
    gates = gates.astype(h.dtype)
    out = jnp.zeros_like(hf)
    for e in range(N_EXPERTS):
        out = out + gates[:, e:e + 1] * _swiglu(hf, w_up[e], w_down[e])
    return out.reshape(B, T, D)


def _lambda_init(layer):
    return 0.8 - 0.6 * math.exp(-0.3 * layer)


def setup_inputs(seed: int = 0) -> dict:
    key = jax.random.key(seed)
    ks = jax.random.split(key, 32)
    f32 = jnp.float32
    D = D_MODEL

    def nrm(i, shape, scale):
        return jax.random.normal(ks[i], shape, f32) * scale

    sd = D ** -0.5
    x = nrm(0, (BATCH, SEQ, D), 1.0)
    c = nrm(1, (BATCH, D), 1.0)
    ctx = nrm(2, (BATCH, CTX_LEN, D), 1.0)
    c_ctx = nrm(3, (D,), 1.0)
    w_mod = nrm(4, (DEPTH, D, 6 * D), 0.5 * sd)
    b_mod = nrm(5, (DEPTH, 6 * D), 0.02)
    ln_g = 1.0 + nrm(6, (DEPTH, 2, D), 0.02)
    ln_b = nrm(7, (DEPTH, 2, D), 0.02)
    w_in_ab = jnp.concatenate([
        nrm(8, (N_EVEN, D, A_Q + A_KV), sd),
        nrm(9, (N_EVEN, D, A_KV), sd * DN_BETA),
        nrm(10, (N_EVEN, D, 2 * B_W), sd),
        nrm(11, (N_EVEN, D, B_W), sd * DN_BETA),
    ], axis=-1)
    w_out_ab = nrm(12, (N_EVEN, AB_OUT, D), AB_OUT ** -0.5 * DN_BETA)
    sink_a = nrm(13, (N_EVEN, A_HEADS), 1.0)
    rpb_b = nrm(14, (N_EVEN, B_HEADS, 2 * NA_KH_MAX - 1, 2 * NA_KW - 1), 0.1)
    w_in_c = jnp.concatenate([
        nrm(15, (N_ODD, D, 2 * C_QK), sd),
        nrm(16, (N_ODD, D, C_OUT), sd * DN_BETA),
    ], axis=-1)
    w_out_c = nrm(17, (N_ODD, C_OUT, D), C_OUT ** -0.5 * DN_BETA)
    lam_c = nrm(18, (N_ODD, 4, HEAD_DIM), 0.1)
    subln_c = 1.0 + nrm(19, (N_ODD, C_VDIM), 0.02)
    w_ffn_up = nrm(20, (N_EVEN, D, 2 * FF_DIM), sd)
    w_ffn_down = nrm(21, (N_EVEN, FF_DIM, D), FF_DIM ** -0.5 * DN_BETA)
    w_router = nrm(22, (N_ODD, D, N_EXPERTS), sd)
    w_exp_up = nrm(23, (N_ODD, N_EXPERTS, D, 2 * EXPERT_FF), sd)
    w_exp_down = nrm(24, (N_ODD, N_EXPERTS, EXPERT_FF, D), EXPERT_FF ** -0.5 * DN_BETA)
    return {'x': x, 'c': c, 'ctx': ctx, 'c_ctx': c_ctx, 'w_mod': w_mod, 'b_mod': b_mod,
            'ln_g': ln_g, 'ln_b': ln_b, 'w_in_ab': w_in_ab, 'w_out_ab': w_out_ab, 'sink_a': sink_a,
            'rpb_b': rpb_b, 'w_in_c': w_in_c, 'w_out_c': w_out_c, 'lam_c': lam_c, 'subln_c': subln_c,
            'w_ffn_up': w_ffn_up, 'w_ffn_down': w_ffn_down, 'w_router': w_router,
            'w_exp_up': w_exp_up, 'w_exp_down': w_exp_down}


def reference(x, c, ctx, c_ctx, w_mod, b_mod, ln_g, ln_b, w_in_ab, w_out_ab, sink_a, rpb_b,
              w_in_c, w_out_c, lam_c, subln_c, w_ffn_up, w_ffn_down, w_router, w_exp_up, w_exp_down):
    B, N, D = x.shape
    C = ctx.shape[1]
    cos, sin = _rope_tables(N)
    sc = jax.nn.silu(c)
    scc = jax.nn.silu(c_ctx)
    for l in range(DEPTH):
        last = l == DEPTH - 1
        i = l // 2
        mod_x = (sc @ w_mod[l] + b_mod[l])[:, None, :]
        mod_c = (scc @ w_mod[l] + b_mod[l])[None, None, :]
        shx1, scx1, gx1, shx2, scx2, gx2 = jnp.split(mod_x, 6, axis=-1)
        shc1, scc1, gc1, shc2, scc2, gc2 = jnp.split(mod_c, 6, axis=-1)
        hx = x * (1.0 + scx1) + shx1
        hc = ctx * (1.0 + scc1) + shc1
        if l % 2 == 0:
            yx, yc = _mixer_ab(hx, hc, w_in_ab[i], w_out_ab[i], sink_a[i], rpb_b[i], cos, sin, not last)
        else:
            yx, yc = _mixer_c(hx, hc, w_in_c[i], w_out_c[i], lam_c[i], subln_c[i], _lambda_init(l),
                              cos, sin, not last)
        x = _layernorm(DN_ALPHA * x + gx1 * yx, ln_g[l, 0], ln_b[l, 0])
        if l % 2 == 0:
            ffn = functools_free_ffn = (lambda h, _i=i: _swiglu(h, w_ffn_up[_i], w_ffn_down[_i]))
        else:
            ffn = (lambda h, _i=i: _moe(h, w_router[_i], w_exp_up[_i], w_exp_down[_i]))
        if not last:
            ctx = _layernorm(DN_ALPHA * ctx + gc1 * yc, ln_g[l, 0], ln_b[l, 0])
            h = jnp.concatenate([ctx * (1.0 + scc2) + shc2, x * (1.0 + scx2) + shx2], axis=1)
            f = ffn(h)
            ctx = _layernorm(DN_ALPHA * ctx + gc2 * f[:, :C], ln_g[l, 1], ln_b[l, 1])
            x = _layernorm(DN_ALPHA * x + gx2 * f[:, C:], ln_g[l, 1], ln_b[l, 1])
        else:
            f = ffn(x * (1.0 + scx2) + shx2)
            x = _layernorm(DN_ALPHA * x + gx2 * f, ln_g[l, 1], ln_b[l, 1])
    return x
```

```python
import functools
import math

import jax
import jax.numpy as jnp
import numpy as np
from jax import lax
from jax.experimental import pallas as pl
from jax.experimental.pallas import tpu as pltpu

F32 = jnp.float32
BF16 = jnp.bfloat16

D_MODEL = 1024
DEPTH = 4
GRID_W = 64
HEAD_DIM = 64
LANES = 128
ATTN_SCALE = HEAD_DIM ** -0.5
A_BLOCK = 128
A_WINDOW = 128
A_HEADS = 8
A_KV_HEADS = 2
B_HEADS = 8
NA_KH = 8
NA_KW = 16
C_HEADS = 8
ROPE_THETA = 10000.0
ROPE_AXIS_DIM = HEAD_DIM // 2
FF_DIM = 2816
N_EXPERTS = 8
EXPERT_FF = 3584
LN_EPS = 1e-5
DN_ALPHA = (2 * DEPTH) ** 0.25
MASK_VALUE = -1e30
A_Q = A_HEADS * HEAD_DIM
A_KV = A_KV_HEADS * HEAD_DIM
B_W = B_HEADS * HEAD_DIM
C_QK = 1024
C_OUT = 1024

ROW_TILE = 512
MOE_TILE = 512
FF_CHUNK = 256
EXPERT_FF_CHUNK = 512
B_QROWS = 2
B_KROWS = B_QROWS + NA_KH - 1
C_QTILE = 256
C_KCHUNK = 512
VMEM_LIMIT = 56 * 1024 * 1024


def _lane_iota(shape):
    return lax.broadcasted_iota(jnp.int32, shape, len(shape) - 1)


def _layernorm_rows(z, g, b):
    mu = jnp.mean(z, axis=-1, keepdims=True)
    zc = z - mu
    var = jnp.mean(zc * zc, axis=-1, keepdims=True)
    return zc * lax.rsqrt(var + LN_EPS) * g + b


def _mod_kernel(c_ref, w_ref, b_ref, o_ref):
    c = c_ref[...]
    s = c * (1.0 / (1.0 + jnp.exp(-c)))
    o_ref[0] = jnp.dot(s, w_ref[0], preferred_element_type=F32,
                       precision=lax.Precision.HIGHEST) + b_ref[0]


def _mod_vectors(cond, w_mod, b_mod):
    depth, d, n6 = w_mod.shape
    tn = 1536
    return pl.pallas_call(
        _mod_kernel,
        grid=(depth, n6 // tn),
        in_specs=[
            pl.BlockSpec((8, d), lambda l, j: (0, 0)),
            pl.BlockSpec((1, d, tn), lambda l, j: (l, 0, j)),
            pl.BlockSpec((1, 1, tn), lambda l, j: (l, 0, j)),
        ],
        out_specs=pl.BlockSpec((1, 8, tn), lambda l, j: (l, 0, j)),
        out_shape=jax.ShapeDtypeStruct((depth, 8, n6), F32),
        compiler_params=pltpu.CompilerParams(vmem_limit_bytes=VMEM_LIMIT),
    )(cond, w_mod, b_mod.reshape(depth, 1, n6))


def _rope_slab(a, cos, sin_signed, low16):
    fwd = pltpu.roll(a, LANES - 16, axis=1)
    bwd = pltpu.roll(a, 16, axis=1)
    return a * cos + jnp.where(low16, fwd, bwd) * sin_signed


def _proj_kernel(x_ref, sc_ref, sh_ref, cos_ref, sin_ref, w_ref, *out_refs, segments):
    x = x_ref[...]
    h = (x * (1.0 + sc_ref[...]) + sh_ref[...]).astype(BF16)
    cos = cos_ref[...]
    sin = sin_ref[...]
    low16 = (_lane_iota(cos.shape) % 32) < 16
    col = 0
    for o_ref, (width, rope, scale) in zip(out_refs, segments):
        for c0 in range(0, width, 256):
            cw = min(256, width - c0)
            acc = jnp.dot(h, w_ref[:, col + c0:col + c0 + cw], preferred_element_type=F32)
            slabs = []
            for s0 in range(0, cw, LANES):
                a = acc[:, s0:s0 + LANES]
                if rope:
                    a = _rope_slab(a, cos, sin, low16)
                if scale != 1.0:
                    a = a * scale
                slabs.append(a.astype(BF16))
            o_ref[:, c0:c0 + cw] = slabs[0] if len(slabs) == 1 else jnp.concatenate(slabs, axis=1)
        col += width


def _in_proj(xs, mods_l, cos_t, sin_t, w, segments, n_tiles, tiles_per_batch, n_batch):
    t_rows, d = xs.shape
    tm = ROW_TILE
    grp = lambda t: jnp.minimum(t // tiles_per_batch, n_batch)
    pos = lambda t: jnp.where(t < n_batch * tiles_per_batch, t % tiles_per_batch, tiles_per_batch)
    out_shape = [jax.ShapeDtypeStruct((t_rows, s[0]), BF16) for s in segments]
    out_specs = [pl.BlockSpec((tm, s[0]), lambda t: (t, 0)) for s in segments]
    return pl.pallas_call(
        functools.partial(_proj_kernel, segments=tuple(segments)),
        grid=(n_tiles,),
        in_specs=[
            pl.BlockSpec((tm, d), lambda t: (t, 0)),
            pl.BlockSpec((None, None, 1, d), lambda t: (grp(t), 1, 0, 0)),
            pl.BlockSpec((None, None, 1, d), lambda t: (grp(t), 0, 0, 0)),
            pl.BlockSpec((tm, LANES), lambda t: (pos(t), 0)),
            pl.BlockSpec((tm, LANES), lambda t: (pos(t), 0)),
            pl.BlockSpec(w.shape, lambda t: (0, 0)),
        ],
        out_specs=out_specs,
        out_shape=out_shape,
        compiler_params=pltpu.CompilerParams(vmem_limit_bytes=VMEM_LIMIT),
    )(xs, mods_l, mods_l, cos_t, sin_t, w)


def _attn_a_kernel(sink_ref, q_ref, k_ref, v_ref, kc_ref, vc_ref, o_ref, *, n_blocks, n_lat):
    n = pl.program_id(1)
    blk = A_BLOCK
    win = 3 * blk
    start = pl.multiple_of(jnp.clip((n - 1) * blk, 0, n_lat - win), blk)
    keys = jnp.concatenate([k_ref[pl.ds(start, win), :], kc_ref[...]], axis=0)
    vals = jnp.concatenate([v_ref[pl.ds(start, win), :], vc_ref[...]], axis=0)
    n_ctx = kc_ref.shape[0]
    qbase = jnp.where(n < n_blocks, n * blk, -(1 << 20))
    qpos = qbase + lax.broadcasted_iota(jnp.int32, (blk, win + n_ctx), 0)
    kidx = lax.broadcasted_iota(jnp.int32, (blk, win + n_ctx), 1)
    in_band = jnp.abs(start + kidx - qpos) <= A_WINDOW
    bias1 = jnp.where((kidx >= win) | in_band, 0.0, MASK_VALUE).astype(F32)
    bias = jnp.concatenate([bias1, bias1], axis=0)
    lane = _lane_iota((blk, LANES))
    lo = lane < HEAD_DIM
    zero = jnp.zeros((blk, LANES), BF16)
    row_hi = lax.broadcasted_iota(jnp.int32, (2 * blk, 1), 0) >= blk
    for j in range(A_HEADS // 2):
        qc = q_ref[:, j * LANES:(j + 1) * LANES]
        qq = jnp.concatenate([jnp.where(lo, qc, zero), jnp.where(lo, zero, qc)], axis=0)
        s = lax.dot_general(qq, keys, (((1,), (1,)), ((), ())), preferred_element_type=F32) + bias
        sink = jnp.where(row_hi, sink_ref[j + A_HEADS // 2], sink_ref[j])
        m = jnp.maximum(jnp.max(s, axis=-1, keepdims=True), sink)
        p = jnp.exp(s - m)
        l = jnp.sum(p, axis=-1, keepdims=True) + jnp.exp(sink - m)
        o = jnp.dot(p.astype(BF16), vals, preferred_element_type=F32) * (1.0 / l)
        o_ref[:, j * LANES:(j + 1) * LANES] = jnp.where(lo, o[:blk], o[blk:]).astype(BF16)


def _attn_a(q, k, v, sink, n_batch, n_lat, n_ctx, with_ctx):
    t_rows = q.shape[0]
    blk = A_BLOCK
    nb = n_lat // blk
    ncb = n_ctx // blk if with_ctx else 0
    qrow = lambda b, n: jnp.where(n < nb, b * nb + n, n_batch * nb + b * (n_ctx // blk) + (n - nb))
    ctx_blk = lambda b: (n_batch * n_lat) // n_ctx + b
    grid_spec = pltpu.PrefetchScalarGridSpec(
        num_scalar_prefetch=1,
        grid=(n_batch, nb + ncb),
        in_specs=[
            pl.BlockSpec((blk, A_Q), lambda b, n, s: (qrow(b, n), 0)),
            pl.BlockSpec((n_lat, A_KV), lambda b, n, s: (b, 0)),
            pl.BlockSpec((n_lat, A_KV), lambda b, n, s: (b, 0)),
            pl.BlockSpec((n_ctx, A_KV), lambda b, n, s: (ctx_blk(b), 0)),
            pl.BlockSpec((n_ctx, A_KV), lambda b, n, s: (ctx_blk(b), 0)),
        ],
        out_specs=pl.BlockSpec((blk, A_Q), lambda b, n, s: (qrow(b, n), 0)),
    )
    return pl.pallas_call(
        functools.partial(_attn_a_kernel, n_blocks=nb, n_lat=n_lat),
        grid_spec=grid_spec,
        out_shape=jax.ShapeDtypeStruct((t_rows if with_ctx else n_batch * n_lat, A_Q), BF16),
        compiler_params=pltpu.CompilerParams(vmem_limit_bytes=VMEM_LIMIT),
    )(sink, q, k, v, k, v)


def _attn_b_kernel(q_ref, k_ref, v_ref, kc_ref, vc_ref, bias_ref, o_ref, *, rows, n_steps):
    r = pl.program_id(2)
    nq = B_QROWS * GRID_W
    nk = B_KROWS * GRID_W
    r0 = jnp.where(r < n_steps, r * B_QROWS, 0)
    ws = jnp.clip(r0 - NA_KH // 2, 0, rows - B_KROWS)
    start = pl.multiple_of(ws * GRID_W, GRID_W)
    kwin = k_ref[pl.ds(start, nk), :]
    vwin = v_ref[pl.ds(start, nk), :]
    kc = kc_ref[...]
    vc = vc_ref[...]
    q = q_ref[...]
    lane = _lane_iota((nq, LANES))
    lo = lane < HEAD_DIM
    zero = jnp.zeros((nq, LANES), BF16)
    qq = jnp.concatenate([jnp.where(lo, q, zero), jnp.where(lo, zero, q)], axis=0)
    dn = (((1,), (1,)), ((), ()))
    s_loc = lax.dot_general(qq, kwin, dn, preferred_element_type=F32)
    s_loc = s_loc + jnp.concatenate([bias_ref[0], bias_ref[1]], axis=0)
    s_ctx = lax.dot_general(qq, kc, dn, preferred_element_type=F32)
    m = jnp.maximum(jnp.max(s_loc, axis=-1, keepdims=True), jnp.max(s_ctx, axis=-1, keepdims=True))
    p_loc = jnp.exp(s_loc - m)
    p_ctx = jnp.exp(s_ctx - m)
    l = jnp.sum(p_loc, axis=-1, keepdims=True) + jnp.sum(p_ctx, axis=-1, keepdims=True)
    o = (jnp.dot(p_loc.astype(BF16), vwin, preferred_element_type=F32)
         + jnp.dot(p_ctx.astype(BF16), vc, preferred_element_type=F32)) * (1.0 / l)
    o_ref[...] = jnp.where(lo, o[:nq], o[nq:]).astype(BF16)


def _nbr_bias_table(rpb, rows):
    n_steps = rows // B_QROWS
    steps = [min(2, n_steps - 1), 0, 1, n_steps - 2, n_steps - 1]
    w = GRID_W
    rq = np.arange(B_QROWS)[:, None, None, None]
    c = np.arange(w)[None, :, None, None]
    ki = np.arange(B_KROWS)[None, None, :, None]
    kc = np.arange(w)[None, None, None, :]
    tabs = []
    for st in steps:
        r0 = st * B_QROWS
        ws = int(np.clip(r0 - NA_KH // 2, 0, rows - B_KROWS))
        r = r0 + rq
        rs = np.clip(r - NA_KH // 2, 0, rows - NA_KH)
        kr = ws + ki
        cs = np.clip(c - NA_KW // 2, 0, w - NA_KW)
        valid = (kr >= rs) & (kr < rs + NA_KH) & (kc >= cs) & (kc < cs + NA_KW)
        dr = np.clip(kr - r + NA_KH - 1, 0, 2 * NA_KH - 2)
        dc = np.clip(kc - c + NA_KW - 1, 0, 2 * NA_KW - 2)
        shape = (B_QROWS, w, B_KROWS, w)
        dr, dc, valid = (np.broadcast_to(a, shape) for a in (dr, dc, valid))
        vals = rpb[:, dr, dc].astype(F32)
        tab = jnp.where(valid[None], vals, MASK_VALUE)
        tabs.append(tab.reshape(rpb.shape[0], B_QROWS * w, B_KROWS * w))
    tabs.append(jnp.full_like(tabs[0], MASK_VALUE))
    return jnp.stack(tabs)


def _attn_b(q, k, v, bias_tab, n_batch, n_lat, n_ctx, with_ctx):
    t_rows = q.shape[0]
    rows = n_lat // GRID_W
    nq = B_QROWS * GRID_W
    n_steps = rows // B_QROWS
    n_cstep = n_ctx // nq if with_ctx else 0
    qrow = lambda b, r: jnp.where(r < n_steps, b * n_steps + r,
                                  n_batch * n_steps + b * (n_ctx // nq) + (r - n_steps))
    ctx_blk = lambda b: (n_batch * n_lat) // n_ctx + b

    def variant(r):
        v = jnp.where(r == 0, 1, 0)
        v = jnp.where(r == 1, 2, v)
        v = jnp.where(r == n_steps - 2, 3, v)
        v = jnp.where(r == n_steps - 1, 4, v)
        return jnp.where(r >= n_steps, 5, v)

    return pl.pallas_call(
        functools.partial(_attn_b_kernel, rows=rows, n_steps=n_steps),
        grid=(n_batch, B_HEADS // 2, n_steps + n_cstep),
        in_specs=[
            pl.BlockSpec((nq, LANES), lambda b, j, r: (qrow(b, r), j)),
            pl.BlockSpec((n_lat, LANES), lambda b, j, r: (b, j)),
            pl.BlockSpec((n_lat, LANES), lambda b, j, r: (b, j)),
            pl.BlockSpec((n_ctx, LANES), lambda b, j, r: (ctx_blk(b), j)),
            pl.BlockSpec((n_ctx, LANES), lambda b, j, r: (ctx_blk(b), j)),
            pl.BlockSpec((None, 2, nq, B_KROWS * GRID_W), lambda b, j, r: (variant(r), j, 0, 0)),
        ],
        out_specs=pl.BlockSpec((nq, LANES), lambda b, j, r: (qrow(b, r), j)),
        out_shape=jax.ShapeDtypeStruct((t_rows if with_ctx else n_batch * n_lat, B_W), BF16),
        compiler_params=pltpu.CompilerParams(vmem_limit_bytes=VMEM_LIMIT),
    )(q, k, v, k, v, bias_tab)


def _attn_c_kernel(lam_ref, q_ref, k_ref, v_ref, kc_ref, vc_ref, g_ref, o_ref, *, n_qlat, n_kchunks,
                   lam_init):
    i = pl.program_id(2)
    tq = q_ref.shape[0]
    q = q_ref[...]
    lane = _lane_iota((tq, LANES))
    lo = lane < HEAD_DIM
    zero = jnp.zeros((tq, LANES), BF16)
    qq = jnp.concatenate([jnp.where(lo, q, zero), jnp.where(lo, zero, q)], axis=0)
    dn = (((1,), (1,)), ((), ()))

    def step(kk, vv, carry):
        m, l, acc = carry
        s = lax.dot_general(qq, kk, dn, preferred_element_type=F32)
        m_new = jnp.maximum(m, jnp.max(s, axis=-1, keepdims=True))
        alpha = jnp.exp(m - m_new)
        p = jnp.exp(s - m_new)
        l = alpha * l + jnp.sum(p, axis=-1, keepdims=True)
        acc = alpha * acc + jnp.dot(p.astype(BF16), vv, preferred_element_type=F32)
        return m_new, l, acc

    def body(c, carry):
        off = pl.multiple_of(c * C_KCHUNK, C_KCHUNK)
        return step(k_ref[pl.ds(off, C_KCHUNK), :], v_ref[pl.ds(off, C_KCHUNK), :], carry)

    init = (jnp.full((2 * tq, 1), -jnp.inf, F32), jnp.zeros((2 * tq, 1), F32),
            jnp.zeros((2 * tq, LANES), F32))
    n_chunks = jnp.where(i < n_qlat, n_kchunks, 0)
    carry = lax.fori_loop(0, n_chunks, body, init)
    m, l, acc = step(kc_ref[...], vc_ref[...], carry)
    o = acc * (1.0 / l)
    lp = lam_ref[...]
    lam = (jnp.exp(jnp.sum(lp[0:1] * lp[1:2], axis=-1, keepdims=True))
           - jnp.exp(jnp.sum(lp[2:3] * lp[3:4], axis=-1, keepdims=True)) + lam_init)
    od = o[:tq] - lam * o[tq:]
    ms = jnp.mean(od * od, axis=-1, keepdims=True)
    o_ref[...] = ((od * lax.rsqrt(ms + LN_EPS) * g_ref[...]) * (1.0 - lam_init)).astype(BF16)


def _attn_c(q, k, v, lam_p, subln, lam_init, n_batch, n_lat, n_ctx, with_ctx):
    t_rows = q.shape[0]
    tq = C_QTILE
    n_qlat = n_lat // tq
    n_qctx = n_ctx // tq if with_ctx else 0
    qrow = lambda b, i: jnp.where(i < n_qlat, b * n_qlat + i,
                                  n_batch * n_qlat + b * (n_ctx // tq) + (i - n_qlat))
    ctx_blk = lambda b: (n_batch * n_lat) // n_ctx + b
    return pl.pallas_call(
        functools.partial(_attn_c_kernel, n_qlat=n_qlat, n_kchunks=n_lat // C_KCHUNK,
                          lam_init=lam_init),
        grid=(n_batch, C_HEADS, n_qlat + n_qctx),
        in_specs=[
            pl.BlockSpec((4, HEAD_DIM), lambda b, h, i: (0, 0)),
            pl.BlockSpec((tq, LANES), lambda b, h, i: (qrow(b, i), h)),
            pl.BlockSpec((n_lat, LANES), lambda b, h, i: (b, h)),
            pl.BlockSpec((n_lat, LANES), lambda b, h, i: (b, h)),
            pl.BlockSpec((n_ctx, LANES), lambda b, h, i: (ctx_blk(b), h)),
            pl.BlockSpec((n_ctx, LANES), lambda b, h, i: (ctx_blk(b), h)),
            pl.BlockSpec((1, LANES), lambda b, h, i: (0, 0)),
        ],
        out_specs=pl.BlockSpec((tq, LANES), lambda b, h, i: (qrow(b, i), h)),
        out_shape=jax.ShapeDtypeStruct((t_rows if with_ctx else n_batch * n_lat, C_OUT), BF16),
        compiler_params=pltpu.CompilerParams(vmem_limit_bytes=VMEM_LIMIT),
    )(lam_p, q, k, v, k, v, subln.reshape(1, LANES))


def _top2_router(logits_src, wr_ref):
    lg = jnp.dot(logits_src, wr_ref[...], preferred_element_type=F32, precision=lax.Precision.HIGHEST)
    lane = _lane_iota(lg.shape)
    lanef = lane.astype(F32)
    lg = jnp.where(lane < N_EXPERTS, lg, -jnp.inf)
    m1 = jnp.max(lg, axis=-1, keepdims=True)
    i1 = jnp.min(jnp.where(lg == m1, lanef, float(LANES)), axis=-1, keepdims=True)
    lg2 = jnp.where(lanef == i1, -jnp.inf, lg)
    m2 = jnp.max(lg2, axis=-1, keepdims=True)
    i2 = jnp.min(jnp.where(lg2 == m2, lanef, float(LANES)), axis=-1, keepdims=True)
    e = jnp.exp(m2 - m1)
    w1 = 1.0 / (1.0 + e)
    w2 = e / (1.0 + e)
    out = jnp.where(lane == 0, i1, 0.0)
    out = jnp.where(lane == 1, i2, out)
    out = jnp.where(lane == 2, w1, out)
    return jnp.where(lane == 3, w2, out)


def _outproj_kernel(*refs, n_in, router):
    o_refs = refs[:n_in]
    w_refs = refs[n_in:2 * n_in]
    x_ref, g_ref, lng_ref, lnb_ref, sc_ref, sh_ref = refs[2 * n_in:2 * n_in + 6]
    rest = refs[2 * n_in + 6:]
    y = jnp.dot(o_refs[0][...], w_refs[0][...], preferred_element_type=F32)
    for o_r, w_r in zip(o_refs[1:], w_refs[1:]):
        y = y + jnp.dot(o_r[...], w_r[...], preferred_element_type=F32)
    xn = _layernorm_rows(DN_ALPHA * x_ref[...] + g_ref[...] * y, lng_ref[...], lnb_ref[...])
    h2 = xn * (1.0 + sc_ref[...]) + sh_ref[...]
    if router:
        wr_ref, xo_ref, h_ref, r_ref = rest
        r_ref[...] = _top2_router(h2, wr_ref)
    else:
        xo_ref, h_ref = rest
    xo_ref[...] = xn
    h_ref[...] = h2.astype(BF16)


def _out_proj(o_list, w_list, xs, mods_l, ln_g, ln_b, n_tiles, tiles_per_batch, n_batch, w_router=None):
    d = xs.shape[1]
    tm = ROW_TILE
    t_rows = n_tiles * tm
    grp = lambda t: jnp.minimum(t // tiles_per_batch, n_batch)
    mod = lambda k: pl.BlockSpec((None, None, 1, d), lambda t: (grp(t), k, 0, 0))
    in_specs = [pl.BlockSpec((tm, o.shape[1]), lambda t: (t, 0)) for o in o_list]
    in_specs += [pl.BlockSpec(w.shape, lambda t: (0, 0)) for w in w_list]
    in_specs += [pl.BlockSpec((tm, d), lambda t: (t, 0)), mod(2),
                 pl.BlockSpec((1, d), lambda t: (0, 0)), pl.BlockSpec((1, d), lambda t: (0, 0)),
                 mod(4), mod(3)]
    args = list(o_list) + list(w_list) + [xs, mods_l, ln_g.reshape(1, d), ln_b.reshape(1, d), mods_l, mods_l]
    out_shape = [jax.ShapeDtypeStruct((t_rows, d), F32), jax.ShapeDtypeStruct((t_rows, d), BF16)]
    out_specs = [pl.BlockSpec((tm, d), lambda t: (t, 0)), pl.BlockSpec((tm, d), lambda t: (t, 0))]
    if w_router is not None:
        in_specs.append(pl.BlockSpec(w_router.shape, lambda t: (0, 0)))
        args.append(w_router)
        out_shape.append(jax.ShapeDtypeStruct((t_rows, LANES), F32))
        out_specs.append(pl.BlockSpec((tm, LANES), lambda t: (t, 0)))
    return pl.pallas_call(
        functools.partial(_outproj_kernel, n_in=len(o_list), router=w_router is not None),
        grid=(n_tiles,),
        in_specs=in_specs,
        out_specs=out_specs,
        out_shape=out_shape,
        compiler_params=pltpu.CompilerParams(vmem_limit_bytes=VMEM_LIMIT),
    )(*args)


def _swiglu_acc(h_ref, wg_ref, wu_ref, wd_ref, acc_ref, c):
    h = h_ref[...]
    g = jnp.dot(h, wg_ref[...], preferred_element_type=F32)
    u = jnp.dot(h, wu_ref[...], preferred_element_type=F32)
    a = (g * (1.0 / (1.0 + jnp.exp(-g))) * u).astype(BF16)
    part = jnp.dot(a, wd_ref[...], preferred_element_type=F32)

    @pl.when(c == 0)
    def _():
        acc_ref[...] = part

    @pl.when(c > 0)
    def _():
        acc_ref[...] += part


def _ffn_dense_kernel(h_ref, wg_ref, wu_ref, wd_ref, x_ref, g_ref, lng_ref, lnb_ref, xo_ref, acc_ref):
    c = pl.program_id(1)
    _swiglu_acc(h_ref, wg_ref, wu_ref, wd_ref, acc_ref, c)

    @pl.when(c == pl.num_programs(1) - 1)
    def _():
        z = DN_ALPHA * x_ref[...] + g_ref[...] * acc_ref[...]
        xo_ref[...] = _layernorm_rows(z, lng_ref[...], lnb_ref[...])


def _ffn_dense(h2, w_up, w_down, xs, mods_l, ln_g, ln_b, n_tiles, tiles_per_batch, n_batch):
    d = xs.shape[1]
    tm = ROW_TILE
    t_rows = n_tiles * tm
    ff = w_down.shape[0]
    nc = ff // FF_CHUNK
    grp = lambda t: jnp.minimum(t // tiles_per_batch, n_batch)
    return pl.pallas_call(
        _ffn_dense_kernel,
        grid=(n_tiles, nc),
        in_specs=[
            pl.BlockSpec((tm, d), lambda t, c: (t, 0)),
            pl.BlockSpec((d, FF_CHUNK), lambda t, c: (0, c)),
            pl.BlockSpec((d, FF_CHUNK), lambda t, c: (0, nc + c)),
            pl.BlockSpec((FF_CHUNK, d), lambda t, c: (c, 0)),
            pl.BlockSpec((tm, d), lambda t, c: (t, 0)),
            pl.BlockSpec((None, None, 1, d), lambda t, c: (grp(t), 5, 0, 0)),
            pl.BlockSpec((1, d), lambda t, c: (0, 0)),
            pl.BlockSpec((1, d), lambda t, c: (0, 0)),
        ],
        out_specs=pl.BlockSpec((tm, d), lambda t, c: (t, 0)),
        out_shape=jax.ShapeDtypeStruct((t_rows, d), F32),
        scratch_shapes=[pltpu.VMEM((tm, d), F32)],
        compiler_params=pltpu.CompilerParams(vmem_limit_bytes=VMEM_LIMIT),
    )(h2, w_up, w_up, w_down, xs, mods_l, ln_g.reshape(1, d), ln_b.reshape(1, d))


def _ffn_expert_kernel(te_ref, nu_ref, h_ref, wg_ref, wu_ref, wd_ref, y_ref, acc_ref):
    i = pl.program_id(0)
    c = pl.program_id(1)

    @pl.when(i < nu_ref[0])
    def _():
        _swiglu_acc(h_ref, wg_ref, wu_ref, wd_ref, acc_ref, c)

    @pl.when(c == pl.num_programs(1) - 1)
    def _():
        y_ref[...] = acc_ref[...]


def _ffn_experts(hs, w_up, w_down, tile_expert, n_used):
    p_rows, d = hs.shape
    tm = MOE_TILE
    n_tiles = p_rows // tm
    ef = w_down.shape[1]
    nc = ef // EXPERT_FF_CHUNK
    chunk = lambda i, c, nu: jnp.where(i < nu[0], c, nc - 1)
    grid_spec = pltpu.PrefetchScalarGridSpec(
        num_scalar_prefetch=2,
        grid=(n_tiles, nc),
        in_specs=[
            pl.BlockSpec((tm, d), lambda i, c, te, nu: (i, 0)),
            pl.BlockSpec((None, d, EXPERT_FF_CHUNK), lambda i, c, te, nu: (te[i], 0, chunk(i, c, nu))),
            pl.BlockSpec((None, d, EXPERT_FF_CHUNK), lambda i, c, te, nu: (te[i], 0, nc + chunk(i, c, nu))),
            pl.BlockSpec((None, EXPERT_FF_CHUNK, d), lambda i, c, te, nu: (te[i], chunk(i, c, nu), 0)),
        ],
        out_specs=pl.BlockSpec((tm, d), lambda i, c, te, nu: (i, 0)),
        scratch_shapes=[pltpu.VMEM((tm, d), F32)],
    )
    return pl.pallas_call(
        _ffn_expert_kernel,
        grid_spec=grid_spec,
        out_shape=jax.ShapeDtypeStruct((p_rows, d), F32),
        compiler_params=pltpu.CompilerParams(vmem_limit_bytes=VMEM_LIMIT),
    )(tile_expert, n_used, hs, w_up, w_up, w_down)


def _combine_kernel(y0_ref, y1_ref, r_ref, x_ref, g_ref, lng_ref, lnb_ref, xo_ref):
    r = r_ref[...]
    f = r[:, 2:3] * y0_ref[...] + r[:, 3:4] * y1_ref[...]
    z = DN_ALPHA * x_ref[...] + g_ref[...] * f
    xo_ref[...] = _layernorm_rows(z, lng_ref[...], lnb_ref[...])


def _moe_combine(y0, y1, route, xs, mods_l, ln_g, ln_b, n_tiles, tiles_per_batch, n_batch):
    d = xs.shape[1]
    tm = ROW_TILE
    t_rows = n_tiles * tm
    grp = lambda t: jnp.minimum(t // tiles_per_batch, n_batch)
    row = pl.BlockSpec((tm, d), lambda t: (t, 0))
    return pl.pallas_call(
        _combine_kernel,
        grid=(n_tiles,),
        in_specs=[row, row, pl.BlockSpec((tm, LANES), lambda t: (t, 0)), row,
                  pl.BlockSpec((None, None, 1, d), lambda t: (grp(t), 5, 0, 0)),
                  pl.BlockSpec((1, d), lambda t: (0, 0)), pl.BlockSpec((1, d), lambda t: (0, 0))],
        out_specs=row,
        out_shape=jax.ShapeDtypeStruct((t_rows, d), F32),
        compiler_params=pltpu.CompilerParams(vmem_limit_bytes=VMEM_LIMIT),
    )(y0, y1, route, xs, mods_l, ln_g.reshape(1, d), ln_b.reshape(1, d))


def _routing_plan(route, n_rows):
    tm = MOE_TILE
    e_idx = route[:n_rows, 0:2].astype(jnp.int32).reshape(-1)
    onehot = (e_idx[:, None] == jnp.arange(N_EXPERTS, dtype=jnp.int32)[None, :]).astype(jnp.int32)
    csum = jnp.cumsum(onehot, axis=0)
    counts = csum[-1]
    rank = jnp.sum((csum - onehot) * onehot, axis=1)
    padded = ((counts + tm - 1) // tm) * tm
    ends = jnp.cumsum(padded)
    starts = ends - padded
    dest = starts[e_idx] + rank
    n_tiles = (2 * n_rows + N_EXPERTS * (tm - 1)) // tm
    p_rows = n_tiles * tm
    row_token = jnp.zeros((p_rows,), jnp.int32).at[dest].set(jnp.arange(2 * n_rows, dtype=jnp.int32) // 2)
    tile_start = jnp.arange(n_tiles, dtype=jnp.int32) * tm
    tile_expert = jnp.minimum(jnp.sum((tile_start[:, None] >= ends[None, :]).astype(jnp.int32), axis=1),
                              N_EXPERTS - 1)
    n_used = (ends[-1] // tm).astype(jnp.int32).reshape(1)
    last = tile_expert[jnp.maximum(n_used[0] - 1, 0)]
    tile_expert = jnp.where(jnp.arange(n_tiles) < n_used[0], tile_expert, last).astype(jnp.int32)
    return row_token, dest.reshape(n_rows, 2), tile_expert, n_used


def _rope_tables(n_lat):
    t = jnp.arange(n_lat, dtype=jnp.int32)
    row = (t // GRID_W).astype(F32)
    col = (t % GRID_W).astype(F32)
    inv = ROPE_THETA ** (-jnp.arange(0, ROPE_AXIS_DIM, 2, dtype=F32) / ROPE_AXIS_DIM)
    ar = row[:, None] * inv[None, :]
    ac = col[:, None] * inv[None, :]
    ang = jnp.concatenate([ar, ar, ac, ac], axis=-1)
    cos = jnp.tile(jnp.cos(ang), (1, LANES // HEAD_DIM))
    sin = jnp.tile(jnp.sin(ang), (1, LANES // HEAD_DIM))
    sign = jnp.where((jnp.arange(LANES) % 32) < 16, -1.0, 1.0).astype(F32)
    cos = jnp.concatenate([cos, jnp.ones((ROW_TILE, LANES), F32)], axis=0)
    sin = jnp.concatenate([sin * sign[None, :], jnp.zeros((ROW_TILE, LANES), F32)], axis=0)
    return cos, sin


def _lambda_init(layer):
    return 0.8 - 0.6 * math.exp(-0.3 * layer)


def kernel(x, c, ctx, c_ctx, w_mod, b_mod, ln_g, ln_b, w_in_ab, w_out_ab, sink_a, rpb_b, w_in_c, w_out_c,
           lam_c, subln_c, w_ffn_up, w_ffn_down, w_router, w_exp_up, w_exp_down):
    n_batch, n_lat, d = x.shape
    n_ctx = ctx.shape[1]
    assert d == D_MODEL and n_batch * n_ctx == ROW_TILE and n_lat % ROW_TILE == 0
    assert n_batch + 1 <= 8 and n_lat % C_KCHUNK == 0 and n_ctx % C_QTILE == 0
    rows = n_lat // GRID_W
    assert rows >= 12 and rows % B_QROWS == 0
    tiles_per_batch = n_lat // ROW_TILE
    lat_tiles = n_batch * tiles_per_batch
    lat_rows = n_batch * n_lat

    xs = jnp.concatenate([x.reshape(lat_rows, d), ctx.reshape(n_batch * n_ctx, d)], axis=0)
    cond = jnp.zeros((8, d), F32).at[:n_batch].set(c).at[n_batch].set(c_ctx)
    mods = _mod_vectors(cond, w_mod, b_mod).reshape(DEPTH, 8, 6, 1, d)
    cos_t, sin_t = _rope_tables(n_lat)

    perm = np.array([(j + (A_HEADS // 2) * half) * HEAD_DIM + dd
                     for j in range(A_HEADS // 2) for half in range(2) for dd in range(HEAD_DIM)])

    for l in range(DEPTH):
        last = l == DEPTH - 1
        i = l // 2
        all_tiles = lat_tiles + 1
        n_tiles = lat_tiles if last else all_tiles
        mods_l = mods[l]
        if l % 2 == 0:
            w_in = w_in_ab[i]
            w_in = jnp.concatenate([w_in[:, :A_Q][:, perm], w_in[:, A_Q:]], axis=1).astype(BF16)
            segs = [(A_Q, True, ATTN_SCALE), (A_KV, True, 1.0), (A_KV, False, 1.0),
                    (B_W, False, ATTN_SCALE), (B_W, False, 1.0), (B_W, False, 1.0)]
            qa, ka, va, qb, kb, vb = _in_proj(xs, mods_l, cos_t, sin_t, w_in, segs, all_tiles,
                                              tiles_per_batch, n_batch)
            sink = sink_a[i].astype(F32)
            oa = _attn_a(qa, ka, va, sink, n_batch, n_lat, n_ctx, not last)
            ob = _attn_b(qb, kb, vb, _nbr_bias_table(rpb_b[i], rows), n_batch, n_lat, n_ctx, not last)
            w_out = w_out_ab[i]
            o_list = [oa, ob]
            w_list = [w_out[:A_Q][perm].astype(BF16), w_out[A_Q:].astype(BF16)]
        else:
            segs = [(C_QK, True, ATTN_SCALE), (C_QK, True, 1.0), (C_OUT, False, 1.0)]
            qc, kc, vc = _in_proj(xs, mods_l, cos_t, sin_t, w_in_c[i].astype(BF16), segs, all_tiles,
                                  tiles_per_batch, n_batch)
            oc = _attn_c(qc, kc, vc, lam_c[i].astype(F32), subln_c[i].astype(F32), _lambda_init(l),
                         n_batch, n_lat, n_ctx, not last)
            o_list = [oc]
            w_list = [w_out_c[i].astype(BF16)]

        if l % 2 == 0:
            xs, h2 = _out_proj(o_list, w_list, xs, mods_l, ln_g[l, 0], ln_b[l, 0], n_tiles,
                               tiles_per_batch, n_batch)
            xs = _ffn_dense(h2, w_ffn_up[i].astype(BF16), w_ffn_down[i].astype(BF16), xs, mods_l,
                            ln_g[l, 1], ln_b[l, 1], n_tiles, tiles_per_batch, n_batch)
        else:
            wr = jnp.zeros((d, LANES), F32).at[:, :N_EXPERTS].set(w_router[i])
            xs, h2, route = _out_proj(o_list, w_list, xs, mods_l, ln_g[l, 0], ln_b[l, 0], n_tiles,
                                      tiles_per_batch, n_batch, w_router=wr)
            n_rows = n_tiles * ROW_TILE
            row_token, dest, tile_expert, n_used = _routing_plan(route, n_rows)
            hs = jnp.take(h2, row_token, axis=0)
            ys = _ffn_experts(hs, w_exp_up[i].astype(BF16), w_exp_down[i].astype(BF16), tile_expert, n_used)
            y0 = jnp.take(ys, dest[:, 0], axis=0)
            y1 = jnp.take(ys, dest[:, 1], axis=0)
            xs = _moe_combine(y0, y1, route, xs, mods_l, ln_g[l, 1], ln_b[l, 1], n_tiles,
                              tiles_per_batch, n_batch)
    return xs[:lat_rows].reshape(n_batch, n_lat, d)
```

```python
import functools
import math

import jax
import jax.numpy as jnp
import numpy as np
from jax import lax
from jax.experimental import pallas as pl
from jax.experimental.pallas import tpu as pltpu

F32 = jnp.float32
BF16 = jnp.bfloat16

D_MODEL = 1024
DEPTH = 4
GRID_W = 64
HEAD_DIM = 64
LANES = 128
ATTN_SCALE = HEAD_DIM ** -0.5
A_BLOCK = 128
A_WINDOW = 128
A_HEADS = 8
A_KV_HEADS = 2
B_HEADS = 8
NA_KH = 8
NA_KW = 16
C_HEADS = 8
ROPE_THETA = 10000.0
ROPE_AXIS_DIM = HEAD_DIM // 2
FF_DIM = 2816
N_EXPERTS = 8
EXPERT_FF = 3584
LN_EPS = 1e-5
DN_ALPHA = (2 * DEPTH) ** 0.25
MASK_VALUE = -1e30
A_Q = A_HEADS * HEAD_DIM
A_KV = A_KV_HEADS * HEAD_DIM
B_W = B_HEADS * HEAD_DIM
C_QK = 1024
C_OUT = 1024

ROW_TILE = 512
MOE_TILE = 512
FF_CHUNK = 256
EXPERT_FF_CHUNK = 512
B_QROWS = 2
B_KROWS = B_QROWS + NA_KH - 1
C_QTILE = 256
C_KCHUNK = 512
VMEM_LIMIT = 56 * 1024 * 1024


def _lane_iota(shape):
    return lax.broadcasted_iota(jnp.int32, shape, len(shape) - 1)


def _layernorm_rows(z, g, b):
    mu = jnp.mean(z, axis=-1, keepdims=True)
    zc = z - mu
    var = jnp.mean(zc * zc, axis=-1, keepdims=True)
    return zc * lax.rsqrt(var + LN_EPS) * g + b


def _mod_kernel(c_ref, w_ref, b_ref, o_ref):
    c = c_ref[...]
    s = c * (1.0 / (1.0 + jnp.exp(-c)))
    o_ref[0] = jnp.dot(s, w_ref[0], preferred_element_type=F32,
                       precision=lax.Precision.HIGHEST) + b_ref[0]


def _mod_vectors(cond, w_mod, b_mod):
    depth, d, n6 = w_mod.shape
    tn = 1536
    return pl.pallas_call(
        _mod_kernel,
        grid=(depth, n6 // tn),
        in_specs=[
            pl.BlockSpec((8, d), lambda l, j: (0, 0)),
            pl.BlockSpec((1, d, tn), lambda l, j: (l, 0, j)),
            pl.BlockSpec((1, 1, tn), lambda l, j: (l, 0, j)),
        ],
        out_specs=pl.BlockSpec((1, 8, tn), lambda l, j: (l, 0, j)),
        out_shape=jax.ShapeDtypeStruct((depth, 8, n6), F32),
        compiler_params=pltpu.CompilerParams(vmem_limit_bytes=VMEM_LIMIT),
    )(cond, w_mod, b_mod.reshape(depth, 1, n6))


def _rope_slab(a, cos, sin_signed, low16):
    fwd = pltpu.roll(a, LANES - 16, axis=1)
    bwd = pltpu.roll(a, 16, axis=1)
    return a * cos + jnp.where(low16, fwd, bwd) * sin_signed


def _proj_kernel(x_ref, sc_ref, sh_ref, cos_ref, sin_ref, w_ref, *out_refs, segments):
    x = x_ref[...]
    h = (x * (1.0 + sc_ref[...]) + sh_ref[...]).astype(BF16)
    cos = cos_ref[...]
    sin = sin_ref[...]
    low16 = (_lane_iota(cos.shape) % 32) < 16
    col = 0
    for o_ref, (width, rope, scale) in zip(out_refs, segments):
        for c0 in range(0, width, 256):
            cw = min(256, width - c0)
            acc = jnp.dot(h, w_ref[:, col + c0:col + c0 + cw], preferred_element_type=F32)
            slabs = []
            for s0 in range(0, cw, LANES):
                a = acc[:, s0:s0 + LANES]
                if rope:
                    a = _rope_slab(a, cos, sin, low16)
                if scale != 1.0:
                    a = a * scale
                slabs.append(a.astype(BF16))
            o_ref[:, c0:c0 + cw] = slabs[0] if len(slabs) == 1 else jnp.concatenate(slabs, axis=1)
        col += width


def _in_proj(xs, mods_l, cos_t, sin_t, w, segments, n_tiles, tiles_per_batch, n_batch):
    t_rows, d = xs.shape
    tm = ROW_TILE
    grp = lambda t: jnp.minimum(t // tiles_per_batch, n_batch)
    pos = lambda t: jnp.where(t < n_batch * tiles_per_batch, t % tiles_per_batch, tiles_per_batch)
    out_shape = [jax.ShapeDtypeStruct((t_rows, s[0]), BF16) for s in segments]
    out_specs = [pl.BlockSpec((tm, s[0]), lambda t: (t, 0)) for s in segments]
    return pl.pallas_call(
        functools.partial(_proj_kernel, segments=tuple(segments)),
        grid=(n_tiles,),
        in_specs=[
            pl.BlockSpec((tm, d), lambda t: (t, 0)),
            pl.BlockSpec((None, None, 1, d), lambda t: (grp(t), 1, 0, 0)),
            pl.BlockSpec((None, None, 1, d), lambda t: (grp(t), 0, 0, 0)),
            pl.BlockSpec((tm, LANES), lambda t: (pos(t), 0)),
            pl.BlockSpec((tm, LANES), lambda t: (pos(t), 0)),
            pl.BlockSpec(w.shape, lambda t: (0, 0)),
        ],
        out_specs=out_specs,
        out_shape=out_shape,
        compiler_params=pltpu.CompilerParams(vmem_limit_bytes=VMEM_LIMIT),
    )(xs, mods_l, mods_l, cos_t, sin_t, w)


def _attn_a_kernel(sink_ref, q_ref, k_ref, v_ref, kc_ref, vc_ref, o_ref, *, n_blocks, n_lat):
    n = pl.program_id(1)
    blk = A_BLOCK
    win = 3 * blk
    start = pl.multiple_of(jnp.clip((n - 1) * blk, 0, n_lat - win), blk)
    keys = jnp.concatenate([k_ref[pl.ds(start, win), :], kc_ref[...]], axis=0)
    vals = jnp.concatenate([v_ref[pl.ds(start, win), :], vc_ref[...]], axis=0)
    n_ctx = kc_ref.shape[0]
    qbase = jnp.where(n < n_blocks, n * blk, -(1 << 20))
    qpos = qbase + lax.broadcasted_iota(jnp.int32, (blk, win + n_ctx), 0)
    kidx = lax.broadcasted_iota(jnp.int32, (blk, win + n_ctx), 1)
    in_band = jnp.abs(start + kidx - qpos) <= A_WINDOW
    bias1 = jnp.where((kidx >= win) | in_band, 0.0, MASK_VALUE).astype(F32)
    bias = jnp.concatenate([bias1, bias1], axis=0)
    lane = _lane_iota((blk, LANES))
    lo = lane < HEAD_DIM
    zero = jnp.zeros((blk, LANES), BF16)
    row_hi = lax.broadcasted_iota(jnp.int32, (2 * blk, 1), 0) >= blk
    for j in range(A_HEADS // 2):
        qc = q_ref[:, j * LANES:(j + 1) * LANES]
        qq = jnp.concatenate([jnp.where(lo, qc, zero), jnp.where(lo, zero, qc)], axis=0)
        s = lax.dot_general(qq, keys, (((1,), (1,)), ((), ())), preferred_element_type=F32) + bias
        sink = jnp.where(row_hi, sink_ref[j + A_HEADS // 2], sink_ref[j])
        m = jnp.maximum(jnp.max(s, axis=-1, keepdims=True), sink)
        p = jnp.exp(s - m)
        l = jnp.sum(p, axis=-1, keepdims=True) + jnp.exp(sink - m)
        o = jnp.dot(p.astype(BF16), vals, preferred_element_type=F32) * (1.0 / l)
        o_ref[:, j * LANES:(j + 1) * LANES] = jnp.where(lo, o[:blk], o[blk:]).astype(BF16)


def _attn_a(q, k, v, sink, n_batch, n_lat, n_ctx, with_ctx):
    t_rows = q.shape[0]
    blk = A_BLOCK
    nb = n_lat // blk
    ncb = n_ctx // blk if with_ctx else 0
    qrow = lambda b, n: jnp.where(n < nb, b * nb + n, n_batch * nb + b * (n_ctx // blk) + (n - nb))
    ctx_blk = lambda b: (n_batch * n_lat) // n_ctx + b
    grid_spec = pltpu.PrefetchScalarGridSpec(
        num_scalar_prefetch=1,
        grid=(n_batch, nb + ncb),
        in_specs=[
            pl.BlockSpec((blk, A_Q), lambda b, n, s: (qrow(b, n), 0)),
            pl.BlockSpec((n_lat, A_KV), lambda b, n, s: (b, 0)),
            pl.BlockSpec((n_lat, A_KV), lambda b, n, s: (b, 0)),
            pl.BlockSpec((n_ctx, A_KV), lambda b, n, s: (ctx_blk(b), 0)),
            pl.BlockSpec((n_ctx, A_KV), lambda b, n, s: (ctx_blk(b), 0)),
        ],
        out_specs=pl.BlockSpec((blk, A_Q), lambda b, n, s: (qrow(b, n), 0)),
    )
    return pl.pallas_call(
        functools.partial(_attn_a_kernel, n_blocks=nb, n_lat=n_lat),
        grid_spec=grid_spec,
        out_shape=jax.ShapeDtypeStruct((t_rows if with_ctx else n_batch * n_lat, A_Q), BF16),
        compiler_params=pltpu.CompilerParams(vmem_limit_bytes=VMEM_LIMIT),
    )(sink, q, k, v, k, v)


def _attn_b_kernel(q_ref, k_ref, v_ref, kc_ref, vc_ref, bias_ref, o_ref, *, rows, n_steps):
    r = pl.program_id(2)
    nq = B_QROWS * GRID_W
    nk = B_KROWS * GRID_W
    r0 = jnp.where(r < n_steps, r * B_QROWS, 0)
    ws = jnp.clip(r0 - NA_KH // 2, 0, rows - B_KROWS)
    start = pl.multiple_of(ws * GRID_W, GRID_W)
    kwin = k_ref[pl.ds(start, nk), :]
    vwin = v_ref[pl.ds(start, nk), :]
    kc = kc_ref[...]
    vc = vc_ref[...]
    q = q_ref[...]
    lane = _lane_iota((nq, LANES))
    lo = lane < HEAD_DIM
    zero = jnp.zeros((nq, LANES), BF16)
    qq = jnp.concatenate([jnp.where(lo, q, zero), jnp.where(lo, zero, q)], axis=0)
    dn = (((1,), (1,)), ((), ()))
    s_loc = lax.dot_general(qq, kwin, dn, preferred_element_type=F32)
    s_loc = s_loc + jnp.concatenate([bias_ref[0], bias_ref[1]], axis=0)
    s_ctx = lax.dot_general(qq, kc, dn, preferred_element_type=F32)
    m = jnp.maximum(jnp.max(s_loc, axis=-1, keepdims=True), jnp.max(s_ctx, axis=-1, keepdims=True))
    p_loc = jnp.exp(s_loc - m)
    p_ctx = jnp.exp(s_ctx - m)
    l = jnp.sum(p_loc, axis=-1, keepdims=True) + jnp.sum(p_ctx, axis=-1, keepdims=True)
    o = (jnp.dot(p_loc.astype(BF16), vwin, preferred_element_type=F32)
         + jnp.dot(p_ctx.astype(BF16), vc, preferred_element_type=F32)) * (1.0 / l)
    o_ref[...] = jnp.where(lo, o[:nq], o[nq:]).astype(BF16)


def _nbr_bias_table(rpb, rows):
    n_steps = rows // B_QROWS
    steps = [min(2, n_steps - 1), 0, 1, n_steps - 2, n_steps - 1]
    w = GRID_W
    n_heads = rpb.shape[0]
    c = np.arange(w)[:, None]
    kc = np.arange(w)[None, :]
    cs = np.clip(c - NA_KW // 2, 0, w - NA_KW)
    col_ok = (kc >= cs) & (kc < cs + NA_KW)
    onehot = (((kc - c + NA_KW - 1)[None] == np.arange(2 * NA_KW - 1)[:, None, None]) & col_ok[None])
    toe = jnp.einsum("hrd,dck->hrck", rpb.astype(F32), jnp.asarray(onehot, F32),
                     precision=lax.Precision.HIGHEST)
    toe = jnp.where(col_ok[None, None], toe, MASK_VALUE)
    masked = jnp.full((n_heads, w, w), MASK_VALUE, F32)
    tabs = []
    for st in steps:
        r0 = st * B_QROWS
        ws = int(np.clip(r0 - NA_KH // 2, 0, rows - B_KROWS))
        q_rows = []
        for rq in range(B_QROWS):
            r = r0 + rq
            rs = int(np.clip(r - NA_KH // 2, 0, rows - NA_KH))
            blocks = [toe[:, ws + ki - r + NA_KH - 1] if rs <= ws + ki < rs + NA_KH else masked
                      for ki in range(B_KROWS)]
            q_rows.append(jnp.concatenate(blocks, axis=-1))
        tabs.append(jnp.concatenate(q_rows, axis=1))
    tabs.append(jnp.full_like(tabs[0], MASK_VALUE))
    return jnp.stack(tabs)


def _attn_b(q, k, v, bias_tab, n_batch, n_lat, n_ctx, with_ctx):
    t_rows = q.shape[0]
    rows = n_lat // GRID_W
    nq = B_QROWS * GRID_W
    n_steps = rows // B_QROWS
    n_cstep = n_ctx // nq if with_ctx else 0
    qrow = lambda b, r: jnp.where(r < n_steps, b * n_steps + r,
                                  n_batch * n_steps + b * (n_ctx // nq) + (r - n_steps))
    ctx_blk = lambda b: (n_batch * n_lat) // n_ctx + b

    def variant(r):
        v = jnp.where(r == 0, 1, 0)
        v = jnp.where(r == 1, 2, v)
        v = jnp.where(r == n_steps - 2, 3, v)
        v = jnp.where(r == n_steps - 1, 4, v)
        return jnp.where(r >= n_steps, 5, v)

    return pl.pallas_call(
        functools.partial(_attn_b_kernel, rows=rows, n_steps=n_steps),
        grid=(n_batch, B_HEADS // 2, n_steps + n_cstep),
        in_specs=[
            pl.BlockSpec((nq, LANES), lambda b, j, r: (qrow(b, r), j)),
            pl.BlockSpec((n_lat, LANES), lambda b, j, r: (b, j)),
            pl.BlockSpec((n_lat, LANES), lambda b, j, r: (b, j)),
            pl.BlockSpec((n_ctx, LANES), lambda b, j, r: (ctx_blk(b), j)),
            pl.BlockSpec((n_ctx, LANES), lambda b, j, r: (ctx_blk(b), j)),
            pl.BlockSpec((None, 2, nq, B_KROWS * GRID_W), lambda b, j, r: (variant(r), j, 0, 0)),
        ],
        out_specs=pl.BlockSpec((nq, LANES), lambda b, j, r: (qrow(b, r), j)),
        out_shape=jax.ShapeDtypeStruct((t_rows if with_ctx else n_batch * n_lat, B_W), BF16),
        compiler_params=pltpu.CompilerParams(vmem_limit_bytes=VMEM_LIMIT),
    )(q, k, v, k, v, bias_tab)


def _attn_c_kernel(lam_ref, q_ref, k_ref, vt_ref, kc_ref, vct_ref, g_ref, o_ref,
                   s_a, s_b, s_c, m_scr, l_scr, acc_scr, *, n_qlat, n_kchunks, lam_init):
    i = pl.program_id(2)
    tq = q_ref.shape[0]
    tk = C_KCHUNK
    q = q_ref[...]
    lo = _lane_iota((tq, LANES)) < HEAD_DIM
    zero = jnp.zeros((tq, LANES), BF16)
    q_maps = (jnp.where(lo, q, zero), jnp.where(lo, zero, q))
    dn = (((1,), (1,)), ((), ()))

    def scores(kk, s_ref):
        for mi in range(2):
            s_ref[mi] = lax.dot_general(kk, q_maps[mi], dn, preferred_element_type=F32)

    def softmax_pv(s_ref, vvt):
        for mi in range(2):
            s = s_ref[mi]
            m = m_scr[mi]
            m_new = jnp.maximum(m, jnp.max(s, axis=0, keepdims=True))
            alpha = jnp.exp2(m - m_new)
            p = jnp.exp2(s - m_new)
            l_scr[mi] = alpha * l_scr[mi] + jnp.sum(p, axis=0, keepdims=True)
            acc_scr[mi] = alpha * acc_scr[mi] + jnp.dot(vvt, p.astype(BF16),
                                                        preferred_element_type=F32)
            m_scr[mi] = m_new

    def k_chunk(c):
        return k_ref[pl.ds(pl.multiple_of(c * tk, tk), tk), :]

    def vt_chunk(c):
        return vt_ref[:, pl.ds(pl.multiple_of(c * tk, tk), tk)]

    m_scr[...] = jnp.full(m_scr.shape, -jnp.inf, F32)
    l_scr[...] = jnp.zeros(l_scr.shape, F32)
    acc_scr[...] = jnp.zeros(acc_scr.shape, F32)

    @pl.when(i < n_qlat)
    def _():
        scores(k_chunk(0), s_a)

        def body(jj, carry):
            c = 2 * jj
            scores(k_chunk(c + 1), s_b)
            softmax_pv(s_a, vt_chunk(c))
            scores(k_chunk(c + 2), s_a)
            softmax_pv(s_b, vt_chunk(c + 1))
            return carry

        lax.fori_loop(0, n_kchunks // 2 - 1, body, 0)
        scores(k_chunk(n_kchunks - 1), s_b)
        softmax_pv(s_a, vt_chunk(n_kchunks - 2))
        scores(kc_ref[...], s_c)
        softmax_pv(s_b, vt_chunk(n_kchunks - 1))

    @pl.when(i >= n_qlat)
    def _():
        scores(kc_ref[...], s_c)

    softmax_pv(s_c, vct_ref[...])
    lp = lam_ref[...]
    lam = (jnp.exp(jnp.sum(lp[0:1] * lp[1:2], axis=-1, keepdims=True))
           - jnp.exp(jnp.sum(lp[2:3] * lp[3:4], axis=-1, keepdims=True)) + lam_init)
    od = acc_scr[0] * (1.0 / l_scr[0]) - lam * (acc_scr[1] * (1.0 / l_scr[1]))
    ms = jnp.mean(od * od, axis=0, keepdims=True)
    on = (od * lax.rsqrt(ms + LN_EPS)).T
    o_ref[...] = ((on * g_ref[...]) * (1.0 - lam_init)).astype(BF16)


def _attn_c(q, k, vt, lam_p, subln, lam_init, n_batch, n_lat, n_ctx, with_ctx):
    t_rows = q.shape[0]
    tq = C_QTILE
    n_qlat = n_lat // tq
    n_qctx = n_ctx // tq if with_ctx else 0
    qrow = lambda b, i: jnp.where(i < n_qlat, b * n_qlat + i,
                                  n_batch * n_qlat + b * (n_ctx // tq) + (i - n_qlat))
    ctx_blk = lambda b: (n_batch * n_lat) // n_ctx + b
    return pl.pallas_call(
        functools.partial(_attn_c_kernel, n_qlat=n_qlat, n_kchunks=n_lat // C_KCHUNK,
                          lam_init=lam_init),
        grid=(n_batch, C_HEADS, n_qlat + n_qctx),
        in_specs=[
            pl.BlockSpec((4, HEAD_DIM), lambda b, h, i: (0, 0)),
            pl.BlockSpec((tq, LANES), lambda b, h, i: (qrow(b, i), h)),
            pl.BlockSpec((n_lat, LANES), lambda b, h, i: (b, h)),
            pl.BlockSpec((LANES, n_lat), lambda b, h, i: (h, b)),
            pl.BlockSpec((n_ctx, LANES), lambda b, h, i: (ctx_blk(b), h)),
            pl.BlockSpec((LANES, n_ctx), lambda b, h, i: (h, ctx_blk(b))),
            pl.BlockSpec((1, LANES), lambda b, h, i: (0, 0)),
        ],
        out_specs=pl.BlockSpec((tq, LANES), lambda b, h, i: (qrow(b, i), h)),
        out_shape=jax.ShapeDtypeStruct((t_rows if with_ctx else n_batch * n_lat, C_OUT), BF16),
        scratch_shapes=[
            pltpu.VMEM((2, C_KCHUNK, tq), F32), pltpu.VMEM((2, C_KCHUNK, tq), F32),
            pltpu.VMEM((2, n_ctx, tq), F32),
            pltpu.VMEM((2, 1, tq), F32), pltpu.VMEM((2, 1, tq), F32), pltpu.VMEM((2, LANES, tq), F32),
        ],
        compiler_params=pltpu.CompilerParams(vmem_limit_bytes=VMEM_LIMIT),
    )(lam_p, q, k, vt, k, vt, subln.reshape(1, LANES))


def _top2_router(logits_src, wr_ref):
    lg = jnp.dot(logits_src, wr_ref[...], preferred_element_type=F32, precision=lax.Precision.HIGHEST)
    lane = _lane_iota(lg.shape)
    lanef = lane.astype(F32)
    lg = jnp.where(lane < N_EXPERTS, lg, -jnp.inf)
    m1 = jnp.max(lg, axis=-1, keepdims=True)
    i1 = jnp.min(jnp.where(lg == m1, lanef, float(LANES)), axis=-1, keepdims=True)
    lg2 = jnp.where(lanef == i1, -jnp.inf, lg)
    m2 = jnp.max(lg2, axis=-1, keepdims=True)
    i2 = jnp.min(jnp.where(lg2 == m2, lanef, float(LANES)), axis=-1, keepdims=True)
    e = jnp.exp(m2 - m1)
    w1 = 1.0 / (1.0 + e)
    w2 = e / (1.0 + e)
    out = jnp.where(lane == 0, i1, 0.0)
    out = jnp.where(lane == 1, i2, out)
    out = jnp.where(lane == 2, w1, out)
    return jnp.where(lane == 3, w2, out)


def _outproj_kernel(*refs, n_in, router):
    o_refs = refs[:n_in]
    w_refs = refs[n_in:2 * n_in]
    x_ref, g_ref, lng_ref, lnb_ref, sc_ref, sh_ref = refs[2 * n_in:2 * n_in + 6]
    rest = refs[2 * n_in + 6:]
    y = jnp.dot(o_refs[0][...], w_refs[0][...], preferred_element_type=F32)
    for o_r, w_r in zip(o_refs[1:], w_refs[1:]):
        y = y + jnp.dot(o_r[...], w_r[...], preferred_element_type=F32)
    xn = _layernorm_rows(DN_ALPHA * x_ref[...] + g_ref[...] * y, lng_ref[...], lnb_ref[...])
    h2 = xn * (1.0 + sc_ref[...]) + sh_ref[...]
    if router:
        wr_ref, xo_ref, h_ref, r_ref = rest
        r_ref[...] = _top2_router(h2, wr_ref)
    else:
        xo_ref, h_ref = rest
    xo_ref[...] = xn
    h_ref[...] = h2.astype(BF16)


def _out_proj(o_list, w_list, xs, mods_l, ln_g, ln_b, n_tiles, tiles_per_batch, n_batch, w_router=None):
    d = xs.shape[1]
    tm = ROW_TILE
    t_rows = n_tiles * tm
    grp = lambda t: jnp.minimum(t // tiles_per_batch, n_batch)
    mod = lambda k: pl.BlockSpec((None, None, 1, d), lambda t: (grp(t), k, 0, 0))
    in_specs = [pl.BlockSpec((tm, o.shape[1]), lambda t: (t, 0)) for o in o_list]
    in_specs += [pl.BlockSpec(w.shape, lambda t: (0, 0)) for w in w_list]
    in_specs += [pl.BlockSpec((tm, d), lambda t: (t, 0)), mod(2),
                 pl.BlockSpec((1, d), lambda t: (0, 0)), pl.BlockSpec((1, d), lambda t: (0, 0)),
                 mod(4), mod(3)]
    args = list(o_list) + list(w_list) + [xs, mods_l, ln_g.reshape(1, d), ln_b.reshape(1, d), mods_l, mods_l]
    out_shape = [jax.ShapeDtypeStruct((t_rows, d), F32), jax.ShapeDtypeStruct((t_rows, d), BF16)]
    out_specs = [pl.BlockSpec((tm, d), lambda t: (t, 0)), pl.BlockSpec((tm, d), lambda t: (t, 0))]
    if w_router is not None:
        in_specs.append(pl.BlockSpec(w_router.shape, lambda t: (0, 0)))
        args.append(w_router)
        out_shape.append(jax.ShapeDtypeStruct((t_rows, LANES), F32))
        out_specs.append(pl.BlockSpec((tm, LANES), lambda t: (t, 0)))
    return pl.pallas_call(
        functools.partial(_outproj_kernel, n_in=len(o_list), router=w_router is not None),
        grid=(n_tiles,),
        in_specs=in_specs,
        out_specs=out_specs,
        out_shape=out_shape,
        compiler_params=pltpu.CompilerParams(vmem_limit_bytes=VMEM_LIMIT),
    )(*args)


def _swiglu_acc(h_ref, wg_ref, wu_ref, wd_ref, acc_ref, c):
    h = h_ref[...]
    g = jnp.dot(h, wg_ref[...], preferred_element_type=F32)
    u = jnp.dot(h, wu_ref[...], preferred_element_type=F32)
    a = (g * (1.0 / (1.0 + jnp.exp(-g))) * u).astype(BF16)
    part = jnp.dot(a, wd_ref[...], preferred_element_type=F32)

    @pl.when(c == 0)
    def _():
        acc_ref[...] = part

    @pl.when(c > 0)
    def _():
        acc_ref[...] += part


def _ffn_dense_kernel(h_ref, wg_ref, wu_ref, wd_ref, x_ref, g_ref, lng_ref, lnb_ref, xo_ref, acc_ref):
    c = pl.program_id(1)
    _swiglu_acc(h_ref, wg_ref, wu_ref, wd_ref, acc_ref, c)

    @pl.when(c == pl.num_programs(1) - 1)
    def _():
        z = DN_ALPHA * x_ref[...] + g_ref[...] * acc_ref[...]
        xo_ref[...] = _layernorm_rows(z, lng_ref[...], lnb_ref[...])


def _ffn_dense(h2, w_up, w_down, xs, mods_l, ln_g, ln_b, n_tiles, tiles_per_batch, n_batch):
    d = xs.shape[1]
    tm = ROW_TILE
    t_rows = n_tiles * tm
    ff = w_down.shape[0]
    nc = ff // FF_CHUNK
    grp = lambda t: jnp.minimum(t // tiles_per_batch, n_batch)
    return pl.pallas_call(
        _ffn_dense_kernel,
        grid=(n_tiles, nc),
        in_specs=[
            pl.BlockSpec((tm, d), lambda t, c: (t, 0)),
            pl.BlockSpec((d, FF_CHUNK), lambda t, c: (0, c)),
            pl.BlockSpec((d, FF_CHUNK), lambda t, c: (0, nc + c)),
            pl.BlockSpec((FF_CHUNK, d), lambda t, c: (c, 0)),
            pl.BlockSpec((tm, d), lambda t, c: (t, 0)),
            pl.BlockSpec((None, None, 1, d), lambda t, c: (grp(t), 5, 0, 0)),
            pl.BlockSpec((1, d), lambda t, c: (0, 0)),
            pl.BlockSpec((1, d), lambda t, c: (0, 0)),
        ],
        out_specs=pl.BlockSpec((tm, d), lambda t, c: (t, 0)),
        out_shape=jax.ShapeDtypeStruct((t_rows, d), F32),
        scratch_shapes=[pltpu.VMEM((tm, d), F32)],
        compiler_params=pltpu.CompilerParams(vmem_limit_bytes=VMEM_LIMIT),
    )(h2, w_up, w_up, w_down, xs, mods_l, ln_g.reshape(1, d), ln_b.reshape(1, d))


def _ffn_expert_kernel(te_ref, nu_ref, h_ref, wg_ref, wu_ref, wd_ref, y_ref, acc_ref):
    i = pl.program_id(0)
    c = pl.program_id(1)

    @pl.when(i < nu_ref[0])
    def _():
        _swiglu_acc(h_ref, wg_ref, wu_ref, wd_ref, acc_ref, c)

    @pl.when(c == pl.num_programs(1) - 1)
    def _():
        y_ref[...] = acc_ref[...]


def _ffn_experts(hs, w_up, w_down, tile_expert, n_used):
    p_rows, d = hs.shape
    tm = MOE_TILE
    n_tiles = p_rows // tm
    ef = w_down.shape[1]
    nc = ef // EXPERT_FF_CHUNK
    chunk = lambda i, c, nu: jnp.where(i < nu[0], c, nc - 1)
    grid_spec = pltpu.PrefetchScalarGridSpec(
        num_scalar_prefetch=2,
        grid=(n_tiles, nc),
        in_specs=[
            pl.BlockSpec((tm, d), lambda i, c, te, nu: (i, 0)),
            pl.BlockSpec((None, d, EXPERT_FF_CHUNK), lambda i, c, te, nu: (te[i], 0, chunk(i, c, nu))),
            pl.BlockSpec((None, d, EXPERT_FF_CHUNK), lambda i, c, te, nu: (te[i], 0, nc + chunk(i, c, nu))),
            pl.BlockSpec((None, EXPERT_FF_CHUNK, d), lambda i, c, te, nu: (te[i], chunk(i, c, nu), 0)),
        ],
        out_specs=pl.BlockSpec((tm, d), lambda i, c, te, nu: (i, 0)),
        scratch_shapes=[pltpu.VMEM((tm, d), F32)],
    )
    return pl.pallas_call(
        _ffn_expert_kernel,
        grid_spec=grid_spec,
        out_shape=jax.ShapeDtypeStruct((p_rows, d), F32),
        compiler_params=pltpu.CompilerParams(vmem_limit_bytes=VMEM_LIMIT),
    )(tile_expert, n_used, hs, w_up, w_up, w_down)


def _combine_kernel(y0_ref, y1_ref, r_ref, x_ref, g_ref, lng_ref, lnb_ref, xo_ref):
    r = r_ref[...]
    f = r[:, 2:3] * y0_ref[...] + r[:, 3:4] * y1_ref[...]
    z = DN_ALPHA * x_ref[...] + g_ref[...] * f
    xo_ref[...] = _layernorm_rows(z, lng_ref[...], lnb_ref[...])


def _moe_combine(y0, y1, route, xs, mods_l, ln_g, ln_b, n_tiles, tiles_per_batch, n_batch):
    d = xs.shape[1]
    tm = ROW_TILE
    t_rows = n_tiles * tm
    grp = lambda t: jnp.minimum(t // tiles_per_batch, n_batch)
    row = pl.BlockSpec((tm, d), lambda t: (t, 0))
    return pl.pallas_call(
        _combine_kernel,
        grid=(n_tiles,),
        in_specs=[row, row, pl.BlockSpec((tm, LANES), lambda t: (t, 0)), row,
                  pl.BlockSpec((None, None, 1, d), lambda t: (grp(t), 5, 0, 0)),
                  pl.BlockSpec((1, d), lambda t: (0, 0)), pl.BlockSpec((1, d), lambda t: (0, 0))],
        out_specs=row,
        out_shape=jax.ShapeDtypeStruct((t_rows, d), F32),
        compiler_params=pltpu.CompilerParams(vmem_limit_bytes=VMEM_LIMIT),
    )(y0, y1, route, xs, mods_l, ln_g.reshape(1, d), ln_b.reshape(1, d))


def _routing_plan(route, n_rows):
    tm = MOE_TILE
    e_idx = route[:n_rows, 0:2].astype(jnp.int32).reshape(-1)
    onehot = (e_idx[:, None] == jnp.arange(N_EXPERTS, dtype=jnp.int32)[None, :]).astype(jnp.int32)
    csum = jnp.cumsum(onehot, axis=0)
    counts = csum[-1]
    rank = jnp.sum((csum - onehot) * onehot, axis=1)
    padded = ((counts + tm - 1) // tm) * tm
    ends = jnp.cumsum(padded)
    starts = ends - padded
    dest = starts[e_idx] + rank
    n_tiles = (2 * n_rows + N_EXPERTS * (tm - 1)) // tm
    p_rows = n_tiles * tm
    row_token = jnp.zeros((p_rows,), jnp.int32).at[dest].set(jnp.arange(2 * n_rows, dtype=jnp.int32) // 2)
    tile_start = jnp.arange(n_tiles, dtype=jnp.int32) * tm
    tile_expert = jnp.minimum(jnp.sum((tile_start[:, None] >= ends[None, :]).astype(jnp.int32), axis=1),
                              N_EXPERTS - 1)
    n_used = (ends[-1] // tm).astype(jnp.int32).reshape(1)
    last = tile_expert[jnp.maximum(n_used[0] - 1, 0)]
    tile_expert = jnp.where(jnp.arange(n_tiles) < n_used[0], tile_expert, last).astype(jnp.int32)
    return row_token, dest.reshape(n_rows, 2), tile_expert, n_used


def _rope_tables(n_lat):
    t = jnp.arange(n_lat, dtype=jnp.int32)
    row = (t // GRID_W).astype(F32)
    col = (t % GRID_W).astype(F32)
    inv = ROPE_THETA ** (-jnp.arange(0, ROPE_AXIS_DIM, 2, dtype=F32) / ROPE_AXIS_DIM)
    ar = row[:, None] * inv[None, :]
    ac = col[:, None] * inv[None, :]
    ang = jnp.concatenate([ar, ar, ac, ac], axis=-1)
    cos = jnp.tile(jnp.cos(ang), (1, LANES // HEAD_DIM))
    sin = jnp.tile(jnp.sin(ang), (1, LANES // HEAD_DIM))
    sign = jnp.where((jnp.arange(LANES) % 32) < 16, -1.0, 1.0).astype(F32)
    cos = jnp.concatenate([cos, jnp.ones((ROW_TILE, LANES), F32)], axis=0)
    sin = jnp.concatenate([sin * sign[None, :], jnp.zeros((ROW_TILE, LANES), F32)], axis=0)
    return cos, sin


def _lambda_init(layer):
    return 0.8 - 0.6 * math.exp(-0.3 * layer)


def kernel(x, c, ctx, c_ctx, w_mod, b_mod, ln_g, ln_b, w_in_ab, w_out_ab, sink_a, rpb_b, w_in_c, w_out_c,
           lam_c, subln_c, w_ffn_up, w_ffn_down, w_router, w_exp_up, w_exp_down):
    n_batch, n_lat, d = x.shape
    n_ctx = ctx.shape[1]
    assert d == D_MODEL and n_batch * n_ctx == ROW_TILE and n_lat % ROW_TILE == 0
    assert n_batch + 1 <= 8 and n_lat % C_KCHUNK == 0 and n_ctx % C_QTILE == 0
    rows = n_lat // GRID_W
    assert rows >= 12 and rows % B_QROWS == 0
    tiles_per_batch = n_lat // ROW_TILE
    lat_tiles = n_batch * tiles_per_batch
    lat_rows = n_batch * n_lat

    xs = jnp.concatenate([x.reshape(lat_rows, d), ctx.reshape(n_batch * n_ctx, d)], axis=0)
    cond = jnp.zeros((8, d), F32).at[:n_batch].set(c).at[n_batch].set(c_ctx)
    mods = _mod_vectors(cond, w_mod, b_mod).reshape(DEPTH, 8, 6, 1, d)
    cos_t, sin_t = _rope_tables(n_lat)

    perm = np.array([(j + (A_HEADS // 2) * half) * HEAD_DIM + dd
                     for j in range(A_HEADS // 2) for half in range(2) for dd in range(HEAD_DIM)])

    for l in range(DEPTH):
        last = l == DEPTH - 1
        i = l // 2
        all_tiles = lat_tiles + 1
        n_tiles = lat_tiles if last else all_tiles
        mods_l = mods[l]
        if l % 2 == 0:
            w_in = w_in_ab[i]
            w_in = jnp.concatenate([w_in[:, :A_Q][:, perm], w_in[:, A_Q:]], axis=1).astype(BF16)
            segs = [(A_Q, True, ATTN_SCALE), (A_KV, True, 1.0), (A_KV, False, 1.0),
                    (B_W, False, ATTN_SCALE), (B_W, False, 1.0), (B_W, False, 1.0)]
            qa, ka, va, qb, kb, vb = _in_proj(xs, mods_l, cos_t, sin_t, w_in, segs, all_tiles,
                                              tiles_per_batch, n_batch)
            sink = sink_a[i].astype(F32)
            oa = _attn_a(qa, ka, va, sink, n_batch, n_lat, n_ctx, not last)
            ob = _attn_b(qb, kb, vb, _nbr_bias_table(rpb_b[i], rows), n_batch, n_lat, n_ctx, not last)
            w_out = w_out_ab[i]
            o_list = [oa, ob]
            w_list = [w_out[:A_Q][perm].astype(BF16), w_out[A_Q:].astype(BF16)]
        else:
            segs = [(C_QK, True, ATTN_SCALE * math.log2(math.e)), (C_QK, True, 1.0), (C_OUT, False, 1.0)]
            qc, kc, vc = _in_proj(xs, mods_l, cos_t, sin_t, w_in_c[i].astype(BF16), segs, all_tiles,
                                  tiles_per_batch, n_batch)
            oc = _attn_c(qc, kc, vc.T, lam_c[i].astype(F32), subln_c[i].astype(F32), _lambda_init(l),
                         n_batch, n_lat, n_ctx, not last)
            o_list = [oc]
            w_list = [w_out_c[i].astype(BF16)]

        if l % 2 == 0:
            xs, h2 = _out_proj(o_list, w_list, xs, mods_l, ln_g[l, 0], ln_b[l, 0], n_tiles,
                               tiles_per_batch, n_batch)
            xs = _ffn_dense(h2, w_ffn_up[i].astype(BF16), w_ffn_down[i].astype(BF16), xs, mods_l,
                            ln_g[l, 1], ln_b[l, 1], n_tiles, tiles_per_batch, n_batch)
        else:
            wr = jnp.zeros((d, LANES), F32).at[:, :N_EXPERTS].set(w_router[i])
            xs, h2, route = _out_proj(o_list, w_list, xs, mods_l, ln_g[l, 0], ln_b[l, 0], n_tiles,
                                      tiles_per_batch, n_batch, w_router=wr)
            n_rows = n_tiles * ROW_TILE
            row_token, dest, tile_expert, n_used = _routing_plan(route, n_rows)
            hs = jnp.take(h2, row_token, axis=0)
            ys = _ffn_experts(hs, w_exp_up[i].astype(BF16), w_exp_down[i].astype(BF16), tile_expert, n_used)
            y0 = jnp.take(ys, dest[:, 0], axis=0)
            y1 = jnp.take(ys, dest[:, 1], axis=0)
            xs = _moe_combine(y0, y1, route, xs, mods_l, ln_g[l, 1], ln_b[l, 1], n_tiles,
                              tiles_per_batch, n_batch)
    return xs[:lat_rows].reshape(n_batch, n_lat, d)
```

```python
import functools
import math

import jax
import jax.numpy as jnp
import numpy as np
from jax import lax
from jax.experimental import pallas as pl
from jax.experimental.pallas import tpu as pltpu

F32 = jnp.float32
BF16 = jnp.bfloat16

D_MODEL = 1024
DEPTH = 4
GRID_W = 64
HEAD_DIM = 64
LANES = 128
ATTN_SCALE = HEAD_DIM ** -0.5
A_BLOCK = 128
A_WINDOW = 128
A_HEADS = 8
A_KV_HEADS = 2
B_HEADS = 8
NA_KH = 8
NA_KW = 16
C_HEADS = 8
ROPE_THETA = 10000.0
ROPE_AXIS_DIM = HEAD_DIM // 2
FF_DIM = 2816
N_EXPERTS = 8
EXPERT_FF = 3584
LN_EPS = 1e-5
DN_ALPHA = (2 * DEPTH) ** 0.25
MASK_VALUE = -1e30
A_Q = A_HEADS * HEAD_DIM
A_KV = A_KV_HEADS * HEAD_DIM
B_W = B_HEADS * HEAD_DIM
C_QK = 1024
C_OUT = 1024

ROW_TILE = 512
MOE_TILE = 512
FF_CHUNK = 1408
EXPERT_FF_CHUNK = 1792
B_QROWS = 2
B_KROWS = B_QROWS + NA_KH - 1
C_QTILE = 256
C_KCHUNK = 512
C_UNROLL = 4
VMEM_LIMIT = 56 * 1024 * 1024


def _lane_iota(shape):
    return lax.broadcasted_iota(jnp.int32, shape, len(shape) - 1)


def _layernorm_rows(z, g, b):
    mu = jnp.mean(z, axis=-1, keepdims=True)
    zc = z - mu
    var = jnp.mean(zc * zc, axis=-1, keepdims=True)
    return zc * lax.rsqrt(var + LN_EPS) * g + b


def _mod_kernel(c_ref, w_ref, b_ref, o_ref):
    c = c_ref[...]
    s = c * (1.0 / (1.0 + jnp.exp(-c)))
    o_ref[0] = jnp.dot(s, w_ref[0], preferred_element_type=F32,
                       precision=lax.Precision.HIGHEST) + b_ref[0]


def _mod_vectors(cond, w_mod, b_mod):
    depth, d, n6 = w_mod.shape
    tn = 1536
    return pl.pallas_call(
        _mod_kernel,
        grid=(depth, n6 // tn),
        in_specs=[
            pl.BlockSpec((8, d), lambda l, j: (0, 0)),
            pl.BlockSpec((1, d, tn), lambda l, j: (l, 0, j)),
            pl.BlockSpec((1, 1, tn), lambda l, j: (l, 0, j)),
        ],
        out_specs=pl.BlockSpec((1, 8, tn), lambda l, j: (l, 0, j)),
        out_shape=jax.ShapeDtypeStruct((depth, 8, n6), F32),
        compiler_params=pltpu.CompilerParams(vmem_limit_bytes=VMEM_LIMIT),
    )(cond, w_mod, b_mod.reshape(depth, 1, n6))


def _rope_slab(a, cos, sin_signed, low16):
    fwd = pltpu.roll(a, LANES - 16, axis=1)
    bwd = pltpu.roll(a, 16, axis=1)
    return a * cos + jnp.where(low16, fwd, bwd) * sin_signed


def _proj_kernel(x_ref, sc_ref, sh_ref, cos_ref, sin_ref, w_ref, *out_refs, segments):
    x = x_ref[...]
    h = (x * (1.0 + sc_ref[...]) + sh_ref[...]).astype(BF16)
    cos = cos_ref[...]
    sin = sin_ref[...]
    low16 = (_lane_iota(cos.shape) % 32) < 16
    col = 0
    for o_ref, (width, rope, scale) in zip(out_refs, segments):
        for c0 in range(0, width, 256):
            cw = min(256, width - c0)
            acc = jnp.dot(h, w_ref[:, col + c0:col + c0 + cw], preferred_element_type=F32)
            slabs = []
            for s0 in range(0, cw, LANES):
                a = acc[:, s0:s0 + LANES]
                if rope:
                    a = _rope_slab(a, cos, sin, low16)
                if scale != 1.0:
                    a = a * scale
                slabs.append(a.astype(BF16))
            o_ref[:, c0:c0 + cw] = slabs[0] if len(slabs) == 1 else jnp.concatenate(slabs, axis=1)
        col += width


def _in_proj(xs, mods_l, cos_t, sin_t, w, segments, n_tiles, tiles_per_batch, n_batch):
    t_rows, d = xs.shape
    tm = ROW_TILE
    grp = lambda t: jnp.minimum(t // tiles_per_batch, n_batch)
    pos = lambda t: jnp.where(t < n_batch * tiles_per_batch, t % tiles_per_batch, tiles_per_batch)
    out_shape = [jax.ShapeDtypeStruct((t_rows, s[0]), BF16) for s in segments]
    out_specs = [pl.BlockSpec((tm, s[0]), lambda t: (t, 0)) for s in segments]
    return pl.pallas_call(
        functools.partial(_proj_kernel, segments=tuple(segments)),
        grid=(n_tiles,),
        in_specs=[
            pl.BlockSpec((tm, d), lambda t: (t, 0)),
            pl.BlockSpec((None, None, 1, d), lambda t: (grp(t), 1, 0, 0)),
            pl.BlockSpec((None, None, 1, d), lambda t: (grp(t), 0, 0, 0)),
            pl.BlockSpec((tm, LANES), lambda t: (pos(t), 0)),
            pl.BlockSpec((tm, LANES), lambda t: (pos(t), 0)),
            pl.BlockSpec(w.shape, lambda t: (0, 0)),
        ],
        out_specs=out_specs,
        out_shape=out_shape,
        compiler_params=pltpu.CompilerParams(vmem_limit_bytes=VMEM_LIMIT),
    )(xs, mods_l, mods_l, cos_t, sin_t, w)


def _attn_a_kernel(sink_ref, q_ref, k_ref, v_ref, kc_ref, vc_ref, o_ref, *, n_blocks, n_lat):
    n = pl.program_id(1)
    blk = A_BLOCK
    win = 3 * blk
    start = pl.multiple_of(jnp.clip((n - 1) * blk, 0, n_lat - win), blk)
    keys = jnp.concatenate([k_ref[pl.ds(start, win), :], kc_ref[...]], axis=0)
    vals = jnp.concatenate([v_ref[pl.ds(start, win), :], vc_ref[...]], axis=0)
    n_ctx = kc_ref.shape[0]
    qbase = jnp.where(n < n_blocks, n * blk, -(1 << 20))
    qpos = qbase + lax.broadcasted_iota(jnp.int32, (blk, win + n_ctx), 0)
    kidx = lax.broadcasted_iota(jnp.int32, (blk, win + n_ctx), 1)
    in_band = jnp.abs(start + kidx - qpos) <= A_WINDOW
    bias1 = jnp.where((kidx >= win) | in_band, 0.0, MASK_VALUE).astype(F32)
    bias = jnp.concatenate([bias1, bias1], axis=0)
    lane = _lane_iota((blk, LANES))
    lo = lane < HEAD_DIM
    zero = jnp.zeros((blk, LANES), BF16)
    row_hi = lax.broadcasted_iota(jnp.int32, (2 * blk, 1), 0) >= blk
    for j in range(A_HEADS // 2):
        qc = q_ref[:, j * LANES:(j + 1) * LANES]
        qq = jnp.concatenate([jnp.where(lo, qc, zero), jnp.where(lo, zero, qc)], axis=0)
        s = lax.dot_general(qq, keys, (((1,), (1,)), ((), ())), preferred_element_type=F32) + bias
        sink = jnp.where(row_hi, sink_ref[j + A_HEADS // 2], sink_ref[j])
        m = jnp.maximum(jnp.max(s, axis=-1, keepdims=True), sink)
        p = jnp.exp(s - m)
        l = jnp.sum(p, axis=-1, keepdims=True) + jnp.exp(sink - m)
        o = jnp.dot(p.astype(BF16), vals, preferred_element_type=F32) * (1.0 / l)
        o_ref[:, j * LANES:(j + 1) * LANES] = jnp.where(lo, o[:blk], o[blk:]).astype(BF16)


def _attn_a(q, k, v, sink, n_batch, n_lat, n_ctx, with_ctx):
    t_rows = q.shape[0]
    blk = A_BLOCK
    nb = n_lat // blk
    ncb = n_ctx // blk if with_ctx else 0
    qrow = lambda b, n: jnp.where(n < nb, b * nb + n, n_batch * nb + b * (n_ctx // blk) + (n - nb))
    ctx_blk = lambda b: (n_batch * n_lat) // n_ctx + b
    grid_spec = pltpu.PrefetchScalarGridSpec(
        num_scalar_prefetch=1,
        grid=(n_batch, nb + ncb),
        in_specs=[
            pl.BlockSpec((blk, A_Q), lambda b, n, s: (qrow(b, n), 0)),
            pl.BlockSpec((n_lat, A_KV), lambda b, n, s: (b, 0)),
            pl.BlockSpec((n_lat, A_KV), lambda b, n, s: (b, 0)),
            pl.BlockSpec((n_ctx, A_KV), lambda b, n, s: (ctx_blk(b), 0)),
            pl.BlockSpec((n_ctx, A_KV), lambda b, n, s: (ctx_blk(b), 0)),
        ],
        out_specs=pl.BlockSpec((blk, A_Q), lambda b, n, s: (qrow(b, n), 0)),
    )
    return pl.pallas_call(
        functools.partial(_attn_a_kernel, n_blocks=nb, n_lat=n_lat),
        grid_spec=grid_spec,
        out_shape=jax.ShapeDtypeStruct((t_rows if with_ctx else n_batch * n_lat, A_Q), BF16),
        compiler_params=pltpu.CompilerParams(vmem_limit_bytes=VMEM_LIMIT),
    )(sink, q, k, v, k, v)


def _attn_b_kernel(q_ref, k_ref, v_ref, kc_ref, vc_ref, bias_ref, o_ref, *, rows, n_steps):
    r = pl.program_id(2)
    nq = B_QROWS * GRID_W
    nk = B_KROWS * GRID_W
    r0 = jnp.where(r < n_steps, r * B_QROWS, 0)
    ws = jnp.clip(r0 - NA_KH // 2, 0, rows - B_KROWS)
    start = pl.multiple_of(ws * GRID_W, GRID_W)
    kwin = k_ref[pl.ds(start, nk), :]
    vwin = v_ref[pl.ds(start, nk), :]
    kc = kc_ref[...]
    vc = vc_ref[...]
    q = q_ref[...]
    lane = _lane_iota((nq, LANES))
    lo = lane < HEAD_DIM
    zero = jnp.zeros((nq, LANES), BF16)
    qq = jnp.concatenate([jnp.where(lo, q, zero), jnp.where(lo, zero, q)], axis=0)
    dn = (((1,), (1,)), ((), ()))
    s_loc = lax.dot_general(qq, kwin, dn, preferred_element_type=F32)
    s_loc = s_loc + jnp.concatenate([bias_ref[0], bias_ref[1]], axis=0)
    s_ctx = lax.dot_general(qq, kc, dn, preferred_element_type=F32)
    m = jnp.maximum(jnp.max(s_loc, axis=-1, keepdims=True), jnp.max(s_ctx, axis=-1, keepdims=True))
    p_loc = jnp.exp(s_loc - m)
    p_ctx = jnp.exp(s_ctx - m)
    l = jnp.sum(p_loc, axis=-1, keepdims=True) + jnp.sum(p_ctx, axis=-1, keepdims=True)
    o = (jnp.dot(p_loc.astype(BF16), vwin, preferred_element_type=F32)
         + jnp.dot(p_ctx.astype(BF16), vc, preferred_element_type=F32)) * (1.0 / l)
    o_ref[...] = jnp.where(lo, o[:nq], o[nq:]).astype(BF16)


def _nbr_bias_table(rpb, rows):
    n_steps = rows // B_QROWS
    steps = [min(2, n_steps - 1), 0, 1, n_steps - 2, n_steps - 1]
    w = GRID_W
    n_heads = rpb.shape[0]
    c = np.arange(w)[:, None]
    kc = np.arange(w)[None, :]
    cs = np.clip(c - NA_KW // 2, 0, w - NA_KW)
    col_ok = (kc >= cs) & (kc < cs + NA_KW)
    onehot = (((kc - c + NA_KW - 1)[None] == np.arange(2 * NA_KW - 1)[:, None, None]) & col_ok[None])
    toe = jnp.einsum("hrd,dck->hrck", rpb.astype(F32), jnp.asarray(onehot, F32),
                     precision=lax.Precision.HIGHEST)
    toe = jnp.where(col_ok[None, None], toe, MASK_VALUE)
    masked = jnp.full((n_heads, w, w), MASK_VALUE, F32)
    tabs = []
    for st in steps:
        r0 = st * B_QROWS
        ws = int(np.clip(r0 - NA_KH // 2, 0, rows - B_KROWS))
        q_rows = []
        for rq in range(B_QROWS):
            r = r0 + rq
            rs = int(np.clip(r - NA_KH // 2, 0, rows - NA_KH))
            blocks = [toe[:, ws + ki - r + NA_KH - 1] if rs <= ws + ki < rs + NA_KH else masked
                      for ki in range(B_KROWS)]
            q_rows.append(jnp.concatenate(blocks, axis=-1))
        tabs.append(jnp.concatenate(q_rows, axis=1))
    tabs.append(jnp.full_like(tabs[0], MASK_VALUE))
    return jnp.stack(tabs)


def _attn_b(q, k, v, bias_tab, n_batch, n_lat, n_ctx, with_ctx):
    t_rows = q.shape[0]
    rows = n_lat // GRID_W
    nq = B_QROWS * GRID_W
    n_steps = rows // B_QROWS
    n_cstep = n_ctx // nq if with_ctx else 0
    qrow = lambda b, r: jnp.where(r < n_steps, b * n_steps + r,
                                  n_batch * n_steps + b * (n_ctx // nq) + (r - n_steps))
    ctx_blk = lambda b: (n_batch * n_lat) // n_ctx + b

    def variant(r):
        v = jnp.where(r == 0, 1, 0)
        v = jnp.where(r == 1, 2, v)
        v = jnp.where(r == n_steps - 2, 3, v)
        v = jnp.where(r == n_steps - 1, 4, v)
        return jnp.where(r >= n_steps, 5, v)

    return pl.pallas_call(
        functools.partial(_attn_b_kernel, rows=rows, n_steps=n_steps),
        grid=(n_batch, B_HEADS // 2, n_steps + n_cstep),
        in_specs=[
            pl.BlockSpec((nq, LANES), lambda b, j, r: (qrow(b, r), j)),
            pl.BlockSpec((n_lat, LANES), lambda b, j, r: (b, j)),
            pl.BlockSpec((n_lat, LANES), lambda b, j, r: (b, j)),
            pl.BlockSpec((n_ctx, LANES), lambda b, j, r: (ctx_blk(b), j)),
            pl.BlockSpec((n_ctx, LANES), lambda b, j, r: (ctx_blk(b), j)),
            pl.BlockSpec((None, 2, nq, B_KROWS * GRID_W), lambda b, j, r: (variant(r), j, 0, 0)),
        ],
        out_specs=pl.BlockSpec((nq, LANES), lambda b, j, r: (qrow(b, r), j)),
        out_shape=jax.ShapeDtypeStruct((t_rows if with_ctx else n_batch * n_lat, B_W), BF16),
        compiler_params=pltpu.CompilerParams(vmem_limit_bytes=VMEM_LIMIT),
    )(q, k, v, k, v, bias_tab)


def _attn_c_kernel(lam_ref, q_ref, k_ref, vt_ref, kc_ref, vct_ref, g_ref, o_ref,
                   s_a, s_b, s_c, m_scr, l_scr, acc_scr, *, n_qlat, n_kchunks, lam_init):
    i = pl.program_id(2)
    tq = q_ref.shape[0]
    tk = C_KCHUNK
    q = q_ref[...]
    lo = _lane_iota((tq, LANES)) < HEAD_DIM
    zero = jnp.zeros((tq, LANES), BF16)
    q_maps = (jnp.where(lo, q, zero), jnp.where(lo, zero, q))
    dn = (((1,), (1,)), ((), ()))

    def scores(kk, s_ref):
        for mi in range(2):
            s_ref[mi] = lax.dot_general(kk, q_maps[mi], dn, preferred_element_type=F32)

    def softmax_pv(s_ref, vvt):
        for mi in range(2):
            s = s_ref[mi]
            m = m_scr[mi]
            m_new = jnp.maximum(m, jnp.max(s, axis=0, keepdims=True))
            alpha = jnp.exp2(m - m_new)
            p = jnp.exp2(s - m_new)
            l_scr[mi] = alpha * l_scr[mi] + jnp.sum(p, axis=0, keepdims=True)
            acc_scr[mi] = alpha * acc_scr[mi] + jnp.dot(vvt, p.astype(BF16),
                                                        preferred_element_type=F32)
            m_scr[mi] = m_new

    def chunk_slice(c):
        return pl.ds(c * tk if isinstance(c, int) else pl.multiple_of(c * tk, tk), tk)

    def k_chunk(c):
        return k_ref[chunk_slice(c), :]

    def vt_chunk(c):
        return vt_ref[:, chunk_slice(c)]

    m_scr[...] = jnp.full(m_scr.shape, -jnp.inf, F32)
    l_scr[...] = jnp.zeros(l_scr.shape, F32)
    acc_scr[...] = jnp.zeros(acc_scr.shape, F32)

    @pl.when(i < n_qlat)
    def _():
        slots = (s_a, s_b)
        scores(k_chunk(0), s_a)

        def body(it, carry):
            c0 = C_UNROLL * it
            for u in range(C_UNROLL):
                scores(k_chunk(c0 + u + 1), slots[(u + 1) % 2])
                softmax_pv(slots[u % 2], vt_chunk(c0 + u))
            return carry

        n_it = (n_kchunks - 1) // C_UNROLL
        lax.fori_loop(0, n_it, body, 0)
        for c in range(C_UNROLL * n_it, n_kchunks):
            if c + 1 < n_kchunks:
                scores(k_chunk(c + 1), slots[(c + 1) % 2])
            else:
                scores(kc_ref[...], s_c)
            softmax_pv(slots[c % 2], vt_chunk(c))

    @pl.when(i >= n_qlat)
    def _():
        scores(kc_ref[...], s_c)

    softmax_pv(s_c, vct_ref[...])
    lp = lam_ref[...]
    lam = (jnp.exp(jnp.sum(lp[0:1] * lp[1:2], axis=-1, keepdims=True))
           - jnp.exp(jnp.sum(lp[2:3] * lp[3:4], axis=-1, keepdims=True)) + lam_init)
    od = acc_scr[0] * (1.0 / l_scr[0]) - lam * (acc_scr[1] * (1.0 / l_scr[1]))
    ms = jnp.mean(od * od, axis=0, keepdims=True)
    on = (od * lax.rsqrt(ms + LN_EPS)).T
    o_ref[...] = ((on * g_ref[...]) * (1.0 - lam_init)).astype(BF16)


def _attn_c(q, k, vt, lam_p, subln, lam_init, n_batch, n_lat, n_ctx, with_ctx):
    t_rows = q.shape[0]
    tq = C_QTILE
    n_qlat = n_lat // tq
    n_qctx = n_ctx // tq if with_ctx else 0
    qrow = lambda b, i: jnp.where(i < n_qlat, b * n_qlat + i,
                                  n_batch * n_qlat + b * (n_ctx // tq) + (i - n_qlat))
    ctx_blk = lambda b: (n_batch * n_lat) // n_ctx + b
    return pl.pallas_call(
        functools.partial(_attn_c_kernel, n_qlat=n_qlat, n_kchunks=n_lat // C_KCHUNK,
                          lam_init=lam_init),
        grid=(n_batch, C_HEADS, n_qlat + n_qctx),
        in_specs=[
            pl.BlockSpec((4, HEAD_DIM), lambda b, h, i: (0, 0)),
            pl.BlockSpec((tq, LANES), lambda b, h, i: (qrow(b, i), h)),
            pl.BlockSpec((n_lat, LANES), lambda b, h, i: (b, h)),
            pl.BlockSpec((LANES, n_lat), lambda b, h, i: (h, b)),
            pl.BlockSpec((n_ctx, LANES), lambda b, h, i: (ctx_blk(b), h)),
            pl.BlockSpec((LANES, n_ctx), lambda b, h, i: (h, ctx_blk(b))),
            pl.BlockSpec((1, LANES), lambda b, h, i: (0, 0)),
        ],
        out_specs=pl.BlockSpec((tq, LANES), lambda b, h, i: (qrow(b, i), h)),
        out_shape=jax.ShapeDtypeStruct((t_rows if with_ctx else n_batch * n_lat, C_OUT), BF16),
        scratch_shapes=[
            pltpu.VMEM((2, C_KCHUNK, tq), F32), pltpu.VMEM((2, C_KCHUNK, tq), F32),
            pltpu.VMEM((2, n_ctx, tq), F32),
            pltpu.VMEM((2, 1, tq), F32), pltpu.VMEM((2, 1, tq), F32), pltpu.VMEM((2, LANES, tq), F32),
        ],
        compiler_params=pltpu.CompilerParams(vmem_limit_bytes=VMEM_LIMIT),
    )(lam_p, q, k, vt, k, vt, subln.reshape(1, LANES))


def _top2_router(logits_src, wr_ref):
    lg = jnp.dot(logits_src, wr_ref[...], preferred_element_type=F32, precision=lax.Precision.HIGHEST)
    lane = _lane_iota(lg.shape)
    lanef = lane.astype(F32)
    lg = jnp.where(lane < N_EXPERTS, lg, -jnp.inf)
    m1 = jnp.max(lg, axis=-1, keepdims=True)
    i1 = jnp.min(jnp.where(lg == m1, lanef, float(LANES)), axis=-1, keepdims=True)
    lg2 = jnp.where(lanef == i1, -jnp.inf, lg)
    m2 = jnp.max(lg2, axis=-1, keepdims=True)
    i2 = jnp.min(jnp.where(lg2 == m2, lanef, float(LANES)), axis=-1, keepdims=True)
    e = jnp.exp(m2 - m1)
    w1 = 1.0 / (1.0 + e)
    w2 = e / (1.0 + e)
    out = jnp.where(lane == 0, i1, 0.0)
    out = jnp.where(lane == 1, i2, out)
    out = jnp.where(lane == 2, w1, out)
    return jnp.where(lane == 3, w2, out)


def _pack_bf16_pairs(h):
    bits = lax.bitcast_convert_type(h.astype(BF16).astype(F32), jnp.uint32)
    half = h.shape[1] // 2
    return (bits[:, half:] & jnp.uint32(0xFFFF0000)) | (bits[:, :half] >> 16)


def _unpack_bf16_pairs(p):
    lo = lax.bitcast_convert_type(p << 16, F32)
    hi = lax.bitcast_convert_type(p & jnp.uint32(0xFFFF0000), F32)
    return jnp.concatenate([lo, hi], axis=1).astype(BF16)


def _outproj_kernel(*refs, n_in, router):
    o_refs = refs[:n_in]
    w_refs = refs[n_in:2 * n_in]
    x_ref, g_ref, lng_ref, lnb_ref, sc_ref, sh_ref = refs[2 * n_in:2 * n_in + 6]
    rest = refs[2 * n_in + 6:]
    y = jnp.dot(o_refs[0][...], w_refs[0][...], preferred_element_type=F32)
    for o_r, w_r in zip(o_refs[1:], w_refs[1:]):
        y = y + jnp.dot(o_r[...], w_r[...], preferred_element_type=F32)
    xn = _layernorm_rows(DN_ALPHA * x_ref[...] + g_ref[...] * y, lng_ref[...], lnb_ref[...])
    h2 = xn * (1.0 + sc_ref[...]) + sh_ref[...]
    if router:
        wr_ref, xo_ref, h_ref, r_ref = rest
        r_ref[...] = _top2_router(h2, wr_ref)
        h_ref[...] = _pack_bf16_pairs(h2)
    else:
        xo_ref, h_ref = rest
        h_ref[...] = h2.astype(BF16)
    xo_ref[...] = xn


def _out_proj(o_list, w_list, xs, mods_l, ln_g, ln_b, n_tiles, tiles_per_batch, n_batch, w_router=None):
    d = xs.shape[1]
    tm = ROW_TILE
    t_rows = n_tiles * tm
    grp = lambda t: jnp.minimum(t // tiles_per_batch, n_batch)
    mod = lambda k: pl.BlockSpec((None, None, 1, d), lambda t: (grp(t), k, 0, 0))
    in_specs = [pl.BlockSpec((tm, o.shape[1]), lambda t: (t, 0)) for o in o_list]
    in_specs += [pl.BlockSpec(w.shape, lambda t: (0, 0)) for w in w_list]
    in_specs += [pl.BlockSpec((tm, d), lambda t: (t, 0)), mod(2),
                 pl.BlockSpec((1, d), lambda t: (0, 0)), pl.BlockSpec((1, d), lambda t: (0, 0)),
                 mod(4), mod(3)]
    args = list(o_list) + list(w_list) + [xs, mods_l, ln_g.reshape(1, d), ln_b.reshape(1, d), mods_l, mods_l]
    out_shape = [jax.ShapeDtypeStruct((t_rows, d), F32), jax.ShapeDtypeStruct((t_rows, d), BF16)]
    out_specs = [pl.BlockSpec((tm, d), lambda t: (t, 0)), pl.BlockSpec((tm, d), lambda t: (t, 0))]
    if w_router is not None:
        out_shape[1] = jax.ShapeDtypeStruct((t_rows, d // 2), jnp.uint32)
        out_specs[1] = pl.BlockSpec((tm, d // 2), lambda t: (t, 0))
        in_specs.append(pl.BlockSpec(w_router.shape, lambda t: (0, 0)))
        args.append(w_router)
        out_shape.append(jax.ShapeDtypeStruct((t_rows, LANES), F32))
        out_specs.append(pl.BlockSpec((tm, LANES), lambda t: (t, 0)))
    return pl.pallas_call(
        functools.partial(_outproj_kernel, n_in=len(o_list), router=w_router is not None),
        grid=(n_tiles,),
        in_specs=in_specs,
        out_specs=out_specs,
        out_shape=out_shape,
        compiler_params=pltpu.CompilerParams(vmem_limit_bytes=VMEM_LIMIT),
    )(*args)


def _swiglu_acc(h, wg_ref, wu_ref, wd_ref, acc_ref, c):
    g = jnp.dot(h, wg_ref[...], preferred_element_type=F32)
    u = jnp.dot(h, wu_ref[...], preferred_element_type=F32)
    a = (g * (1.0 / (1.0 + jnp.exp(-g))) * u).astype(BF16)
    part = jnp.dot(a, wd_ref[...], preferred_element_type=F32)

    @pl.when(c == 0)
    def _():
        acc_ref[...] = part

    @pl.when(c > 0)
    def _():
        acc_ref[...] += part


def _ffn_dense_kernel(h_ref, wg_ref, wu_ref, wd_ref, x_ref, g_ref, lng_ref, lnb_ref, xo_ref, acc_ref):
    c = pl.program_id(1)
    _swiglu_acc(h_ref[...], wg_ref, wu_ref, wd_ref, acc_ref, c)

    @pl.when(c == pl.num_programs(1) - 1)
    def _():
        z = DN_ALPHA * x_ref[...] + g_ref[...] * acc_ref[...]
        xo_ref[...] = _layernorm_rows(z, lng_ref[...], lnb_ref[...])


def _ffn_dense(h2, w_up, w_down, xs, mods_l, ln_g, ln_b, n_tiles, tiles_per_batch, n_batch):
    d = xs.shape[1]
    tm = ROW_TILE
    t_rows = n_tiles * tm
    ff = w_down.shape[0]
    nc = ff // FF_CHUNK
    grp = lambda t: jnp.minimum(t // tiles_per_batch, n_batch)
    return pl.pallas_call(
        _ffn_dense_kernel,
        grid=(n_tiles, nc),
        in_specs=[
            pl.BlockSpec((tm, d), lambda t, c: (t, 0)),
            pl.BlockSpec((d, FF_CHUNK), lambda t, c: (0, c)),
            pl.BlockSpec((d, FF_CHUNK), lambda t, c: (0, nc + c)),
            pl.BlockSpec((FF_CHUNK, d), lambda t, c: (c, 0)),
            pl.BlockSpec((tm, d), lambda t, c: (t, 0)),
            pl.BlockSpec((None, None, 1, d), lambda t, c: (grp(t), 5, 0, 0)),
            pl.BlockSpec((1, d), lambda t, c: (0, 0)),
            pl.BlockSpec((1, d), lambda t, c: (0, 0)),
        ],
        out_specs=pl.BlockSpec((tm, d), lambda t, c: (t, 0)),
        out_shape=jax.ShapeDtypeStruct((t_rows, d), F32),
        scratch_shapes=[pltpu.VMEM((tm, d), F32)],
        compiler_params=pltpu.CompilerParams(vmem_limit_bytes=VMEM_LIMIT),
    )(h2, w_up, w_up, w_down, xs, mods_l, ln_g.reshape(1, d), ln_b.reshape(1, d))


def _ffn_expert_kernel(te_ref, nu_ref, h_ref, wg_ref, wu_ref, wd_ref, y_ref, acc_ref):
    i = pl.program_id(0)
    c = pl.program_id(1)

    @pl.when(i < nu_ref[0])
    def _():
        _swiglu_acc(_unpack_bf16_pairs(h_ref[...]), wg_ref, wu_ref, wd_ref, acc_ref, c)

    @pl.when(c == pl.num_programs(1) - 1)
    def _():
        y_ref[...] = acc_ref[...]


def _ffn_experts(hs, w_up, w_down, layer, tile_expert, n_used):
    p_rows = hs.shape[0]
    d = w_down.shape[-1]
    tm = MOE_TILE
    n_tiles = p_rows // tm
    ef = w_down.shape[2]
    nc = ef // EXPERT_FF_CHUNK
    chunk = lambda i, c, nu: jnp.where(i < nu[0], c, nc - 1)
    grid_spec = pltpu.PrefetchScalarGridSpec(
        num_scalar_prefetch=2,
        grid=(n_tiles, nc),
        in_specs=[
            pl.BlockSpec((tm, d // 2), lambda i, c, te, nu: (i, 0)),
            pl.BlockSpec((None, None, d, EXPERT_FF_CHUNK),
                         lambda i, c, te, nu: (layer, te[i], 0, chunk(i, c, nu))),
            pl.BlockSpec((None, None, d, EXPERT_FF_CHUNK),
                         lambda i, c, te, nu: (layer, te[i], 0, nc + chunk(i, c, nu))),
            pl.BlockSpec((None, None, EXPERT_FF_CHUNK, d),
                         lambda i, c, te, nu: (layer, te[i], chunk(i, c, nu), 0)),
        ],
        out_specs=pl.BlockSpec((tm, d), lambda i, c, te, nu: (i, 0)),
        scratch_shapes=[pltpu.VMEM((tm, d), F32)],
    )
    return pl.pallas_call(
        _ffn_expert_kernel,
        grid_spec=grid_spec,
        out_shape=jax.ShapeDtypeStruct((p_rows, d), F32),
        compiler_params=pltpu.CompilerParams(vmem_limit_bytes=VMEM_LIMIT),
    )(tile_expert, n_used, hs, w_up, w_up, w_down)


def _combine_kernel(y0_ref, y1_ref, r_ref, x_ref, g_ref, lng_ref, lnb_ref, xo_ref):
    r = r_ref[...]
    f = r[:, 2:3] * y0_ref[...] + r[:, 3:4] * y1_ref[...]
    z = DN_ALPHA * x_ref[...] + g_ref[...] * f
    xo_ref[...] = _layernorm_rows(z, lng_ref[...], lnb_ref[...])


def _moe_combine(y0, y1, route, xs, mods_l, ln_g, ln_b, n_tiles, tiles_per_batch, n_batch):
    d = xs.shape[1]
    tm = ROW_TILE
    t_rows = n_tiles * tm
    grp = lambda t: jnp.minimum(t // tiles_per_batch, n_batch)
    row = pl.BlockSpec((tm, d), lambda t: (t, 0))
    return pl.pallas_call(
        _combine_kernel,
        grid=(n_tiles,),
        in_specs=[row, row, pl.BlockSpec((tm, LANES), lambda t: (t, 0)), row,
                  pl.BlockSpec((None, None, 1, d), lambda t: (grp(t), 5, 0, 0)),
                  pl.BlockSpec((1, d), lambda t: (0, 0)), pl.BlockSpec((1, d), lambda t: (0, 0))],
        out_specs=row,
        out_shape=jax.ShapeDtypeStruct((t_rows, d), F32),
        compiler_params=pltpu.CompilerParams(vmem_limit_bytes=VMEM_LIMIT),
    )(y0, y1, route, xs, mods_l, ln_g.reshape(1, d), ln_b.reshape(1, d))


def _routing_plan(route, n_rows):
    tm = MOE_TILE
    e_idx = route[:n_rows, 0:2].astype(jnp.int32).reshape(-1)
    onehot = (e_idx[:, None] == jnp.arange(N_EXPERTS, dtype=jnp.int32)[None, :]).astype(jnp.int32)
    csum = jnp.cumsum(onehot, axis=0)
    counts = csum[-1]
    rank = jnp.sum((csum - onehot) * onehot, axis=1)
    padded = ((counts + tm - 1) // tm) * tm
    ends = jnp.cumsum(padded)
    starts = ends - padded
    dest = starts[e_idx] + rank
    n_tiles = (2 * n_rows + N_EXPERTS * (tm - 1)) // tm
    p_rows = n_tiles * tm
    row_token = jnp.zeros((p_rows,), jnp.int32).at[dest].set(jnp.arange(2 * n_rows, dtype=jnp.int32) // 2)
    tile_start = jnp.arange(n_tiles, dtype=jnp.int32) * tm
    tile_expert = jnp.minimum(jnp.sum((tile_start[:, None] >= ends[None, :]).astype(jnp.int32), axis=1),
                              N_EXPERTS - 1)
    n_used = (ends[-1] // tm).astype(jnp.int32).reshape(1)
    last = tile_expert[jnp.maximum(n_used[0] - 1, 0)]
    tile_expert = jnp.where(jnp.arange(n_tiles) < n_used[0], tile_expert, last).astype(jnp.int32)
    return row_token, dest.reshape(n_rows, 2), tile_expert, n_used


def _rope_tables(n_lat):
    t = jnp.arange(n_lat, dtype=jnp.int32)
    row = (t // GRID_W).astype(F32)
    col = (t % GRID_W).astype(F32)
    inv = ROPE_THETA ** (-jnp.arange(0, ROPE_AXIS_DIM, 2, dtype=F32) / ROPE_AXIS_DIM)
    ar = row[:, None] * inv[None, :]
    ac = col[:, None] * inv[None, :]
    ang = jnp.concatenate([ar, ar, ac, ac], axis=-1)
    cos = jnp.tile(jnp.cos(ang), (1, LANES // HEAD_DIM))
    sin = jnp.tile(jnp.sin(ang), (1, LANES // HEAD_DIM))
    sign = jnp.where((jnp.arange(LANES) % 32) < 16, -1.0, 1.0).astype(F32)
    cos = jnp.concatenate([cos, jnp.ones((ROW_TILE, LANES), F32)], axis=0)
    sin = jnp.concatenate([sin * sign[None, :], jnp.zeros((ROW_TILE, LANES), F32)], axis=0)
    return cos, sin


def _lambda_init(layer):
    return 0.8 - 0.6 * math.exp(-0.3 * layer)


def kernel(x, c, ctx, c_ctx, w_mod, b_mod, ln_g, ln_b, w_in_ab, w_out_ab, sink_a, rpb_b, w_in_c, w_out_c,
           lam_c, subln_c, w_ffn_up, w_ffn_down, w_router, w_exp_up, w_exp_down):
    n_batch, n_lat, d = x.shape
    n_ctx = ctx.shape[1]
    assert d == D_MODEL and n_batch * n_ctx == ROW_TILE and n_lat % ROW_TILE == 0
    assert n_batch + 1 <= 8 and n_lat % C_KCHUNK == 0 and n_ctx % C_QTILE == 0
    rows = n_lat // GRID_W
    assert rows >= 12 and rows % B_QROWS == 0
    tiles_per_batch = n_lat // ROW_TILE
    lat_tiles = n_batch * tiles_per_batch
    lat_rows = n_batch * n_lat

    xs = jnp.concatenate([x.reshape(lat_rows, d), ctx.reshape(n_batch * n_ctx, d)], axis=0)
    cond = jnp.zeros((8, d), F32).at[:n_batch].set(c).at[n_batch].set(c_ctx)
    mods = _mod_vectors(cond, w_mod, b_mod).reshape(DEPTH, 8, 6, 1, d)
    cos_t, sin_t = _rope_tables(n_lat)
    w_exp_up_b = w_exp_up.astype(BF16)
    w_exp_down_b = w_exp_down.astype(BF16)

    perm = np.array([(j + (A_HEADS // 2) * half) * HEAD_DIM + dd
                     for j in range(A_HEADS // 2) for half in range(2) for dd in range(HEAD_DIM)])

    for l in range(DEPTH):
        last = l == DEPTH - 1
        i = l // 2
        all_tiles = lat_tiles + 1
        n_tiles = lat_tiles if last else all_tiles
        mods_l = mods[l]
        if l % 2 == 0:
            w_in = w_in_ab[i]
            w_in = jnp.concatenate([w_in[:, :A_Q][:, perm], w_in[:, A_Q:]], axis=1).astype(BF16)
            segs = [(A_Q, True, ATTN_SCALE), (A_KV, True, 1.0), (A_KV, False, 1.0),
                    (B_W, False, ATTN_SCALE), (B_W, False, 1.0), (B_W, False, 1.0)]
            qa, ka, va, qb, kb, vb = _in_proj(xs, mods_l, cos_t, sin_t, w_in, segs, all_tiles,
                                              tiles_per_batch, n_batch)
            sink = sink_a[i].astype(F32)
            oa = _attn_a(qa, ka, va, sink, n_batch, n_lat, n_ctx, not last)
            ob = _attn_b(qb, kb, vb, _nbr_bias_table(rpb_b[i], rows), n_batch, n_lat, n_ctx, not last)
            w_out = w_out_ab[i]
            o_list = [oa, ob]
            w_list = [w_out[:A_Q][perm].astype(BF16), w_out[A_Q:].astype(BF16)]
        else:
            segs = [(C_QK, True, ATTN_SCALE * math.log2(math.e)), (C_QK, True, 1.0), (C_OUT, False, 1.0)]
            qc, kc, vc = _in_proj(xs, mods_l, cos_t, sin_t, w_in_c[i].astype(BF16), segs, all_tiles,
                                  tiles_per_batch, n_batch)
            oc = _attn_c(qc, kc, vc.T, lam_c[i].astype(F32), subln_c[i].astype(F32), _lambda_init(l),
                         n_batch, n_lat, n_ctx, not last)
            o_list = [oc]
            w_list = [w_out_c[i].astype(BF16)]

        if l % 2 == 0:
            xs, h2 = _out_proj(o_list, w_list, xs, mods_l, ln_g[l, 0], ln_b[l, 0], n_tiles,
                               tiles_per_batch, n_batch)
            xs = _ffn_dense(h2, w_ffn_up[i].astype(BF16), w_ffn_down[i].astype(BF16), xs, mods_l,
                            ln_g[l, 1], ln_b[l, 1], n_tiles, tiles_per_batch, n_batch)
        else:
            wr = jnp.zeros((d, LANES), F32).at[:, :N_EXPERTS].set(w_router[i])
            xs, h2, route = _out_proj(o_list, w_list, xs, mods_l, ln_g[l, 0], ln_b[l, 0], n_tiles,
                                      tiles_per_batch, n_batch, w_router=wr)
            n_rows = n_tiles * ROW_TILE
            row_token, dest, tile_expert, n_used = _routing_plan(route, n_rows)
            hs = jnp.take(h2, row_token, axis=0, mode="clip")
            ys = _ffn_experts(hs, w_exp_up_b, w_exp_down_b, i, tile_expert, n_used)
            y0 = jnp.take(ys, dest[:, 0], axis=0, mode="clip")
            y1 = jnp.take(ys, dest[:, 1], axis=0, mode="clip")
            xs = _moe_combine(y0, y1, route, xs, mods_l, ln_g[l, 1], ln_b[l, 1], n_tiles,
                              tiles_per_batch, n_batch)
    return xs[:lat_rows].reshape(n_batch, n_lat, d)
```

```python
import functools
import math

import jax
import jax.numpy as jnp
import numpy as np
from jax import lax
from jax.experimental import pallas as pl
from jax.experimental.pallas import tpu as pltpu

F32 = jnp.float32
BF16 = jnp.bfloat16

D_MODEL = 1024
DEPTH = 4
GRID_W = 64
HEAD_DIM = 64
LANES = 128
ATTN_SCALE = HEAD_DIM ** -0.5
A_BLOCK = 128
A_WINDOW = 128
A_HEADS = 8
A_KV_HEADS = 2
B_HEADS = 8
NA_KH = 8
NA_KW = 16
C_HEADS = 8
ROPE_THETA = 10000.0
ROPE_AXIS_DIM = HEAD_DIM // 2
FF_DIM = 2816
N_EXPERTS = 8
EXPERT_FF = 3584
LN_EPS = 1e-5
DN_ALPHA = (2 * DEPTH) ** 0.25
MASK_VALUE = -1e30
A_Q = A_HEADS * HEAD_DIM
A_KV = A_KV_HEADS * HEAD_DIM
B_W = B_HEADS * HEAD_DIM
C_QK = 1024
C_OUT = 1024

ROW_TILE = 512
MOE_TILE = 512
FF_CHUNK = 1408
EXPERT_FF_CHUNK = 1792
B_QROWS = 2
B_KROWS = B_QROWS + NA_KH
B_PAIRS = 4
C_QTILE = 256
C_KCHUNK = 512
C_UNROLL = 4
VMEM_LIMIT = 56 * 1024 * 1024


def _lane_iota(shape):
    return lax.broadcasted_iota(jnp.int32, shape, len(shape) - 1)


def _layernorm_rows(z, g, b):
    mu = jnp.mean(z, axis=-1, keepdims=True)
    zc = z - mu
    var = jnp.mean(zc * zc, axis=-1, keepdims=True)
    return zc * lax.rsqrt(var + LN_EPS) * g + b


def _mod_kernel(c_ref, w_ref, b_ref, o_ref):
    c = c_ref[...]
    s = c * (1.0 / (1.0 + jnp.exp(-c)))
    o_ref[0] = jnp.dot(s, w_ref[0], preferred_element_type=F32,
                       precision=lax.Precision.HIGHEST) + b_ref[0]


def _mod_vectors(cond, w_mod, b_mod):
    depth, d, n6 = w_mod.shape
    tn = 1536
    return pl.pallas_call(
        _mod_kernel,
        grid=(depth, n6 // tn),
        in_specs=[
            pl.BlockSpec((8, d), lambda l, j: (0, 0)),
            pl.BlockSpec((1, d, tn), lambda l, j: (l, 0, j)),
            pl.BlockSpec((1, 1, tn), lambda l, j: (l, 0, j)),
        ],
        out_specs=pl.BlockSpec((1, 8, tn), lambda l, j: (l, 0, j)),
        out_shape=jax.ShapeDtypeStruct((depth, 8, n6), F32),
        compiler_params=pltpu.CompilerParams(vmem_limit_bytes=VMEM_LIMIT),
    )(cond, w_mod, b_mod.reshape(depth, 1, n6))


def _rope_slab(a, cos, sin_signed, low16):
    fwd = pltpu.roll(a, LANES - 16, axis=1)
    bwd = pltpu.roll(a, 16, axis=1)
    return a * cos + jnp.where(low16, fwd, bwd) * sin_signed


def _proj_kernel(x_ref, sc_ref, sh_ref, cos_ref, sin_ref, w_ref, *out_refs, segments):
    x = x_ref[...]
    h = (x * (1.0 + sc_ref[...]) + sh_ref[...]).astype(BF16)
    cos = cos_ref[...]
    sin = sin_ref[...]
    low16 = (_lane_iota(cos.shape) % 32) < 16
    col = 0
    for o_ref, (width, rope, scale) in zip(out_refs, segments):
        for c0 in range(0, width, 256):
            cw = min(256, width - c0)
            acc = jnp.dot(h, w_ref[:, col + c0:col + c0 + cw], preferred_element_type=F32)
            slabs = []
            for s0 in range(0, cw, LANES):
                a = acc[:, s0:s0 + LANES]
                if rope:
                    a = _rope_slab(a, cos, sin, low16)
                if scale != 1.0:
                    a = a * scale
                slabs.append(a.astype(BF16))
            o_ref[:, c0:c0 + cw] = slabs[0] if len(slabs) == 1 else jnp.concatenate(slabs, axis=1)
        col += width


def _in_proj(xs, mods_l, cos_t, sin_t, w, segments, n_tiles, tiles_per_batch, n_batch):
    t_rows, d = xs.shape
    tm = ROW_TILE
    grp = lambda t: jnp.minimum(t // tiles_per_batch, n_batch)
    pos = lambda t: jnp.where(t < n_batch * tiles_per_batch, t % tiles_per_batch, tiles_per_batch)
    out_shape = [jax.ShapeDtypeStruct((t_rows, s[0]), BF16) for s in segments]
    out_specs = [pl.BlockSpec((tm, s[0]), lambda t: (t, 0)) for s in segments]
    return pl.pallas_call(
        functools.partial(_proj_kernel, segments=tuple(segments)),
        grid=(n_tiles,),
        in_specs=[
            pl.BlockSpec((tm, d), lambda t: (t, 0)),
            pl.BlockSpec((None, None, 1, d), lambda t: (grp(t), 1, 0, 0)),
            pl.BlockSpec((None, None, 1, d), lambda t: (grp(t), 0, 0, 0)),
            pl.BlockSpec((tm, LANES), lambda t: (pos(t), 0)),
            pl.BlockSpec((tm, LANES), lambda t: (pos(t), 0)),
            pl.BlockSpec(w.shape, lambda t: (0, 0)),
        ],
        out_specs=out_specs,
        out_shape=out_shape,
        compiler_params=pltpu.CompilerParams(vmem_limit_bytes=VMEM_LIMIT),
    )(xs, mods_l, mods_l, cos_t, sin_t, w)


def _attn_a_kernel(sink_ref, q_ref, k_ref, v_ref, kc_ref, vc_ref, o_ref, *, n_blocks, n_lat):
    n = pl.program_id(1)
    blk = A_BLOCK
    win = 3 * blk
    start = pl.multiple_of(jnp.clip((n - 1) * blk, 0, n_lat - win), blk)
    keys = jnp.concatenate([k_ref[pl.ds(start, win), :], kc_ref[...]], axis=0)
    vals = jnp.concatenate([v_ref[pl.ds(start, win), :], vc_ref[...]], axis=0)
    n_ctx = kc_ref.shape[0]
    qbase = jnp.where(n < n_blocks, n * blk, -(1 << 20))
    qpos = qbase + lax.broadcasted_iota(jnp.int32, (blk, win + n_ctx), 0)
    kidx = lax.broadcasted_iota(jnp.int32, (blk, win + n_ctx), 1)
    in_band = jnp.abs(start + kidx - qpos) <= A_WINDOW
    bias1 = jnp.where((kidx >= win) | in_band, 0.0, MASK_VALUE).astype(F32)
    bias = jnp.concatenate([bias1, bias1], axis=0)
    lane = _lane_iota((blk, LANES))
    lo = lane < HEAD_DIM
    zero = jnp.zeros((blk, LANES), BF16)
    row_hi = lax.broadcasted_iota(jnp.int32, (2 * blk, 1), 0) >= blk
    for j in range(A_HEADS // 2):
        qc = q_ref[:, j * LANES:(j + 1) * LANES]
        qq = jnp.concatenate([jnp.where(lo, qc, zero), jnp.where(lo, zero, qc)], axis=0)
        s = lax.dot_general(qq, keys, (((1,), (1,)), ((), ())), preferred_element_type=F32) + bias
        sink = jnp.where(row_hi, sink_ref[j + A_HEADS // 2], sink_ref[j])
        m = jnp.maximum(jnp.max(s, axis=-1, keepdims=True), sink)
        p = jnp.exp(s - m)
        l = jnp.sum(p, axis=-1, keepdims=True) + jnp.exp(sink - m)
        o = jnp.dot(p.astype(BF16), vals, preferred_element_type=F32) * (1.0 / l)
        o_ref[:, j * LANES:(j + 1) * LANES] = jnp.where(lo, o[:blk], o[blk:]).astype(BF16)


def _attn_a(q, k, v, sink, n_batch, n_lat, n_ctx, with_ctx):
    t_rows = q.shape[0]
    blk = A_BLOCK
    nb = n_lat // blk
    ncb = n_ctx // blk if with_ctx else 0
    qrow = lambda b, n: jnp.where(n < nb, b * nb + n, n_batch * nb + b * (n_ctx // blk) + (n - nb))
    ctx_blk = lambda b: (n_batch * n_lat) // n_ctx + b
    grid_spec = pltpu.PrefetchScalarGridSpec(
        num_scalar_prefetch=1,
        grid=(n_batch, nb + ncb),
        in_specs=[
            pl.BlockSpec((blk, A_Q), lambda b, n, s: (qrow(b, n), 0)),
            pl.BlockSpec((n_lat, A_KV), lambda b, n, s: (b, 0)),
            pl.BlockSpec((n_lat, A_KV), lambda b, n, s: (b, 0)),
            pl.BlockSpec((n_ctx, A_KV), lambda b, n, s: (ctx_blk(b), 0)),
            pl.BlockSpec((n_ctx, A_KV), lambda b, n, s: (ctx_blk(b), 0)),
        ],
        out_specs=pl.BlockSpec((blk, A_Q), lambda b, n, s: (qrow(b, n), 0)),
    )
    return pl.pallas_call(
        functools.partial(_attn_a_kernel, n_blocks=nb, n_lat=n_lat),
        grid_spec=grid_spec,
        out_shape=jax.ShapeDtypeStruct((t_rows if with_ctx else n_batch * n_lat, A_Q), BF16),
        compiler_params=pltpu.CompilerParams(vmem_limit_bytes=VMEM_LIMIT),
    )(sink, q, k, v, k, v)


def _attn_b_kernel(q_ref, k_ref, vt_ref, kc_ref, vct_ref, bias_ref, o_ref, *, rows, n_steps):
    r = pl.program_id(2)
    nq = B_QROWS * GRID_W
    nk = B_KROWS * GRID_W
    r0 = jnp.where(r < n_steps, r * B_QROWS, 0)
    ws = jnp.clip(r0 - NA_KH // 2, 0, rows - B_KROWS)
    start = pl.multiple_of(ws * GRID_W, LANES)
    lo = _lane_iota((nq, LANES)) < HEAD_DIM
    zero = jnp.zeros((nq, LANES), BF16)
    dn = (((1,), (1,)), ((), ()))
    scores = []
    for g in range(B_PAIRS):
        cols = slice(g * LANES, (g + 1) * LANES)
        q = q_ref[:, cols]
        qq = jnp.concatenate([jnp.where(lo, q, zero), jnp.where(lo, zero, q)], axis=0)
        s_loc = lax.dot_general(k_ref[pl.ds(start, nk), cols], qq, dn,
                                preferred_element_type=F32) + bias_ref[g]
        s_ctx = lax.dot_general(kc_ref[:, cols], qq, dn, preferred_element_type=F32)
        scores.append((s_loc, s_ctx))
    for g, (s_loc, s_ctx) in enumerate(scores):
        cols = slice(g * LANES, (g + 1) * LANES)
        m = jnp.maximum(jnp.max(s_loc, axis=0, keepdims=True), jnp.max(s_ctx, axis=0, keepdims=True))
        p_loc = jnp.exp2(s_loc - m)
        p_ctx = jnp.exp2(s_ctx - m)
        l = jnp.sum(p_loc, axis=0, keepdims=True) + jnp.sum(p_ctx, axis=0, keepdims=True)
        ot = (jnp.dot(vt_ref[cols, pl.ds(start, nk)], p_loc.astype(BF16), preferred_element_type=F32)
              + jnp.dot(vct_ref[cols, :], p_ctx.astype(BF16), preferred_element_type=F32)) * (1.0 / l)
        o = jnp.concatenate([ot[:HEAD_DIM, :nq], ot[HEAD_DIM:, nq:]], axis=0)
        o_ref[:, cols] = o.T.astype(BF16)


def _nbr_bias_table(rpb, rows):
    n_steps = rows // B_QROWS
    steps = [min(2, n_steps - 1), 0, 1, n_steps - 2, n_steps - 1]
    w = GRID_W
    n_heads = rpb.shape[0]
    c = np.arange(w)[None, :]
    kc = np.arange(w)[:, None]
    cs = np.clip(c - NA_KW // 2, 0, w - NA_KW)
    col_ok = (kc >= cs) & (kc < cs + NA_KW)
    onehot = (((kc - c + NA_KW - 1)[None] == np.arange(2 * NA_KW - 1)[:, None, None]) & col_ok[None])
    toe = jnp.einsum("hrd,dkc->hrkc", rpb.astype(F32), jnp.asarray(onehot, F32),
                     precision=lax.Precision.HIGHEST)
    toe = jnp.where(col_ok[None, None], toe * math.log2(math.e), MASK_VALUE)
    masked = jnp.full((n_heads, w, w), MASK_VALUE, F32)
    tabs = []
    for st in steps:
        r0 = st * B_QROWS
        ws = int(np.clip(r0 - NA_KH // 2, 0, rows - B_KROWS))
        q_cols = []
        for rq in range(B_QROWS):
            r = r0 + rq
            rs = int(np.clip(r - NA_KH // 2, 0, rows - NA_KH))
            blocks = [toe[:, ws + ki - r + NA_KH - 1] if rs <= ws + ki < rs + NA_KH else masked
                      for ki in range(B_KROWS)]
            q_cols.append(jnp.concatenate(blocks, axis=1))
        tabs.append(jnp.concatenate(q_cols, axis=2))
    tabs.append(jnp.full_like(tabs[0], MASK_VALUE))
    tab = jnp.stack(tabs)
    n_var, _, nk, nq = tab.shape
    tab = tab.reshape(n_var, n_heads // 2, 2, nk, nq)
    return jnp.transpose(tab, (0, 1, 3, 2, 4)).reshape(n_var, n_heads // 2, nk, 2 * nq)


def _attn_b(q, k, vt, bias_tab, n_batch, n_lat, n_ctx, with_ctx):
    t_rows = q.shape[0]
    rows = n_lat // GRID_W
    nq = B_QROWS * GRID_W
    nk = B_KROWS * GRID_W
    gw = B_PAIRS * LANES
    n_steps = rows // B_QROWS
    n_cstep = n_ctx // nq if with_ctx else 0
    qrow = lambda b, r: jnp.where(r < n_steps, b * n_steps + r,
                                  n_batch * n_steps + b * (n_ctx // nq) + (r - n_steps))
    ctx_blk = lambda b: (n_batch * n_lat) // n_ctx + b

    def variant(r):
        v = jnp.where(r == 0, 1, 0)
        v = jnp.where(r == 1, 2, v)
        v = jnp.where(r == n_steps - 2, 3, v)
        v = jnp.where(r == n_steps - 1, 4, v)
        return jnp.where(r >= n_steps, 5, v)

    return pl.pallas_call(
        functools.partial(_attn_b_kernel, rows=rows, n_steps=n_steps),
        grid=(n_batch, B_HEADS // (2 * B_PAIRS), n_steps + n_cstep),
        in_specs=[
            pl.BlockSpec((nq, gw), lambda b, j, r: (qrow(b, r), j)),
            pl.BlockSpec((n_lat, gw), lambda b, j, r: (b, j)),
            pl.BlockSpec((gw, n_lat), lambda b, j, r: (j, b)),
            pl.BlockSpec((n_ctx, gw), lambda b, j, r: (ctx_blk(b), j)),
            pl.BlockSpec((gw, n_ctx), lambda b, j, r: (j, ctx_blk(b))),
            pl.BlockSpec((None, B_PAIRS, nk, 2 * nq), lambda b, j, r: (variant(r), j, 0, 0)),
        ],
        out_specs=pl.BlockSpec((nq, gw), lambda b, j, r: (qrow(b, r), j)),
        out_shape=jax.ShapeDtypeStruct((t_rows if with_ctx else n_batch * n_lat, B_W), BF16),
        compiler_params=pltpu.CompilerParams(vmem_limit_bytes=VMEM_LIMIT),
    )(q, k, vt, k, vt, bias_tab)


def _attn_c_kernel(lam_ref, q_ref, k_ref, vt_ref, kc_ref, vct_ref, g_ref, o_ref,
                   s_a, s_b, s_c, m_scr, l_scr, acc_scr, *, n_qlat, n_kchunks, lam_init):
    i = pl.program_id(2)
    tq = q_ref.shape[0]
    tk = C_KCHUNK
    q = q_ref[...]
    lo = _lane_iota((tq, LANES)) < HEAD_DIM
    zero = jnp.zeros((tq, LANES), BF16)
    q_maps = (jnp.where(lo, q, zero), jnp.where(lo, zero, q))
    dn = (((1,), (1,)), ((), ()))

    def scores(kk, s_ref):
        for mi in range(2):
            s_ref[mi] = lax.dot_general(kk, q_maps[mi], dn, preferred_element_type=F32)

    def softmax_pv(s_ref, vvt):
        for mi in range(2):
            s = s_ref[mi]
            m = m_scr[mi]
            m_new = jnp.maximum(m, jnp.max(s, axis=0, keepdims=True))
            alpha = jnp.exp2(m - m_new)
            p = jnp.exp2(s - m_new)
            l_scr[mi] = alpha * l_scr[mi] + jnp.sum(p, axis=0, keepdims=True)
            acc_scr[mi] = alpha * acc_scr[mi] + jnp.dot(vvt, p.astype(BF16),
                                                        preferred_element_type=F32)
            m_scr[mi] = m_new

    def chunk_slice(c):
        return pl.ds(c * tk if isinstance(c, int) else pl.multiple_of(c * tk, tk), tk)

    def k_chunk(c):
        return k_ref[chunk_slice(c), :]

    def vt_chunk(c):
        return vt_ref[:, chunk_slice(c)]

    m_scr[...] = jnp.full(m_scr.shape, -jnp.inf, F32)
    l_scr[...] = jnp.zeros(l_scr.shape, F32)
    acc_scr[...] = jnp.zeros(acc_scr.shape, F32)

    def finish():
        softmax_pv(s_c, vct_ref[...])
        lp = lam_ref[...]
        lam = (jnp.exp(jnp.sum(lp[0:1] * lp[1:2], axis=-1, keepdims=True))
               - jnp.exp(jnp.sum(lp[2:3] * lp[3:4], axis=-1, keepdims=True)) + lam_init)
        od = acc_scr[0] * (1.0 / l_scr[0]) - lam * (acc_scr[1] * (1.0 / l_scr[1]))
        ms = jnp.mean(od * od, axis=0, keepdims=True)
        on = (od * lax.rsqrt(ms + LN_EPS)).T
        o_ref[...] = ((on * g_ref[...]) * (1.0 - lam_init)).astype(BF16)

    @pl.when(i < n_qlat)
    def _():
        slots = (s_a, s_b)
        scores(k_chunk(0), s_a)

        def body(it, carry):
            c0 = C_UNROLL * it
            for u in range(C_UNROLL):
                scores(k_chunk(c0 + u + 1), slots[(u + 1) % 2])
                softmax_pv(slots[u % 2], vt_chunk(c0 + u))
            return carry

        n_it = (n_kchunks - 1) // C_UNROLL
        lax.fori_loop(0, n_it, body, 0)
        for c in range(C_UNROLL * n_it, n_kchunks):
            if c + 1 < n_kchunks:
                scores(k_chunk(c + 1), slots[(c + 1) % 2])
            else:
                scores(kc_ref[...], s_c)
            softmax_pv(slots[c % 2], vt_chunk(c))
        finish()

    @pl.when(i >= n_qlat)
    def _():
        scores(kc_ref[...], s_c)
        finish()


def _attn_c(q, k, vt, lam_p, subln, lam_init, n_batch, n_lat, n_ctx, with_ctx):
    t_rows = q.shape[0]
    tq = C_QTILE
    n_qlat = n_lat // tq
    n_qctx = n_ctx // tq if with_ctx else 0
    qrow = lambda b, i: jnp.where(i < n_qlat, b * n_qlat + i,
                                  n_batch * n_qlat + b * (n_ctx // tq) + (i - n_qlat))
    ctx_blk = lambda b: (n_batch * n_lat) // n_ctx + b
    return pl.pallas_call(
        functools.partial(_attn_c_kernel, n_qlat=n_qlat, n_kchunks=n_lat // C_KCHUNK,
                          lam_init=lam_init),
        grid=(n_batch, C_HEADS, n_qlat + n_qctx),
        in_specs=[
            pl.BlockSpec((4, HEAD_DIM), lambda b, h, i: (0, 0)),
            pl.BlockSpec((tq, LANES), lambda b, h, i: (qrow(b, i), h)),
            pl.BlockSpec((n_lat, LANES), lambda b, h, i: (b, h)),
            pl.BlockSpec((LANES, n_lat), lambda b, h, i: (h, b)),
            pl.BlockSpec((n_ctx, LANES), lambda b, h, i: (ctx_blk(b), h)),
            pl.BlockSpec((LANES, n_ctx), lambda b, h, i: (h, ctx_blk(b))),
            pl.BlockSpec((1, LANES), lambda b, h, i: (0, 0)),
        ],
        out_specs=pl.BlockSpec((tq, LANES), lambda b, h, i: (qrow(b, i), h)),
        out_shape=jax.ShapeDtypeStruct((t_rows if with_ctx else n_batch * n_lat, C_OUT), BF16),
        scratch_shapes=[
            pltpu.VMEM((2, C_KCHUNK, tq), F32), pltpu.VMEM((2, C_KCHUNK, tq), F32),
            pltpu.VMEM((2, n_ctx, tq), F32),
            pltpu.VMEM((2, 1, tq), F32), pltpu.VMEM((2, 1, tq), F32), pltpu.VMEM((2, LANES, tq), F32),
        ],
        compiler_params=pltpu.CompilerParams(vmem_limit_bytes=VMEM_LIMIT),
    )(lam_p, q, k, vt, k, vt, subln.reshape(1, LANES))


def _bf16_truncate(x):
    bits = lax.bitcast_convert_type(x, jnp.uint32) & jnp.uint32(0xFFFF0000)
    return lax.bitcast_convert_type(bits, F32)


def _top2_router(h, wr_ref):
    h_top = _bf16_truncate(h)
    h_hi = h_top.astype(BF16)
    h_lo = (h - h_top).astype(BF16)
    a = jnp.dot(h_hi, wr_ref[...], preferred_element_type=F32)
    b = jnp.dot(h_lo, wr_ref[:, :LANES], preferred_element_type=F32)
    lg = a[:, :LANES] + a[:, LANES:] + b
    lane = _lane_iota(lg.shape)
    lanef = lane.astype(F32)
    lg = jnp.where(lane < N_EXPERTS, lg, -jnp.inf)
    m1 = jnp.max(lg, axis=-1, keepdims=True)
    i1 = jnp.min(jnp.where(lg == m1, lanef, float(LANES)), axis=-1, keepdims=True)
    lg2 = jnp.where(lanef == i1, -jnp.inf, lg)
    m2 = jnp.max(lg2, axis=-1, keepdims=True)
    i2 = jnp.min(jnp.where(lg2 == m2, lanef, float(LANES)), axis=-1, keepdims=True)
    e = jnp.exp(m2 - m1)
    w1 = 1.0 / (1.0 + e)
    w2 = e / (1.0 + e)
    out = jnp.where(lane == 0, i1, 0.0)
    out = jnp.where(lane == 1, i2, out)
    out = jnp.where(lane == 2, w1, out)
    return jnp.where(lane == 3, w2, out)


def _pack_bf16_pairs(h):
    bits = lax.bitcast_convert_type(h, jnp.uint32)
    bits = bits + (jnp.uint32(0x7FFF) + ((bits >> 16) & jnp.uint32(1)))
    half = h.shape[1] // 2
    packed = (bits[:, half:] & jnp.uint32(0xFFFF0000)) | (bits[:, :half] >> 16)
    return lax.bitcast_convert_type(packed, F32)


def _unpack_bf16_pairs(words):
    p = lax.bitcast_convert_type(words, jnp.uint32)
    lo = lax.bitcast_convert_type(p << 16, F32)
    hi = lax.bitcast_convert_type(p & jnp.uint32(0xFFFF0000), F32)
    return jnp.concatenate([lo, hi], axis=1).astype(BF16)


def _outproj_kernel(*refs, n_in, router):
    o_refs = refs[:n_in]
    w_refs = refs[n_in:2 * n_in]
    x_ref, g_ref, lng_ref, lnb_ref, sc_ref, sh_ref = refs[2 * n_in:2 * n_in + 6]
    rest = refs[2 * n_in + 6:]
    y = jnp.dot(o_refs[0][...], w_refs[0][...], preferred_element_type=F32)
    for o_r, w_r in zip(o_refs[1:], w_refs[1:]):
        y = y + jnp.dot(o_r[...], w_r[...], preferred_element_type=F32)
    xn = _layernorm_rows(DN_ALPHA * x_ref[...] + g_ref[...] * y, lng_ref[...], lnb_ref[...])
    h2 = xn * (1.0 + sc_ref[...]) + sh_ref[...]
    if router:
        wr_ref, xo_ref, h_ref, r_ref = rest
        r_ref[...] = _top2_router(h2, wr_ref)
        h_ref[...] = _pack_bf16_pairs(h2)
    else:
        xo_ref, h_ref = rest
        h_ref[...] = h2.astype(BF16)
    xo_ref[...] = xn


def _out_proj(o_list, w_list, xs, mods_l, ln_g, ln_b, n_tiles, tiles_per_batch, n_batch, w_router=None):
    d = xs.shape[1]
    tm = ROW_TILE
    t_rows = n_tiles * tm
    grp = lambda t: jnp.minimum(t // tiles_per_batch, n_batch)
    mod = lambda k: pl.BlockSpec((None, None, 1, d), lambda t: (grp(t), k, 0, 0))
    in_specs = [pl.BlockSpec((tm, o.shape[1]), lambda t: (t, 0)) for o in o_list]
    in_specs += [pl.BlockSpec(w.shape, lambda t: (0, 0)) for w in w_list]
    in_specs += [pl.BlockSpec((tm, d), lambda t: (t, 0)), mod(2),
                 pl.BlockSpec((1, d), lambda t: (0, 0)), pl.BlockSpec((1, d), lambda t: (0, 0)),
                 mod(4), mod(3)]
    args = list(o_list) + list(w_list) + [xs, mods_l, ln_g.reshape(1, d), ln_b.reshape(1, d), mods_l, mods_l]
    out_shape = [jax.ShapeDtypeStruct((t_rows, d), F32), jax.ShapeDtypeStruct((t_rows, d), BF16)]
    out_specs = [pl.BlockSpec((tm, d), lambda t: (t, 0)), pl.BlockSpec((tm, d), lambda t: (t, 0))]
    if w_router is not None:
        out_shape[1] = jax.ShapeDtypeStruct((t_rows, d // 2), F32)
        out_specs[1] = pl.BlockSpec((tm, d // 2), lambda t: (t, 0))
        in_specs.append(pl.BlockSpec(w_router.shape, lambda t: (0, 0)))
        args.append(w_router)
        out_shape.append(jax.ShapeDtypeStruct((t_rows, LANES), F32))
        out_specs.append(pl.BlockSpec((tm, LANES), lambda t: (t, 0)))
    return pl.pallas_call(
        functools.partial(_outproj_kernel, n_in=len(o_list), router=w_router is not None),
        grid=(n_tiles,),
        in_specs=in_specs,
        out_specs=out_specs,
        out_shape=out_shape,
        compiler_params=pltpu.CompilerParams(vmem_limit_bytes=VMEM_LIMIT),
    )(*args)


def _swiglu_acc(h, wg_ref, wu_ref, wd_ref, acc_ref, c):
    g = jnp.dot(h, wg_ref[...], preferred_element_type=F32)
    u = jnp.dot(h, wu_ref[...], preferred_element_type=F32)
    a = (g * (1.0 / (1.0 + jnp.exp(-g))) * u).astype(BF16)
    part = jnp.dot(a, wd_ref[...], preferred_element_type=F32)

    @pl.when(c == 0)
    def _():
        acc_ref[...] = part

    @pl.when(c > 0)
    def _():
        acc_ref[...] += part


def _ffn_dense_kernel(h_ref, wg_ref, wu_ref, wd_ref, x_ref, g_ref, lng_ref, lnb_ref, xo_ref, acc_ref):
    c = pl.program_id(1)
    _swiglu_acc(h_ref[...], wg_ref, wu_ref, wd_ref, acc_ref, c)

    @pl.when(c == pl.num_programs(1) - 1)
    def _():
        z = DN_ALPHA * x_ref[...] + g_ref[...] * acc_ref[...]
        xo_ref[...] = _layernorm_rows(z, lng_ref[...], lnb_ref[...])


def _ffn_dense(h2, w_up, w_down, xs, mods_l, ln_g, ln_b, n_tiles, tiles_per_batch, n_batch):
    d = xs.shape[1]
    tm = ROW_TILE
    t_rows = n_tiles * tm
    ff = w_down.shape[0]
    nc = ff // FF_CHUNK
    grp = lambda t: jnp.minimum(t // tiles_per_batch, n_batch)
    return pl.pallas_call(
        _ffn_dense_kernel,
        grid=(n_tiles, nc),
        in_specs=[
            pl.BlockSpec((tm, d), lambda t, c: (t, 0)),
            pl.BlockSpec((d, FF_CHUNK), lambda t, c: (0, c)),
            pl.BlockSpec((d, FF_CHUNK), lambda t, c: (0, nc + c)),
            pl.BlockSpec((FF_CHUNK, d), lambda t, c: (c, 0)),
            pl.BlockSpec((tm, d), lambda t, c: (t, 0)),
            pl.BlockSpec((None, None, 1, d), lambda t, c: (grp(t), 5, 0, 0)),
            pl.BlockSpec((1, d), lambda t, c: (0, 0)),
            pl.BlockSpec((1, d), lambda t, c: (0, 0)),
        ],
        out_specs=pl.BlockSpec((tm, d), lambda t, c: (t, 0)),
        out_shape=jax.ShapeDtypeStruct((t_rows, d), F32),
        scratch_shapes=[pltpu.VMEM((tm, d), F32)],
        compiler_params=pltpu.CompilerParams(vmem_limit_bytes=VMEM_LIMIT),
    )(h2, w_up, w_up, w_down, xs, mods_l, ln_g.reshape(1, d), ln_b.reshape(1, d))


def _ffn_expert_kernel(te_ref, nu_ref, h_ref, wg_ref, wu_ref, wd_ref, y_ref, acc_ref):
    i = pl.program_id(0)
    c = pl.program_id(1)

    @pl.when(i < nu_ref[0])
    def _():
        _swiglu_acc(_unpack_bf16_pairs(h_ref[...]), wg_ref, wu_ref, wd_ref, acc_ref, c)

    @pl.when(c == pl.num_programs(1) - 1)
    def _():
        y_ref[...] = acc_ref[...]


def _ffn_experts(hs, w_up, w_down, layer, tile_expert, n_used):
    p_rows = hs.shape[0]
    d = w_down.shape[-1]
    tm = MOE_TILE
    n_tiles = p_rows // tm
    ef = w_down.shape[2]
    nc = ef // EXPERT_FF_CHUNK
    chunk = lambda i, c, nu: jnp.where(i < nu[0], c, nc - 1)
    grid_spec = pltpu.PrefetchScalarGridSpec(
        num_scalar_prefetch=2,
        grid=(n_tiles, nc),
        in_specs=[
            pl.BlockSpec((tm, d // 2), lambda i, c, te, nu: (i, 0)),
            pl.BlockSpec((None, None, d, EXPERT_FF_CHUNK),
                         lambda i, c, te, nu: (layer, te[i], 0, chunk(i, c, nu))),
            pl.BlockSpec((None, None, d, EXPERT_FF_CHUNK),
                         lambda i, c, te, nu: (layer, te[i], 0, nc + chunk(i, c, nu))),
            pl.BlockSpec((None, None, EXPERT_FF_CHUNK, d),
                         lambda i, c, te, nu: (layer, te[i], chunk(i, c, nu), 0)),
        ],
        out_specs=pl.BlockSpec((tm, d), lambda i, c, te, nu: (i, 0)),
        scratch_shapes=[pltpu.VMEM((tm, d), F32)],
    )
    return pl.pallas_call(
        _ffn_expert_kernel,
        grid_spec=grid_spec,
        out_shape=jax.ShapeDtypeStruct((p_rows, d), F32),
        compiler_params=pltpu.CompilerParams(vmem_limit_bytes=VMEM_LIMIT),
    )(tile_expert, n_used, hs, w_up, w_up, w_down)


def _combine_kernel(y0_ref, y1_ref, r_ref, x_ref, g_ref, lng_ref, lnb_ref, xo_ref):
    r = r_ref[...]
    f = r[:, 2:3] * y0_ref[...] + r[:, 3:4] * y1_ref[...]
    z = DN_ALPHA * x_ref[...] + g_ref[...] * f
    xo_ref[...] = _layernorm_rows(z, lng_ref[...], lnb_ref[...])


def _moe_combine(y0, y1, route, xs, mods_l, ln_g, ln_b, n_tiles, tiles_per_batch, n_batch):
    d = xs.shape[1]
    tm = ROW_TILE
    t_rows = n_tiles * tm
    grp = lambda t: jnp.minimum(t // tiles_per_batch, n_batch)
    row = pl.BlockSpec((tm, d), lambda t: (t, 0))
    return pl.pallas_call(
        _combine_kernel,
        grid=(n_tiles,),
        in_specs=[row, row, pl.BlockSpec((tm, LANES), lambda t: (t, 0)), row,
                  pl.BlockSpec((None, None, 1, d), lambda t: (grp(t), 5, 0, 0)),
                  pl.BlockSpec((1, d), lambda t: (0, 0)), pl.BlockSpec((1, d), lambda t: (0, 0))],
        out_specs=row,
        out_shape=jax.ShapeDtypeStruct((t_rows, d), F32),
        compiler_params=pltpu.CompilerParams(vmem_limit_bytes=VMEM_LIMIT),
    )(y0, y1, route, xs, mods_l, ln_g.reshape(1, d), ln_b.reshape(1, d))


def _routing_plan(route, n_rows):
    tm = MOE_TILE
    e_idx = route[:n_rows, 0:2].astype(jnp.int32).reshape(-1)
    onehot = (e_idx[:, None] == jnp.arange(N_EXPERTS, dtype=jnp.int32)[None, :]).astype(jnp.int32)
    csum = jnp.cumsum(onehot, axis=0)
    counts = csum[-1]
    rank = jnp.sum((csum - onehot) * onehot, axis=1)
    padded = ((counts + tm - 1) // tm) * tm
    ends = jnp.cumsum(padded)
    starts = ends - padded
    dest = starts[e_idx] + rank
    n_tiles = (2 * n_rows + N_EXPERTS * (tm - 1)) // tm
    p_rows = n_tiles * tm
    row_token = jnp.zeros((p_rows,), jnp.int32).at[dest].set(jnp.arange(2 * n_rows, dtype=jnp.int32) // 2)
    tile_start = jnp.arange(n_tiles, dtype=jnp.int32) * tm
    tile_expert = jnp.minimum(jnp.sum((tile_start[:, None] >= ends[None, :]).astype(jnp.int32), axis=1),
                              N_EXPERTS - 1)
    n_used = (ends[-1] // tm).astype(jnp.int32).reshape(1)
    last = tile_expert[jnp.maximum(n_used[0] - 1, 0)]
    tile_expert = jnp.where(jnp.arange(n_tiles) < n_used[0], tile_expert, last).astype(jnp.int32)
    return row_token, dest.reshape(n_rows, 2), tile_expert, n_used


def _rope_tables(n_lat):
    t = jnp.arange(n_lat, dtype=jnp.int32)
    row = (t // GRID_W).astype(F32)
    col = (t % GRID_W).astype(F32)
    inv = ROPE_THETA ** (-jnp.arange(0, ROPE_AXIS_DIM, 2, dtype=F32) / ROPE_AXIS_DIM)
    ar = row[:, None] * inv[None, :]
    ac = col[:, None] * inv[None, :]
    ang = jnp.concatenate([ar, ar, ac, ac], axis=-1)
    cos = jnp.tile(jnp.cos(ang), (1, LANES // HEAD_DIM))
    sin = jnp.tile(jnp.sin(ang), (1, LANES // HEAD_DIM))
    sign = jnp.where((jnp.arange(LANES) % 32) < 16, -1.0, 1.0).astype(F32)
    cos = jnp.concatenate([cos, jnp.ones((ROW_TILE, LANES), F32)], axis=0)
    sin = jnp.concatenate([sin * sign[None, :], jnp.zeros((ROW_TILE, LANES), F32)], axis=0)
    return cos, sin


def _lambda_init(layer):
    return 0.8 - 0.6 * math.exp(-0.3 * layer)


def kernel(x, c, ctx, c_ctx, w_mod, b_mod, ln_g, ln_b, w_in_ab, w_out_ab, sink_a, rpb_b, w_in_c, w_out_c,
           lam_c, subln_c, w_ffn_up, w_ffn_down, w_router, w_exp_up, w_exp_down):
    n_batch, n_lat, d = x.shape
    n_ctx = ctx.shape[1]
    assert d == D_MODEL and n_batch * n_ctx == ROW_TILE and n_lat % ROW_TILE == 0
    assert n_batch + 1 <= 8 and n_lat % C_KCHUNK == 0 and n_ctx % C_QTILE == 0
    rows = n_lat // GRID_W
    assert rows >= 12 and rows % B_QROWS == 0
    tiles_per_batch = n_lat // ROW_TILE
    lat_tiles = n_batch * tiles_per_batch
    lat_rows = n_batch * n_lat

    xs = jnp.concatenate([x.reshape(lat_rows, d), ctx.reshape(n_batch * n_ctx, d)], axis=0)
    cond = jnp.zeros((8, d), F32).at[:n_batch].set(c).at[n_batch].set(c_ctx)
    mods = _mod_vectors(cond, w_mod, b_mod).reshape(DEPTH, 8, 6, 1, d)
    cos_t, sin_t = _rope_tables(n_lat)
    w_exp_up_b = w_exp_up.astype(BF16)
    w_exp_down_b = w_exp_down.astype(BF16)

    perm = np.array([(j + (A_HEADS // 2) * half) * HEAD_DIM + dd
                     for j in range(A_HEADS // 2) for half in range(2) for dd in range(HEAD_DIM)])

    for l in range(DEPTH):
        last = l == DEPTH - 1
        i = l // 2
        all_tiles = lat_tiles + 1
        n_tiles = lat_tiles if last else all_tiles
        mods_l = mods[l]
        if l % 2 == 0:
            w_in = w_in_ab[i]
            w_in = jnp.concatenate([w_in[:, :A_Q][:, perm], w_in[:, A_Q:]], axis=1).astype(BF16)
            segs = [(A_Q, True, ATTN_SCALE), (A_KV, True, 1.0), (A_KV, False, 1.0),
                    (B_W, False, ATTN_SCALE * math.log2(math.e)), (B_W, False, 1.0), (B_W, False, 1.0)]
            qa, ka, va, qb, kb, vb = _in_proj(xs, mods_l, cos_t, sin_t, w_in, segs, all_tiles,
                                              tiles_per_batch, n_batch)
            sink = sink_a[i].astype(F32)
            oa = _attn_a(qa, ka, va, sink, n_batch, n_lat, n_ctx, not last)
            ob = _attn_b(qb, kb, vb.T, _nbr_bias_table(rpb_b[i], rows), n_batch, n_lat, n_ctx, not last)
            w_out = w_out_ab[i]
            o_list = [oa, ob]
            w_list = [w_out[:A_Q][perm].astype(BF16), w_out[A_Q:].astype(BF16)]
        else:
            segs = [(C_QK, True, ATTN_SCALE * math.log2(math.e)), (C_QK, True, 1.0), (C_OUT, False, 1.0)]
            qc, kc, vc = _in_proj(xs, mods_l, cos_t, sin_t, w_in_c[i].astype(BF16), segs, all_tiles,
                                  tiles_per_batch, n_batch)
            oc = _attn_c(qc, kc, vc.T, lam_c[i].astype(F32), subln_c[i].astype(F32), _lambda_init(l),
                         n_batch, n_lat, n_ctx, not last)
            o_list = [oc]
            w_list = [w_out_c[i].astype(BF16)]

        if l % 2 == 0:
            xs, h2 = _out_proj(o_list, w_list, xs, mods_l, ln_g[l, 0], ln_b[l, 0], n_tiles,
                               tiles_per_batch, n_batch)
            xs = _ffn_dense(h2, w_ffn_up[i].astype(BF16), w_ffn_down[i].astype(BF16), xs, mods_l,
                            ln_g[l, 1], ln_b[l, 1], n_tiles, tiles_per_batch, n_batch)
        else:
            wr = jnp.zeros((d, LANES), F32).at[:, :N_EXPERTS].set(w_router[i])
            wr_top = _bf16_truncate(wr)
            wr = jnp.concatenate([wr_top.astype(BF16), (wr - wr_top).astype(BF16)], axis=1)
            xs, h2, route = _out_proj(o_list, w_list, xs, mods_l, ln_g[l, 0], ln_b[l, 0], n_tiles,
                                      tiles_per_batch, n_batch, w_router=wr)
            n_rows = n_tiles * ROW_TILE
            row_token, dest, tile_expert, n_used = _routing_plan(route, n_rows)
            hs = jnp.take(h2, row_token, axis=0, mode="clip")
            ys = _ffn_experts(hs, w_exp_up_b, w_exp_down_b, i, tile_expert, n_used)
            y0 = jnp.take(ys, dest[:, 0], axis=0, mode="clip")
            y1 = jnp.take(ys, dest[:, 1], axis=0, mode="clip")
            xs = _moe_combine(y0, y1, route, xs, mods_l, ln_g[l, 1], ln_b[l, 1], n_tiles,
                              tiles_per_batch, n_batch)
    return xs[:lat_rows].reshape(n_batch, n_lat, d)
```

```python
import functools
import math

import jax
import jax.numpy as jnp
import numpy as np
from jax import lax
from jax.experimental import pallas as pl
from jax.experimental.pallas import tpu as pltpu

F32 = jnp.float32
BF16 = jnp.bfloat16

D_MODEL = 1024
DEPTH = 4
GRID_W = 64
HEAD_DIM = 64
LANES = 128
ATTN_SCALE = HEAD_DIM ** -0.5
A_BLOCK = 128
A_WINDOW = 128
A_HEADS = 8
A_KV_HEADS = 2
B_HEADS = 8
NA_KH = 8
NA_KW = 16
C_HEADS = 8
ROPE_THETA = 10000.0
ROPE_AXIS_DIM = HEAD_DIM // 2
FF_DIM = 2816
N_EXPERTS = 8
EXPERT_FF = 3584
LN_EPS = 1e-5
DN_ALPHA = (2 * DEPTH) ** 0.25
MASK_VALUE = -1e30
A_Q = A_HEADS * HEAD_DIM
A_KV = A_KV_HEADS * HEAD_DIM
B_W = B_HEADS * HEAD_DIM
C_QK = 1024
C_OUT = 1024

ROW_TILE = 512
MOE_TILE = 512
FF_CHUNK = 1408
EXPERT_FF_CHUNK = 1792
B_QROWS = 2
B_KROWS = B_QROWS + NA_KH
B_PAIRS = 4
C_QTILE = 256
C_KCHUNK = 512
C_UNROLL = 4
VMEM_LIMIT = 56 * 1024 * 1024


def _lane_iota(shape):
    return lax.broadcasted_iota(jnp.int32, shape, len(shape) - 1)


def _layernorm_rows(z, g, b):
    mu = jnp.mean(z, axis=-1, keepdims=True)
    zc = z - mu
    var = jnp.mean(zc * zc, axis=-1, keepdims=True)
    return zc * lax.rsqrt(var + LN_EPS) * g + b


def _mod_kernel(c_ref, w_ref, b_ref, o_ref):
    c = c_ref[...]
    s = c * (1.0 / (1.0 + jnp.exp(-c)))
    o_ref[0] = jnp.dot(s, w_ref[0], preferred_element_type=F32,
                       precision=lax.Precision.HIGHEST) + b_ref[0]


def _mod_vectors(cond, w_mod, b_mod):
    depth, d, n6 = w_mod.shape
    tn = 1536
    return pl.pallas_call(
        _mod_kernel,
        grid=(depth, n6 // tn),
        in_specs=[
            pl.BlockSpec((8, d), lambda l, j: (0, 0)),
            pl.BlockSpec((1, d, tn), lambda l, j: (l, 0, j)),
            pl.BlockSpec((1, 1, tn), lambda l, j: (l, 0, j)),
        ],
        out_specs=pl.BlockSpec((1, 8, tn), lambda l, j: (l, 0, j)),
        out_shape=jax.ShapeDtypeStruct((depth, 8, n6), F32),
        compiler_params=pltpu.CompilerParams(vmem_limit_bytes=VMEM_LIMIT),
    )(cond, w_mod, b_mod.reshape(depth, 1, n6))


def _rope_slab(a, cos, sin_signed, low16):
    fwd = pltpu.roll(a, LANES - 16, axis=1)
    bwd = pltpu.roll(a, 16, axis=1)
    return a * cos + jnp.where(low16, fwd, bwd) * sin_signed


def _proj_kernel(x_ref, sc_ref, sh_ref, cos_ref, sin_ref, w_ref, *out_refs, segments):
    x = x_ref[...]
    h = (x * (1.0 + sc_ref[...]) + sh_ref[...]).astype(BF16)
    cos = cos_ref[...]
    sin = sin_ref[...]
    low16 = (_lane_iota(cos.shape) % 32) < 16
    col = 0
    for o_ref, (width, rope, scale) in zip(out_refs, segments):
        for c0 in range(0, width, 256):
            cw = min(256, width - c0)
            acc = jnp.dot(h, w_ref[:, col + c0:col + c0 + cw], preferred_element_type=F32)
            slabs = []
            for s0 in range(0, cw, LANES):
                a = acc[:, s0:s0 + LANES]
                if rope:
                    a = _rope_slab(a, cos, sin, low16)
                if scale != 1.0:
                    a = a * scale
                slabs.append(a.astype(BF16))
            o_ref[:, c0:c0 + cw] = slabs[0] if len(slabs) == 1 else jnp.concatenate(slabs, axis=1)
        col += width


def _in_proj(xs, mods_l, cos_t, sin_t, w, segments, n_tiles, tiles_per_batch, n_batch):
    t_rows, d = xs.shape
    tm = ROW_TILE
    grp = lambda t: jnp.minimum(t // tiles_per_batch, n_batch)
    pos = lambda t: jnp.where(t < n_batch * tiles_per_batch, t % tiles_per_batch, tiles_per_batch)
    out_shape = [jax.ShapeDtypeStruct((t_rows, s[0]), BF16) for s in segments]
    out_specs = [pl.BlockSpec((tm, s[0]), lambda t: (t, 0)) for s in segments]
    return pl.pallas_call(
        functools.partial(_proj_kernel, segments=tuple(segments)),
        grid=(n_tiles,),
        in_specs=[
            pl.BlockSpec((tm, d), lambda t: (t, 0)),
            pl.BlockSpec((None, None, 1, d), lambda t: (grp(t), 1, 0, 0)),
            pl.BlockSpec((None, None, 1, d), lambda t: (grp(t), 0, 0, 0)),
            pl.BlockSpec((tm, LANES), lambda t: (pos(t), 0)),
            pl.BlockSpec((tm, LANES), lambda t: (pos(t), 0)),
            pl.BlockSpec(w.shape, lambda t: (0, 0)),
        ],
        out_specs=out_specs,
        out_shape=out_shape,
        compiler_params=pltpu.CompilerParams(vmem_limit_bytes=VMEM_LIMIT),
    )(xs, mods_l, mods_l, cos_t, sin_t, w)


def _attn_a_kernel(sink_ref, q_ref, k_ref, vt_ref, kc_ref, vct_ref, o_ref, *, n_blocks, n_lat):
    n = pl.program_id(1)
    blk = A_BLOCK
    win = 3 * blk
    start = pl.multiple_of(jnp.clip((n - 1) * blk, 0, n_lat - win), blk)
    keys = jnp.concatenate([k_ref[pl.ds(start, win), :], kc_ref[...]], axis=0)
    vals_t = jnp.concatenate([vt_ref[:, pl.ds(start, win)], vct_ref[...]], axis=1)
    nk = keys.shape[0]
    qbase = jnp.where(n < n_blocks, n * blk, -(1 << 20))
    kidx = lax.broadcasted_iota(jnp.int32, (nk, 2 * blk), 0)
    qidx = lax.broadcasted_iota(jnp.int32, (nk, 2 * blk), 1)
    qpos = qbase + jnp.where(qidx >= blk, qidx - blk, qidx)
    in_band = jnp.abs(start + kidx - qpos) <= A_WINDOW
    bias = jnp.where((kidx >= win) | in_band, 0.0, MASK_VALUE).astype(F32)
    lo = _lane_iota((blk, LANES)) < HEAD_DIM
    zero = jnp.zeros((blk, LANES), BF16)
    col_hi = _lane_iota((1, 2 * blk)) >= blk
    dn = (((1,), (1,)), ((), ()))
    scores = []
    for j in range(A_HEADS // 2):
        qc = q_ref[:, j * LANES:(j + 1) * LANES]
        qq = jnp.concatenate([jnp.where(lo, qc, zero), jnp.where(lo, zero, qc)], axis=0)
        scores.append(lax.dot_general(keys, qq, dn, preferred_element_type=F32) + bias)
    for j, s in enumerate(scores):
        sink = jnp.where(col_hi, sink_ref[j + A_HEADS // 2], sink_ref[j])
        m = jnp.maximum(jnp.max(s, axis=0, keepdims=True), sink)
        p = jnp.exp2(s - m)
        l = jnp.sum(p, axis=0, keepdims=True) + jnp.exp2(sink - m)
        ot = jnp.dot(vals_t, p.astype(BF16), preferred_element_type=F32) * (1.0 / l)
        o = jnp.concatenate([ot[:HEAD_DIM, :blk], ot[HEAD_DIM:, blk:]], axis=0)
        o_ref[:, j * LANES:(j + 1) * LANES] = o.T.astype(BF16)


def _attn_a(q, k, vt, sink, n_batch, n_lat, n_ctx, with_ctx):
    t_rows = q.shape[0]
    blk = A_BLOCK
    nb = n_lat // blk
    ncb = n_ctx // blk if with_ctx else 0
    qrow = lambda b, n: jnp.where(n < nb, b * nb + n, n_batch * nb + b * (n_ctx // blk) + (n - nb))
    ctx_blk = lambda b: (n_batch * n_lat) // n_ctx + b
    grid_spec = pltpu.PrefetchScalarGridSpec(
        num_scalar_prefetch=1,
        grid=(n_batch, nb + ncb),
        in_specs=[
            pl.BlockSpec((blk, A_Q), lambda b, n, s: (qrow(b, n), 0)),
            pl.BlockSpec((n_lat, A_KV), lambda b, n, s: (b, 0)),
            pl.BlockSpec((A_KV, n_lat), lambda b, n, s: (0, b)),
            pl.BlockSpec((n_ctx, A_KV), lambda b, n, s: (ctx_blk(b), 0)),
            pl.BlockSpec((A_KV, n_ctx), lambda b, n, s: (0, ctx_blk(b))),
        ],
        out_specs=pl.BlockSpec((blk, A_Q), lambda b, n, s: (qrow(b, n), 0)),
    )
    return pl.pallas_call(
        functools.partial(_attn_a_kernel, n_blocks=nb, n_lat=n_lat),
        grid_spec=grid_spec,
        out_shape=jax.ShapeDtypeStruct((t_rows if with_ctx else n_batch * n_lat, A_Q), BF16),
        compiler_params=pltpu.CompilerParams(vmem_limit_bytes=VMEM_LIMIT),
    )(sink, q, k, vt, k, vt)


def _attn_b_kernel(q_ref, k_ref, vt_ref, kc_ref, vct_ref, bias_ref, o_ref, *, rows, n_steps):
    r = pl.program_id(2)
    nq = B_QROWS * GRID_W
    nk = B_KROWS * GRID_W
    r0 = jnp.where(r < n_steps, r * B_QROWS, 0)
    ws = jnp.clip(r0 - NA_KH // 2, 0, rows - B_KROWS)
    start = pl.multiple_of(ws * GRID_W, LANES)
    lo = _lane_iota((nq, LANES)) < HEAD_DIM
    zero = jnp.zeros((nq, LANES), BF16)
    dn = (((1,), (1,)), ((), ()))
    scores = []
    for g in range(B_PAIRS):
        cols = slice(g * LANES, (g + 1) * LANES)
        q = q_ref[:, cols]
        qq = jnp.concatenate([jnp.where(lo, q, zero), jnp.where(lo, zero, q)], axis=0)
        s_loc = lax.dot_general(k_ref[pl.ds(start, nk), cols], qq, dn,
                                preferred_element_type=F32) + bias_ref[g]
        s_ctx = lax.dot_general(kc_ref[:, cols], qq, dn, preferred_element_type=F32)
        scores.append((s_loc, s_ctx))
    for g, (s_loc, s_ctx) in enumerate(scores):
        cols = slice(g * LANES, (g + 1) * LANES)
        m = jnp.maximum(jnp.max(s_loc, axis=0, keepdims=True), jnp.max(s_ctx, axis=0, keepdims=True))
        p_loc = jnp.exp2(s_loc - m)
        p_ctx = jnp.exp2(s_ctx - m)
        l = jnp.sum(p_loc, axis=0, keepdims=True) + jnp.sum(p_ctx, axis=0, keepdims=True)
        ot = (jnp.dot(vt_ref[cols, pl.ds(start, nk)], p_loc.astype(BF16), preferred_element_type=F32)
              + jnp.dot(vct_ref[cols, :], p_ctx.astype(BF16), preferred_element_type=F32)) * (1.0 / l)
        o = jnp.concatenate([ot[:HEAD_DIM, :nq], ot[HEAD_DIM:, nq:]], axis=0)
        o_ref[:, cols] = o.T.astype(BF16)


def _nbr_bias_table(rpb, rows):
    n_steps = rows // B_QROWS
    steps = [min(2, n_steps - 1), 0, 1, n_steps - 2, n_steps - 1]
    w = GRID_W
    n_heads = rpb.shape[0]
    c = np.arange(w)[None, :]
    kc = np.arange(w)[:, None]
    cs = np.clip(c - NA_KW // 2, 0, w - NA_KW)
    col_ok = (kc >= cs) & (kc < cs + NA_KW)
    onehot = (((kc - c + NA_KW - 1)[None] == np.arange(2 * NA_KW - 1)[:, None, None]) & col_ok[None])
    toe = jnp.einsum("hrd,dkc->hrkc", rpb.astype(F32), jnp.asarray(onehot, F32),
                     precision=lax.Precision.HIGHEST)
    toe = jnp.where(col_ok[None, None], toe * math.log2(math.e), MASK_VALUE)
    masked = jnp.full((n_heads, w, w), MASK_VALUE, F32)
    tabs = []
    for st in steps:
        r0 = st * B_QROWS
        ws = int(np.clip(r0 - NA_KH // 2, 0, rows - B_KROWS))
        q_cols = []
        for rq in range(B_QROWS):
            r = r0 + rq
            rs = int(np.clip(r - NA_KH // 2, 0, rows - NA_KH))
            blocks = [toe[:, ws + ki - r + NA_KH - 1] if rs <= ws + ki < rs + NA_KH else masked
                      for ki in range(B_KROWS)]
            q_cols.append(jnp.concatenate(blocks, axis=1))
        tabs.append(jnp.concatenate(q_cols, axis=2))
    tabs.append(jnp.full_like(tabs[0], MASK_VALUE))
    tab = jnp.stack(tabs)
    n_var, _, nk, nq = tab.shape
    tab = tab.reshape(n_var, n_heads // 2, 2, nk, nq)
    return jnp.transpose(tab, (0, 1, 3, 2, 4)).reshape(n_var, n_heads // 2, nk, 2 * nq)


def _attn_b(q, k, vt, bias_tab, n_batch, n_lat, n_ctx, with_ctx):
    t_rows = q.shape[0]
    rows = n_lat // GRID_W
    nq = B_QROWS * GRID_W
    nk = B_KROWS * GRID_W
    gw = B_PAIRS * LANES
    n_steps = rows // B_QROWS
    n_cstep = n_ctx // nq if with_ctx else 0
    qrow = lambda b, r: jnp.where(r < n_steps, b * n_steps + r,
                                  n_batch * n_steps + b * (n_ctx // nq) + (r - n_steps))
    ctx_blk = lambda b: (n_batch * n_lat) // n_ctx + b

    def variant(r):
        v = jnp.where(r == 0, 1, 0)
        v = jnp.where(r == 1, 2, v)
        v = jnp.where(r == n_steps - 2, 3, v)
        v = jnp.where(r == n_steps - 1, 4, v)
        return jnp.where(r >= n_steps, 5, v)

    return pl.pallas_call(
        functools.partial(_attn_b_kernel, rows=rows, n_steps=n_steps),
        grid=(n_batch, B_HEADS // (2 * B_PAIRS), n_steps + n_cstep),
        in_specs=[
            pl.BlockSpec((nq, gw), lambda b, j, r: (qrow(b, r), j)),
            pl.BlockSpec((n_lat, gw), lambda b, j, r: (b, j)),
            pl.BlockSpec((gw, n_lat), lambda b, j, r: (j, b)),
            pl.BlockSpec((n_ctx, gw), lambda b, j, r: (ctx_blk(b), j)),
            pl.BlockSpec((gw, n_ctx), lambda b, j, r: (j, ctx_blk(b))),
            pl.BlockSpec((None, B_PAIRS, nk, 2 * nq), lambda b, j, r: (variant(r), j, 0, 0)),
        ],
        out_specs=pl.BlockSpec((nq, gw), lambda b, j, r: (qrow(b, r), j)),
        out_shape=jax.ShapeDtypeStruct((t_rows if with_ctx else n_batch * n_lat, B_W), BF16),
        compiler_params=pltpu.CompilerParams(vmem_limit_bytes=VMEM_LIMIT),
    )(q, k, vt, k, vt, bias_tab)


def _attn_c_kernel(lam_ref, q_ref, k_ref, vt_ref, kc_ref, vct_ref, g_ref, o_ref,
                   s_a, s_b, s_c, m_scr, l_scr, acc_scr, *, n_qlat, n_kchunks, lam_init):
    i = pl.program_id(2)
    tq = q_ref.shape[0]
    tk = C_KCHUNK
    q = q_ref[...]
    lo = _lane_iota((tq, LANES)) < HEAD_DIM
    zero = jnp.zeros((tq, LANES), BF16)
    q_maps = (jnp.where(lo, q, zero), jnp.where(lo, zero, q))
    dn = (((1,), (1,)), ((), ()))

    def scores(kk, s_ref):
        for mi in range(2):
            s_ref[mi] = lax.dot_general(kk, q_maps[mi], dn, preferred_element_type=F32)

    def softmax_pv(s_ref, vvt):
        for mi in range(2):
            s = s_ref[mi]
            m = m_scr[mi]
            m_new = jnp.maximum(m, jnp.max(s, axis=0, keepdims=True))
            alpha = jnp.exp2(m - m_new)
            p = jnp.exp2(s - m_new)
            l_scr[mi] = alpha * l_scr[mi] + jnp.sum(p, axis=0, keepdims=True)
            acc_scr[mi] = alpha * acc_scr[mi] + jnp.dot(vvt, p.astype(BF16),
                                                        preferred_element_type=F32)
            m_scr[mi] = m_new

    def chunk_slice(c):
        return pl.ds(c * tk if isinstance(c, int) else pl.multiple_of(c * tk, tk), tk)

    def k_chunk(c):
        return k_ref[chunk_slice(c), :]

    def vt_chunk(c):
        return vt_ref[:, chunk_slice(c)]

    m_scr[...] = jnp.full(m_scr.shape, -jnp.inf, F32)
    l_scr[...] = jnp.zeros(l_scr.shape, F32)
    acc_scr[...] = jnp.zeros(acc_scr.shape, F32)

    def finish():
        softmax_pv(s_c, vct_ref[...])
        lp = lam_ref[...]
        lam = (jnp.exp(jnp.sum(lp[0:1] * lp[1:2], axis=-1, keepdims=True))
               - jnp.exp(jnp.sum(lp[2:3] * lp[3:4], axis=-1, keepdims=True)) + lam_init)
        od = acc_scr[0] * (1.0 / l_scr[0]) - lam * (acc_scr[1] * (1.0 / l_scr[1]))
        ms = jnp.mean(od * od, axis=0, keepdims=True)
        on = (od * lax.rsqrt(ms + LN_EPS)).T
        o_ref[...] = ((on * g_ref[...]) * (1.0 - lam_init)).astype(BF16)

    @pl.when(i < n_qlat)
    def _():
        slots = (s_a, s_b)
        scores(k_chunk(0), s_a)

        def body(it, carry):
            c0 = C_UNROLL * it
            for u in range(C_UNROLL):
                scores(k_chunk(c0 + u + 1), slots[(u + 1) % 2])
                softmax_pv(slots[u % 2], vt_chunk(c0 + u))
            return carry

        n_it = (n_kchunks - 1) // C_UNROLL
        lax.fori_loop(0, n_it, body, 0)
        for c in range(C_UNROLL * n_it, n_kchunks):
            if c + 1 < n_kchunks:
                scores(k_chunk(c + 1), slots[(c + 1) % 2])
            else:
                scores(kc_ref[...], s_c)
            softmax_pv(slots[c % 2], vt_chunk(c))
        finish()

    @pl.when(i >= n_qlat)
    def _():
        scores(kc_ref[...], s_c)
        finish()


def _attn_c(q, k, vt, lam_p, subln, lam_init, n_batch, n_lat, n_ctx, with_ctx):
    t_rows = q.shape[0]
    tq = C_QTILE
    n_qlat = n_lat // tq
    n_qctx = n_ctx // tq if with_ctx else 0
    qrow = lambda b, i: jnp.where(i < n_qlat, b * n_qlat + i,
                                  n_batch * n_qlat + b * (n_ctx // tq) + (i - n_qlat))
    ctx_blk = lambda b: (n_batch * n_lat) // n_ctx + b
    return pl.pallas_call(
        functools.partial(_attn_c_kernel, n_qlat=n_qlat, n_kchunks=n_lat // C_KCHUNK,
                          lam_init=lam_init),
        grid=(n_batch, C_HEADS, n_qlat + n_qctx),
        in_specs=[
            pl.BlockSpec((4, HEAD_DIM), lambda b, h, i: (0, 0)),
            pl.BlockSpec((tq, LANES), lambda b, h, i: (qrow(b, i), h)),
            pl.BlockSpec((n_lat, LANES), lambda b, h, i: (b, h)),
            pl.BlockSpec((LANES, n_lat), lambda b, h, i: (h, b)),
            pl.BlockSpec((n_ctx, LANES), lambda b, h, i: (ctx_blk(b), h)),
            pl.BlockSpec((LANES, n_ctx), lambda b, h, i: (h, ctx_blk(b))),
            pl.BlockSpec((1, LANES), lambda b, h, i: (0, 0)),
        ],
        out_specs=pl.BlockSpec((tq, LANES), lambda b, h, i: (qrow(b, i), h)),
        out_shape=jax.ShapeDtypeStruct((t_rows if with_ctx else n_batch * n_lat, C_OUT), BF16),
        scratch_shapes=[
            pltpu.VMEM((2, C_KCHUNK, tq), F32), pltpu.VMEM((2, C_KCHUNK, tq), F32),
            pltpu.VMEM((2, n_ctx, tq), F32),
            pltpu.VMEM((2, 1, tq), F32), pltpu.VMEM((2, 1, tq), F32), pltpu.VMEM((2, LANES, tq), F32),
        ],
        compiler_params=pltpu.CompilerParams(vmem_limit_bytes=VMEM_LIMIT),
    )(lam_p, q, k, vt, k, vt, subln.reshape(1, LANES))


def _bf16_truncate(x):
    bits = lax.bitcast_convert_type(x, jnp.uint32) & jnp.uint32(0xFFFF0000)
    return lax.bitcast_convert_type(bits, F32)


def _top2_router(h, wr_ref):
    h_top = _bf16_truncate(h)
    h_hi = h_top.astype(BF16)
    h_lo = (h - h_top).astype(BF16)
    a = jnp.dot(h_hi, wr_ref[...], preferred_element_type=F32)
    b = jnp.dot(h_lo, wr_ref[:, :LANES], preferred_element_type=F32)
    lg = a[:, :LANES] + a[:, LANES:] + b
    lane = _lane_iota(lg.shape)
    lanef = lane.astype(F32)
    lg = jnp.where(lane < N_EXPERTS, lg, -jnp.inf)
    m1 = jnp.max(lg, axis=-1, keepdims=True)
    i1 = jnp.min(jnp.where(lg == m1, lanef, float(LANES)), axis=-1, keepdims=True)
    lg2 = jnp.where(lanef == i1, -jnp.inf, lg)
    m2 = jnp.max(lg2, axis=-1, keepdims=True)
    i2 = jnp.min(jnp.where(lg2 == m2, lanef, float(LANES)), axis=-1, keepdims=True)
    e = jnp.exp(m2 - m1)
    w1 = 1.0 / (1.0 + e)
    w2 = e / (1.0 + e)
    out = jnp.where(lane == 0, i1, 0.0)
    out = jnp.where(lane == 1, i2, out)
    out = jnp.where(lane == 2, w1, out)
    return jnp.where(lane == 3, w2, out)


def _pack_bf16_pairs(h):
    bits = lax.bitcast_convert_type(h, jnp.uint32)
    bits = bits + (jnp.uint32(0x7FFF) + ((bits >> 16) & jnp.uint32(1)))
    half = h.shape[1] // 2
    packed = (bits[:, half:] & jnp.uint32(0xFFFF0000)) | (bits[:, :half] >> 16)
    return lax.bitcast_convert_type(packed, F32)


def _unpack_bf16_pairs(words):
    p = lax.bitcast_convert_type(words, jnp.uint32)
    lo = lax.bitcast_convert_type(p << 16, F32)
    hi = lax.bitcast_convert_type(p & jnp.uint32(0xFFFF0000), F32)
    return jnp.concatenate([lo, hi], axis=1).astype(BF16)


def _outproj_kernel(*refs, n_in, router):
    o_refs = refs[:n_in]
    w_refs = refs[n_in:2 * n_in]
    x_ref, g_ref, lng_ref, lnb_ref, sc_ref, sh_ref = refs[2 * n_in:2 * n_in + 6]
    rest = refs[2 * n_in + 6:]
    y = jnp.dot(o_refs[0][...], w_refs[0][...], preferred_element_type=F32)
    for o_r, w_r in zip(o_refs[1:], w_refs[1:]):
        y = y + jnp.dot(o_r[...], w_r[...], preferred_element_type=F32)
    xn = _layernorm_rows(DN_ALPHA * x_ref[...] + g_ref[...] * y, lng_ref[...], lnb_ref[...])
    h2 = xn * (1.0 + sc_ref[...]) + sh_ref[...]
    if router:
        wr_ref, xo_ref, h_ref, r_ref = rest
        r_ref[...] = _top2_router(h2, wr_ref)
        h_ref[...] = _pack_bf16_pairs(h2)
    else:
        xo_ref, h_ref = rest
        h_ref[...] = h2.astype(BF16)
    xo_ref[...] = xn


def _out_proj(o_list, w_list, xs, mods_l, ln_g, ln_b, n_tiles, tiles_per_batch, n_batch, w_router=None):
    d = xs.shape[1]
    tm = ROW_TILE
    t_rows = n_tiles * tm
    grp = lambda t: jnp.minimum(t // tiles_per_batch, n_batch)
    mod = lambda k: pl.BlockSpec((None, None, 1, d), lambda t: (grp(t), k, 0, 0))
    in_specs = [pl.BlockSpec((tm, o.shape[1]), lambda t: (t, 0)) for o in o_list]
    in_specs += [pl.BlockSpec(w.shape, lambda t: (0, 0)) for w in w_list]
    in_specs += [pl.BlockSpec((tm, d), lambda t: (t, 0)), mod(2),
                 pl.BlockSpec((1, d), lambda t: (0, 0)), pl.BlockSpec((1, d), lambda t: (0, 0)),
                 mod(4), mod(3)]
    args = list(o_list) + list(w_list) + [xs, mods_l, ln_g.reshape(1, d), ln_b.reshape(1, d), mods_l, mods_l]
    out_shape = [jax.ShapeDtypeStruct((t_rows, d), F32), jax.ShapeDtypeStruct((t_rows, d), BF16)]
    out_specs = [pl.BlockSpec((tm, d), lambda t: (t, 0)), pl.BlockSpec((tm, d), lambda t: (t, 0))]
    if w_router is not None:
        out_shape[1] = jax.ShapeDtypeStruct((t_rows, d // 2), F32)
        out_specs[1] = pl.BlockSpec((tm, d // 2), lambda t: (t, 0))
        in_specs.append(pl.BlockSpec(w_router.shape, lambda t: (0, 0)))
        args.append(w_router)
        out_shape.append(jax.ShapeDtypeStruct((t_rows, LANES), F32))
        out_specs.append(pl.BlockSpec((tm, LANES), lambda t: (t, 0)))
    return pl.pallas_call(
        functools.partial(_outproj_kernel, n_in=len(o_list), router=w_router is not None),
        grid=(n_tiles,),
        in_specs=in_specs,
        out_specs=out_specs,
        out_shape=out_shape,
        compiler_params=pltpu.CompilerParams(vmem_limit_bytes=VMEM_LIMIT),
    )(*args)


def _swiglu_acc(h, wg_ref, wu_ref, wd_ref, acc_ref, c):
    g = jnp.dot(h, wg_ref[...], preferred_element_type=F32)
    u = jnp.dot(h, wu_ref[...], preferred_element_type=F32)
    a = (g * (1.0 / (1.0 + jnp.exp(-g))) * u).astype(BF16)
    part = jnp.dot(a, wd_ref[...], preferred_element_type=F32)

    @pl.when(c == 0)
    def _():
        acc_ref[...] = part

    @pl.when(c > 0)
    def _():
        acc_ref[...] += part


def _ffn_dense_kernel(h_ref, wg_ref, wu_ref, wd_ref, x_ref, g_ref, lng_ref, lnb_ref, xo_ref, acc_ref):
    c = pl.program_id(1)
    _swiglu_acc(h_ref[...], wg_ref, wu_ref, wd_ref, acc_ref, c)

    @pl.when(c == pl.num_programs(1) - 1)
    def _():
        z = DN_ALPHA * x_ref[...] + g_ref[...] * acc_ref[...]
        xo_ref[...] = _layernorm_rows(z, lng_ref[...], lnb_ref[...])


def _ffn_dense(h2, w_up, w_down, xs, mods_l, ln_g, ln_b, n_tiles, tiles_per_batch, n_batch):
    d = xs.shape[1]
    tm = ROW_TILE
    t_rows = n_tiles * tm
    ff = w_down.shape[0]
    nc = ff // FF_CHUNK
    grp = lambda t: jnp.minimum(t // tiles_per_batch, n_batch)
    return pl.pallas_call(
        _ffn_dense_kernel,
        grid=(n_tiles, nc),
        in_specs=[
            pl.BlockSpec((tm, d), lambda t, c: (t, 0)),
            pl.BlockSpec((d, FF_CHUNK), lambda t, c: (0, c)),
            pl.BlockSpec((d, FF_CHUNK), lambda t, c: (0, nc + c)),
            pl.BlockSpec((FF_CHUNK, d), lambda t, c: (c, 0)),
            pl.BlockSpec((tm, d), lambda t, c: (t, 0)),
            pl.BlockSpec((None, None, 1, d), lambda t, c: (grp(t), 5, 0, 0)),
            pl.BlockSpec((1, d), lambda t, c: (0, 0)),
            pl.BlockSpec((1, d), lambda t, c: (0, 0)),
        ],
        out_specs=pl.BlockSpec((tm, d), lambda t, c: (t, 0)),
        out_shape=jax.ShapeDtypeStruct((t_rows, d), F32),
        scratch_shapes=[pltpu.VMEM((tm, d), F32)],
        compiler_params=pltpu.CompilerParams(vmem_limit_bytes=VMEM_LIMIT),
    )(h2, w_up, w_up, w_down, xs, mods_l, ln_g.reshape(1, d), ln_b.reshape(1, d))


def _ffn_expert_kernel(te_ref, nu_ref, h_ref, wg_ref, wu_ref, wd_ref, y_ref, acc_ref):
    i = pl.program_id(0)
    c = pl.program_id(1)

    @pl.when(i < nu_ref[0])
    def _():
        _swiglu_acc(_unpack_bf16_pairs(h_ref[...]), wg_ref, wu_ref, wd_ref, acc_ref, c)

    @pl.when(c == pl.num_programs(1) - 1)
    def _():
        y_ref[...] = acc_ref[...]


def _ffn_experts(hs, w_up, w_down, layer, tile_expert, n_used):
    p_rows = hs.shape[0]
    d = w_down.shape[-1]
    tm = MOE_TILE
    n_tiles = p_rows // tm
    ef = w_down.shape[2]
    nc = ef // EXPERT_FF_CHUNK
    chunk = lambda i, c, nu: jnp.where(i < nu[0], c, nc - 1)
    grid_spec = pltpu.PrefetchScalarGridSpec(
        num_scalar_prefetch=2,
        grid=(n_tiles, nc),
        in_specs=[
            pl.BlockSpec((tm, d // 2), lambda i, c, te, nu: (i, 0)),
            pl.BlockSpec((None, None, d, EXPERT_FF_CHUNK),
                         lambda i, c, te, nu: (layer, te[i], 0, chunk(i, c, nu))),
            pl.BlockSpec((None, None, d, EXPERT_FF_CHUNK),
                         lambda i, c, te, nu: (layer, te[i], 0, nc + chunk(i, c, nu))),
            pl.BlockSpec((None, None, EXPERT_FF_CHUNK, d),
                         lambda i, c, te, nu: (layer, te[i], chunk(i, c, nu), 0)),
        ],
        out_specs=pl.BlockSpec((tm, d), lambda i, c, te, nu: (i, 0)),
        scratch_shapes=[pltpu.VMEM((tm, d), F32)],
    )
    return pl.pallas_call(
        _ffn_expert_kernel,
        grid_spec=grid_spec,
        out_shape=jax.ShapeDtypeStruct((p_rows, d), F32),
        compiler_params=pltpu.CompilerParams(vmem_limit_bytes=VMEM_LIMIT),
    )(tile_expert, n_used, hs, w_up, w_up, w_down)


def _combine_kernel(y0_ref, y1_ref, r_ref, x_ref, g_ref, lng_ref, lnb_ref, xo_ref):
    r = r_ref[...]
    f = r[:, 2:3] * y0_ref[...] + r[:, 3:4] * y1_ref[...]
    z = DN_ALPHA * x_ref[...] + g_ref[...] * f
    xo_ref[...] = _layernorm_rows(z, lng_ref[...], lnb_ref[...])


def _moe_combine(y0, y1, route, xs, mods_l, ln_g, ln_b, n_tiles, tiles_per_batch, n_batch):
    d = xs.shape[1]
    tm = ROW_TILE
    t_rows = n_tiles * tm
    grp = lambda t: jnp.minimum(t // tiles_per_batch, n_batch)
    row = pl.BlockSpec((tm, d), lambda t: (t, 0))
    return pl.pallas_call(
        _combine_kernel,
        grid=(n_tiles,),
        in_specs=[row, row, pl.BlockSpec((tm, LANES), lambda t: (t, 0)), row,
                  pl.BlockSpec((None, None, 1, d), lambda t: (grp(t), 5, 0, 0)),
                  pl.BlockSpec((1, d), lambda t: (0, 0)), pl.BlockSpec((1, d), lambda t: (0, 0))],
        out_specs=row,
        out_shape=jax.ShapeDtypeStruct((t_rows, d), F32),
        compiler_params=pltpu.CompilerParams(vmem_limit_bytes=VMEM_LIMIT),
    )(y0, y1, route, xs, mods_l, ln_g.reshape(1, d), ln_b.reshape(1, d))


def _routing_plan(route, n_rows):
    tm = MOE_TILE
    e_idx = route[:n_rows, 0:2].astype(jnp.int32).reshape(-1)
    onehot = (e_idx[:, None] == jnp.arange(N_EXPERTS, dtype=jnp.int32)[None, :]).astype(jnp.int32)
    csum = jnp.cumsum(onehot, axis=0)
    counts = csum[-1]
    rank = jnp.sum((csum - onehot) * onehot, axis=1)
    padded = ((counts + tm - 1) // tm) * tm
    ends = jnp.cumsum(padded)
    starts = ends - padded
    dest = starts[e_idx] + rank
    n_tiles = (2 * n_rows + N_EXPERTS * (tm - 1)) // tm
    p_rows = n_tiles * tm
    row_token = jnp.zeros((p_rows,), jnp.int32).at[dest].set(
        jnp.arange(2 * n_rows, dtype=jnp.int32) // 2, unique_indices=True)
    tile_start = jnp.arange(n_tiles, dtype=jnp.int32) * tm
    tile_expert = jnp.minimum(jnp.sum((tile_start[:, None] >= ends[None, :]).astype(jnp.int32), axis=1),
                              N_EXPERTS - 1)
    n_used = (ends[-1] // tm).astype(jnp.int32).reshape(1)
    last = tile_expert[jnp.maximum(n_used[0] - 1, 0)]
    tile_expert = jnp.where(jnp.arange(n_tiles) < n_used[0], tile_expert, last).astype(jnp.int32)
    return row_token, dest.reshape(n_rows, 2), tile_expert, n_used


def _rope_tables(n_lat):
    t = jnp.arange(n_lat, dtype=jnp.int32)
    row = (t // GRID_W).astype(F32)
    col = (t % GRID_W).astype(F32)
    inv = ROPE_THETA ** (-jnp.arange(0, ROPE_AXIS_DIM, 2, dtype=F32) / ROPE_AXIS_DIM)
    ar = row[:, None] * inv[None, :]
    ac = col[:, None] * inv[None, :]
    ang = jnp.concatenate([ar, ar, ac, ac], axis=-1)
    cos = jnp.tile(jnp.cos(ang), (1, LANES // HEAD_DIM))
    sin = jnp.tile(jnp.sin(ang), (1, LANES // HEAD_DIM))
    sign = jnp.where((jnp.arange(LANES) % 32) < 16, -1.0, 1.0).astype(F32)
    cos = jnp.concatenate([cos, jnp.ones((ROW_TILE, LANES), F32)], axis=0)
    sin = jnp.concatenate([sin * sign[None, :], jnp.zeros((ROW_TILE, LANES), F32)], axis=0)
    return cos, sin


def _lambda_init(layer):
    return 0.8 - 0.6 * math.exp(-0.3 * layer)


def kernel(x, c, ctx, c_ctx, w_mod, b_mod, ln_g, ln_b, w_in_ab, w_out_ab, sink_a, rpb_b, w_in_c, w_out_c,
           lam_c, subln_c, w_ffn_up, w_ffn_down, w_router, w_exp_up, w_exp_down):
    n_batch, n_lat, d = x.shape
    n_ctx = ctx.shape[1]
    assert d == D_MODEL and n_batch * n_ctx == ROW_TILE and n_lat % ROW_TILE == 0
    assert n_batch + 1 <= 8 and n_lat % C_KCHUNK == 0 and n_ctx % C_QTILE == 0
    rows = n_lat // GRID_W
    assert rows >= 12 and rows % B_QROWS == 0
    tiles_per_batch = n_lat // ROW_TILE
    lat_tiles = n_batch * tiles_per_batch
    lat_rows = n_batch * n_lat

    xs = jnp.concatenate([x.reshape(lat_rows, d), ctx.reshape(n_batch * n_ctx, d)], axis=0)
    cond = jnp.zeros((8, d), F32).at[:n_batch].set(c).at[n_batch].set(c_ctx)
    mods = _mod_vectors(cond, w_mod, b_mod).reshape(DEPTH, 8, 6, 1, d)
    cos_t, sin_t = _rope_tables(n_lat)
    w_exp_up_b = w_exp_up.astype(BF16)
    w_exp_down_b = w_exp_down.astype(BF16)

    perm = np.array([(j + (A_HEADS // 2) * half) * HEAD_DIM + dd
                     for j in range(A_HEADS // 2) for half in range(2) for dd in range(HEAD_DIM)])

    for l in range(DEPTH):
        last = l == DEPTH - 1
        i = l // 2
        all_tiles = lat_tiles + 1
        n_tiles = lat_tiles if last else all_tiles
        mods_l = mods[l]
        if l % 2 == 0:
            w_in = w_in_ab[i]
            w_in = jnp.concatenate([w_in[:, :A_Q][:, perm], w_in[:, A_Q:]], axis=1).astype(BF16)
            segs = [(A_Q, True, ATTN_SCALE * math.log2(math.e)), (A_KV, True, 1.0), (A_KV, False, 1.0),
                    (B_W, False, ATTN_SCALE * math.log2(math.e)), (B_W, False, 1.0), (B_W, False, 1.0)]
            qa, ka, va, qb, kb, vb = _in_proj(xs, mods_l, cos_t, sin_t, w_in, segs, all_tiles,
                                              tiles_per_batch, n_batch)
            sink = sink_a[i].astype(F32) * math.log2(math.e)
            oa = _attn_a(qa, ka, va.T, sink, n_batch, n_lat, n_ctx, not last)
            ob = _attn_b(qb, kb, vb.T, _nbr_bias_table(rpb_b[i], rows), n_batch, n_lat, n_ctx, not last)
            w_out = w_out_ab[i]
            o_list = [oa, ob]
            w_list = [w_out[:A_Q][perm].astype(BF16), w_out[A_Q:].astype(BF16)]
        else:
            segs = [(C_QK, True, ATTN_SCALE * math.log2(math.e)), (C_QK, True, 1.0), (C_OUT, False, 1.0)]
            qc, kc, vc = _in_proj(xs, mods_l, cos_t, sin_t, w_in_c[i].astype(BF16), segs, all_tiles,
                                  tiles_per_batch, n_batch)
            oc = _attn_c(qc, kc, vc.T, lam_c[i].astype(F32), subln_c[i].astype(F32), _lambda_init(l),
                         n_batch, n_lat, n_ctx, not last)
            o_list = [oc]
            w_list = [w_out_c[i].astype(BF16)]

        if l % 2 == 0:
            xs, h2 = _out_proj(o_list, w_list, xs, mods_l, ln_g[l, 0], ln_b[l, 0], n_tiles,
                               tiles_per_batch, n_batch)
            xs = _ffn_dense(h2, w_ffn_up[i].astype(BF16), w_ffn_down[i].astype(BF16), xs, mods_l,
                            ln_g[l, 1], ln_b[l, 1], n_tiles, tiles_per_batch, n_batch)
        else:
            wr = jnp.zeros((d, LANES), F32).at[:, :N_EXPERTS].set(w_router[i])
            wr_top = _bf16_truncate(wr)
            wr = jnp.concatenate([wr_top.astype(BF16), (wr - wr_top).astype(BF16)], axis=1)
            xs, h2, route = _out_proj(o_list, w_list, xs, mods_l, ln_g[l, 0], ln_b[l, 0], n_tiles,
                                      tiles_per_batch, n_batch, w_router=wr)
            n_rows = n_tiles * ROW_TILE
            row_token, dest, tile_expert, n_used = _routing_plan(route, n_rows)
            half_p = row_token.shape[0] // 2
            hs = jnp.concatenate([jnp.take(h2, row_token[:half_p], axis=0, mode="clip"),
                                  jnp.take(h2, row_token[half_p:], axis=0, mode="clip")], axis=0)
            ys = _ffn_experts(hs, w_exp_up_b, w_exp_down_b, i, tile_expert, n_used)
            y0 = jnp.take(ys, dest[:, 0], axis=0, mode="clip")
            y1 = jnp.take(ys, dest[:, 1], axis=0, mode="clip")
            xs = _moe_combine(y0, y1, route, xs, mods_l, ln_g[l, 1], ln_b[l, 1], n_tiles,
                              tiles_per_batch, n_batch)
    return xs[:lat_rows].reshape(n_batch, n_lat, d)
```

```python
import functools
import math

import jax
import jax.numpy as jnp
import numpy as np
from jax import lax
from jax.experimental import pallas as pl
from jax.experimental.pallas import tpu as pltpu

F32 = jnp.float32
BF16 = jnp.bfloat16

D_MODEL = 1024
DEPTH = 4
GRID_W = 64
HEAD_DIM = 64
LANES = 128
ATTN_SCALE = HEAD_DIM ** -0.5
A_BLOCK = 128
A_WINDOW = 128
A_HEADS = 8
A_KV_HEADS = 2
B_HEADS = 8
NA_KH = 8
NA_KW = 16
C_HEADS = 8
ROPE_THETA = 10000.0
ROPE_AXIS_DIM = HEAD_DIM // 2
FF_DIM = 2816
N_EXPERTS = 8
EXPERT_FF = 3584
LN_EPS = 1e-5
DN_ALPHA = (2 * DEPTH) ** 0.25
MASK_VALUE = -1e30
A_Q = A_HEADS * HEAD_DIM
A_KV = A_KV_HEADS * HEAD_DIM
B_W = B_HEADS * HEAD_DIM
C_QK = 1024
C_OUT = 1024

ROW_TILE = 512
MOE_TILE = 512
FF_CHUNK = 1408
EXPERT_FF_CHUNK = 1792
B_QROWS = 2
B_KROWS = B_QROWS + NA_KH
B_PAIRS = 4
C_QTILE = 256
C_KCHUNK = 512
C_UNROLL = 4
C_STEP_HEADS = 2
VMEM_LIMIT = 56 * 1024 * 1024


def _lane_iota(shape):
    return lax.broadcasted_iota(jnp.int32, shape, len(shape) - 1)


def _layernorm_rows(z, g, b):
    mu = jnp.mean(z, axis=-1, keepdims=True)
    zc = z - mu
    var = jnp.mean(zc * zc, axis=-1, keepdims=True)
    return zc * lax.rsqrt(var + LN_EPS) * g + b


def _mod_kernel(c_ref, w_ref, b_ref, o_ref):
    c = c_ref[...]
    s = c * (1.0 / (1.0 + jnp.exp(-c)))
    o_ref[0] = jnp.dot(s, w_ref[0], preferred_element_type=F32,
                       precision=lax.Precision.HIGHEST) + b_ref[0]


def _mod_vectors(cond, w_mod, b_mod):
    depth, d, n6 = w_mod.shape
    tn = 1536
    return pl.pallas_call(
        _mod_kernel,
        grid=(depth, n6 // tn),
        in_specs=[
            pl.BlockSpec((8, d), lambda l, j: (0, 0)),
            pl.BlockSpec((1, d, tn), lambda l, j: (l, 0, j)),
            pl.BlockSpec((1, 1, tn), lambda l, j: (l, 0, j)),
        ],
        out_specs=pl.BlockSpec((1, 8, tn), lambda l, j: (l, 0, j)),
        out_shape=jax.ShapeDtypeStruct((depth, 8, n6), F32),
        compiler_params=pltpu.CompilerParams(vmem_limit_bytes=VMEM_LIMIT),
    )(cond, w_mod, b_mod.reshape(depth, 1, n6))


def _rope_slab(a, cos, sin_signed, low16):
    fwd = pltpu.roll(a, LANES - 16, axis=1)
    bwd = pltpu.roll(a, 16, axis=1)
    return a * cos + jnp.where(low16, fwd, bwd) * sin_signed


def _proj_kernel(x_ref, sc_ref, sh_ref, cos_ref, sin_ref, w_ref, *out_refs, segments):
    x = x_ref[...]
    h = (x * (1.0 + sc_ref[...]) + sh_ref[...]).astype(BF16)
    cos = cos_ref[...]
    sin = sin_ref[...]
    low16 = (_lane_iota(cos.shape) % 32) < 16
    col = 0
    for o_ref, (width, rope, scale) in zip(out_refs, segments):
        for c0 in range(0, width, 256):
            cw = min(256, width - c0)
            acc = jnp.dot(h, w_ref[:, col + c0:col + c0 + cw], preferred_element_type=F32)
            slabs = []
            for s0 in range(0, cw, LANES):
                a = acc[:, s0:s0 + LANES]
                if rope:
                    a = _rope_slab(a, cos, sin, low16)
                if scale != 1.0:
                    a = a * scale
                slabs.append(a.astype(BF16))
            o_ref[:, c0:c0 + cw] = slabs[0] if len(slabs) == 1 else jnp.concatenate(slabs, axis=1)
        col += width


def _in_proj(xs, mods_l, cos_t, sin_t, w, segments, n_tiles, tiles_per_batch, n_batch):
    t_rows, d = xs.shape
    tm = ROW_TILE
    grp = lambda t: jnp.minimum(t // tiles_per_batch, n_batch)
    pos = lambda t: jnp.where(t < n_batch * tiles_per_batch, t % tiles_per_batch, tiles_per_batch)
    out_shape = [jax.ShapeDtypeStruct((t_rows, s[0]), BF16) for s in segments]
    out_specs = [pl.BlockSpec((tm, s[0]), lambda t: (t, 0)) for s in segments]
    return pl.pallas_call(
        functools.partial(_proj_kernel, segments=tuple(segments)),
        grid=(n_tiles,),
        in_specs=[
            pl.BlockSpec((tm, d), lambda t: (t, 0)),
            pl.BlockSpec((None, None, 1, d), lambda t: (grp(t), 1, 0, 0)),
            pl.BlockSpec((None, None, 1, d), lambda t: (grp(t), 0, 0, 0)),
            pl.BlockSpec((tm, LANES), lambda t: (pos(t), 0)),
            pl.BlockSpec((tm, LANES), lambda t: (pos(t), 0)),
            pl.BlockSpec(w.shape, lambda t: (0, 0)),
        ],
        out_specs=out_specs,
        out_shape=out_shape,
        compiler_params=pltpu.CompilerParams(vmem_limit_bytes=VMEM_LIMIT),
    )(xs, mods_l, mods_l, cos_t, sin_t, w)


def _attn_a_kernel(sink_ref, q_ref, k_ref, vt_ref, kc_ref, vct_ref, o_ref, *, n_blocks, n_lat):
    n = pl.program_id(1)
    blk = A_BLOCK
    win = 3 * blk
    start = pl.multiple_of(jnp.clip((n - 1) * blk, 0, n_lat - win), blk)
    keys = jnp.concatenate([k_ref[pl.ds(start, win), :], kc_ref[...]], axis=0)
    vals_t = jnp.concatenate([vt_ref[:, pl.ds(start, win)], vct_ref[...]], axis=1)
    nk = keys.shape[0]
    qbase = jnp.where(n < n_blocks, n * blk, -(1 << 20))
    kidx = lax.broadcasted_iota(jnp.int32, (nk, 2 * blk), 0)
    qidx = lax.broadcasted_iota(jnp.int32, (nk, 2 * blk), 1)
    qpos = qbase + jnp.where(qidx >= blk, qidx - blk, qidx)
    in_band = jnp.abs(start + kidx - qpos) <= A_WINDOW
    bias = jnp.where((kidx >= win) | in_band, 0.0, MASK_VALUE).astype(F32)
    lo = _lane_iota((blk, LANES)) < HEAD_DIM
    zero = jnp.zeros((blk, LANES), BF16)
    col_hi = _lane_iota((1, 2 * blk)) >= blk
    dn = (((1,), (1,)), ((), ()))
    scores = []
    for j in range(A_HEADS // 2):
        qc = q_ref[:, j * LANES:(j + 1) * LANES]
        qq = jnp.concatenate([jnp.where(lo, qc, zero), jnp.where(lo, zero, qc)], axis=0)
        scores.append(lax.dot_general(keys, qq, dn, preferred_element_type=F32) + bias)
    for j, s in enumerate(scores):
        sink = jnp.where(col_hi, sink_ref[j + A_HEADS // 2], sink_ref[j])
        m = jnp.maximum(jnp.max(s, axis=0, keepdims=True), sink)
        p = jnp.exp2(s - m)
        l = jnp.sum(p, axis=0, keepdims=True) + jnp.exp2(sink - m)
        ot = jnp.dot(vals_t, p.astype(BF16), preferred_element_type=F32) * (1.0 / l)
        o = jnp.concatenate([ot[:HEAD_DIM, :blk], ot[HEAD_DIM:, blk:]], axis=0)
        o_ref[:, j * LANES:(j + 1) * LANES] = o.T.astype(BF16)


def _attn_a(q, k, vt, sink, n_batch, n_lat, n_ctx, with_ctx):
    t_rows = q.shape[0]
    blk = A_BLOCK
    nb = n_lat // blk
    ncb = n_ctx // blk if with_ctx else 0
    qrow = lambda b, n: jnp.where(n < nb, b * nb + n, n_batch * nb + b * (n_ctx // blk) + (n - nb))
    ctx_blk = lambda b: (n_batch * n_lat) // n_ctx + b
    grid_spec = pltpu.PrefetchScalarGridSpec(
        num_scalar_prefetch=1,
        grid=(n_batch, nb + ncb),
        in_specs=[
            pl.BlockSpec((blk, A_Q), lambda b, n, s: (qrow(b, n), 0)),
            pl.BlockSpec((n_lat, A_KV), lambda b, n, s: (b, 0)),
            pl.BlockSpec((A_KV, n_lat), lambda b, n, s: (0, b)),
            pl.BlockSpec((n_ctx, A_KV), lambda b, n, s: (ctx_blk(b), 0)),
            pl.BlockSpec((A_KV, n_ctx), lambda b, n, s: (0, ctx_blk(b))),
        ],
        out_specs=pl.BlockSpec((blk, A_Q), lambda b, n, s: (qrow(b, n), 0)),
    )
    return pl.pallas_call(
        functools.partial(_attn_a_kernel, n_blocks=nb, n_lat=n_lat),
        grid_spec=grid_spec,
        out_shape=jax.ShapeDtypeStruct((t_rows if with_ctx else n_batch * n_lat, A_Q), BF16),
        compiler_params=pltpu.CompilerParams(vmem_limit_bytes=VMEM_LIMIT),
    )(sink, q, k, vt, k, vt)


def _attn_b_kernel(q_ref, k_ref, vt_ref, kc_ref, vct_ref, bias_ref, o_ref, *, rows, n_steps):
    r = pl.program_id(2)
    nq = B_QROWS * GRID_W
    nk = B_KROWS * GRID_W
    r0 = jnp.where(r < n_steps, r * B_QROWS, 0)
    ws = jnp.clip(r0 - NA_KH // 2, 0, rows - B_KROWS)
    start = pl.multiple_of(ws * GRID_W, LANES)
    lo = _lane_iota((nq, LANES)) < HEAD_DIM
    zero = jnp.zeros((nq, LANES), BF16)
    dn = (((1,), (1,)), ((), ()))
    scores = []
    for g in range(B_PAIRS):
        cols = slice(g * LANES, (g + 1) * LANES)
        q = q_ref[:, cols]
        qq = jnp.concatenate([jnp.where(lo, q, zero), jnp.where(lo, zero, q)], axis=0)
        s_loc = lax.dot_general(k_ref[pl.ds(start, nk), cols], qq, dn,
                                preferred_element_type=F32) + bias_ref[g]
        s_ctx = lax.dot_general(kc_ref[:, cols], qq, dn, preferred_element_type=F32)
        scores.append((s_loc, s_ctx))
    for g, (s_loc, s_ctx) in enumerate(scores):
        cols = slice(g * LANES, (g + 1) * LANES)
        m = jnp.maximum(jnp.max(s_loc, axis=0, keepdims=True), jnp.max(s_ctx, axis=0, keepdims=True))
        p_loc = jnp.exp2(s_loc - m)
        p_ctx = jnp.exp2(s_ctx - m)
        l = jnp.sum(p_loc, axis=0, keepdims=True) + jnp.sum(p_ctx, axis=0, keepdims=True)
        ot = (jnp.dot(vt_ref[cols, pl.ds(start, nk)], p_loc.astype(BF16), preferred_element_type=F32)
              + jnp.dot(vct_ref[cols, :], p_ctx.astype(BF16), preferred_element_type=F32)) * (1.0 / l)
        o = jnp.concatenate([ot[:HEAD_DIM, :nq], ot[HEAD_DIM:, nq:]], axis=0)
        o_ref[:, cols] = o.T.astype(BF16)


def _nbr_bias_table(rpb, rows):
    n_steps = rows // B_QROWS
    steps = [min(2, n_steps - 1), 0, 1, n_steps - 2, n_steps - 1]
    w = GRID_W
    n_heads = rpb.shape[0]
    c = np.arange(w)[None, :]
    kc = np.arange(w)[:, None]
    cs = np.clip(c - NA_KW // 2, 0, w - NA_KW)
    col_ok = (kc >= cs) & (kc < cs + NA_KW)
    onehot = (((kc - c + NA_KW - 1)[None] == np.arange(2 * NA_KW - 1)[:, None, None]) & col_ok[None])
    toe = jnp.einsum("hrd,dkc->hrkc", rpb.astype(F32), jnp.asarray(onehot, F32),
                     precision=lax.Precision.HIGHEST)
    toe = jnp.where(col_ok[None, None], toe * math.log2(math.e), MASK_VALUE)
    masked = jnp.full((n_heads, w, w), MASK_VALUE, F32)
    tabs = []
    for st in steps:
        r0 = st * B_QROWS
        ws = int(np.clip(r0 - NA_KH // 2, 0, rows - B_KROWS))
        q_cols = []
        for rq in range(B_QROWS):
            r = r0 + rq
            rs = int(np.clip(r - NA_KH // 2, 0, rows - NA_KH))
            blocks = [toe[:, ws + ki - r + NA_KH - 1] if rs <= ws + ki < rs + NA_KH else masked
                      for ki in range(B_KROWS)]
            q_cols.append(jnp.concatenate(blocks, axis=1))
        tabs.append(jnp.concatenate(q_cols, axis=2))
    tabs.append(jnp.full_like(tabs[0], MASK_VALUE))
    tab = jnp.stack(tabs)
    n_var, _, nk, nq = tab.shape
    tab = tab.reshape(n_var, n_heads // 2, 2, nk, nq)
    return jnp.transpose(tab, (0, 1, 3, 2, 4)).reshape(n_var, n_heads // 2, nk, 2 * nq)


def _attn_b(q, k, vt, bias_tab, n_batch, n_lat, n_ctx, with_ctx):
    t_rows = q.shape[0]
    rows = n_lat // GRID_W
    nq = B_QROWS * GRID_W
    nk = B_KROWS * GRID_W
    gw = B_PAIRS * LANES
    n_steps = rows // B_QROWS
    n_cstep = n_ctx // nq if with_ctx else 0
    qrow = lambda b, r: jnp.where(r < n_steps, b * n_steps + r,
                                  n_batch * n_steps + b * (n_ctx // nq) + (r - n_steps))
    ctx_blk = lambda b: (n_batch * n_lat) // n_ctx + b

    def variant(r):
        v = jnp.where(r == 0, 1, 0)
        v = jnp.where(r == 1, 2, v)
        v = jnp.where(r == n_steps - 2, 3, v)
        v = jnp.where(r == n_steps - 1, 4, v)
        return jnp.where(r >= n_steps, 5, v)

    return pl.pallas_call(
        functools.partial(_attn_b_kernel, rows=rows, n_steps=n_steps),
        grid=(n_batch, B_HEADS // (2 * B_PAIRS), n_steps + n_cstep),
        in_specs=[
            pl.BlockSpec((nq, gw), lambda b, j, r: (qrow(b, r), j)),
            pl.BlockSpec((n_lat, gw), lambda b, j, r: (b, j)),
            pl.BlockSpec((gw, n_lat), lambda b, j, r: (j, b)),
            pl.BlockSpec((n_ctx, gw), lambda b, j, r: (ctx_blk(b), j)),
            pl.BlockSpec((gw, n_ctx), lambda b, j, r: (j, ctx_blk(b))),
            pl.BlockSpec((None, B_PAIRS, nk, 2 * nq), lambda b, j, r: (variant(r), j, 0, 0)),
        ],
        out_specs=pl.BlockSpec((nq, gw), lambda b, j, r: (qrow(b, r), j)),
        out_shape=jax.ShapeDtypeStruct((t_rows if with_ctx else n_batch * n_lat, B_W), BF16),
        compiler_params=pltpu.CompilerParams(vmem_limit_bytes=VMEM_LIMIT),
    )(q, k, vt, k, vt, bias_tab)


def _attn_c_kernel(lam_ref, q_ref, k_ref, vt_ref, kc_ref, vct_ref, g_ref, o_ref,
                   s_a, s_b, s_c, m_scr, l_scr, acc_scr, *, n_qlat, n_kchunks, lam_init):
    i = pl.program_id(2)
    tq = q_ref.shape[0]
    tk = C_KCHUNK
    lo = _lane_iota((tq, LANES)) < HEAD_DIM
    zero = jnp.zeros((tq, LANES), BF16)
    q_maps = []
    for hh in range(C_STEP_HEADS):
        q = q_ref[:, hh * LANES:(hh + 1) * LANES]
        q_maps += [jnp.where(lo, q, zero), jnp.where(lo, zero, q)]
    n_chains = len(q_maps)
    dn = (((1,), (1,)), ((), ()))

    def head_cols(ci):
        return slice((ci // 2) * LANES, (ci // 2 + 1) * LANES)

    def scores(kk_of, s_ref):
        for ci in range(n_chains):
            s_ref[ci] = lax.dot_general(kk_of(head_cols(ci)), q_maps[ci], dn,
                                        preferred_element_type=F32)

    def softmax_pv(s_ref, vvt_of):
        for ci in range(n_chains):
            s = s_ref[ci]
            m = m_scr[ci]
            m_new = jnp.maximum(m, jnp.max(s, axis=0, keepdims=True))
            alpha = jnp.exp2(m - m_new)
            p = jnp.exp2(s - m_new)
            l_scr[ci] = alpha * l_scr[ci] + jnp.sum(p, axis=0, keepdims=True)
            acc_scr[ci] = alpha * acc_scr[ci] + jnp.dot(vvt_of(head_cols(ci)), p.astype(BF16),
                                                        preferred_element_type=F32)
            m_scr[ci] = m_new

    def chunk_slice(c):
        return pl.ds(c * tk if isinstance(c, int) else pl.multiple_of(c * tk, tk), tk)

    def k_chunk(c):
        return lambda cols: k_ref[chunk_slice(c), cols]

    def vt_chunk(c):
        return lambda cols: vt_ref[cols, chunk_slice(c)]

    k_ctx = lambda cols: kc_ref[:, cols]
    vt_ctx = lambda cols: vct_ref[cols, :]

    m_scr[...] = jnp.full(m_scr.shape, -jnp.inf, F32)
    l_scr[...] = jnp.zeros(l_scr.shape, F32)
    acc_scr[...] = jnp.zeros(acc_scr.shape, F32)

    def finish():
        softmax_pv(s_c, vt_ctx)
        lp = lam_ref[...]
        lam = (jnp.exp(jnp.sum(lp[0:1] * lp[1:2], axis=-1, keepdims=True))
               - jnp.exp(jnp.sum(lp[2:3] * lp[3:4], axis=-1, keepdims=True)) + lam_init)
        for hh in range(C_STEP_HEADS):
            c1, c2 = 2 * hh, 2 * hh + 1
            od = acc_scr[c1] * (1.0 / l_scr[c1]) - lam * (acc_scr[c2] * (1.0 / l_scr[c2]))
            ms = jnp.mean(od * od, axis=0, keepdims=True)
            on = (od * lax.rsqrt(ms + LN_EPS)).T
            o_ref[:, hh * LANES:(hh + 1) * LANES] = ((on * g_ref[...]) * (1.0 - lam_init)).astype(BF16)

    @pl.when(i < n_qlat)
    def _():
        slots = (s_a, s_b)
        scores(k_chunk(0), s_a)

        def body(it, carry):
            c0 = C_UNROLL * it
            for u in range(C_UNROLL):
                scores(k_chunk(c0 + u + 1), slots[(u + 1) % 2])
                softmax_pv(slots[u % 2], vt_chunk(c0 + u))
            return carry

        n_it = (n_kchunks - 1) // C_UNROLL
        lax.fori_loop(0, n_it, body, 0)
        for c in range(C_UNROLL * n_it, n_kchunks):
            if c + 1 < n_kchunks:
                scores(k_chunk(c + 1), slots[(c + 1) % 2])
            else:
                scores(k_ctx, s_c)
            softmax_pv(slots[c % 2], vt_chunk(c))
        finish()

    @pl.when(i >= n_qlat)
    def _():
        scores(k_ctx, s_c)
        finish()


def _attn_c(q, k, vt, lam_p, subln, lam_init, n_batch, n_lat, n_ctx, with_ctx):
    t_rows = q.shape[0]
    tq = C_QTILE
    hw = C_STEP_HEADS * LANES
    nch = 2 * C_STEP_HEADS
    n_qlat = n_lat // tq
    n_qctx = n_ctx // tq if with_ctx else 0
    qrow = lambda b, i: jnp.where(i < n_qlat, b * n_qlat + i,
                                  n_batch * n_qlat + b * (n_ctx // tq) + (i - n_qlat))
    ctx_blk = lambda b: (n_batch * n_lat) // n_ctx + b
    return pl.pallas_call(
        functools.partial(_attn_c_kernel, n_qlat=n_qlat, n_kchunks=n_lat // C_KCHUNK,
                          lam_init=lam_init),
        grid=(n_batch, C_HEADS // C_STEP_HEADS, n_qlat + n_qctx),
        in_specs=[
            pl.BlockSpec((4, HEAD_DIM), lambda b, h, i: (0, 0)),
            pl.BlockSpec((tq, hw), lambda b, h, i: (qrow(b, i), h)),
            pl.BlockSpec((n_lat, hw), lambda b, h, i: (b, h)),
            pl.BlockSpec((hw, n_lat), lambda b, h, i: (h, b)),
            pl.BlockSpec((n_ctx, hw), lambda b, h, i: (ctx_blk(b), h)),
            pl.BlockSpec((hw, n_ctx), lambda b, h, i: (h, ctx_blk(b))),
            pl.BlockSpec((1, LANES), lambda b, h, i: (0, 0)),
        ],
        out_specs=pl.BlockSpec((tq, hw), lambda b, h, i: (qrow(b, i), h)),
        out_shape=jax.ShapeDtypeStruct((t_rows if with_ctx else n_batch * n_lat, C_OUT), BF16),
        scratch_shapes=[
            pltpu.VMEM((nch, C_KCHUNK, tq), F32), pltpu.VMEM((nch, C_KCHUNK, tq), F32),
            pltpu.VMEM((nch, n_ctx, tq), F32),
            pltpu.VMEM((nch, 1, tq), F32), pltpu.VMEM((nch, 1, tq), F32), pltpu.VMEM((nch, LANES, tq), F32),
        ],
        compiler_params=pltpu.CompilerParams(vmem_limit_bytes=VMEM_LIMIT),
    )(lam_p, q, k, vt, k, vt, subln.reshape(1, LANES))


def _bf16_truncate(x):
    bits = lax.bitcast_convert_type(x, jnp.uint32) & jnp.uint32(0xFFFF0000)
    return lax.bitcast_convert_type(bits, F32)


def _top2_router(h, wr_ref):
    h_top = _bf16_truncate(h)
    h_hi = h_top.astype(BF16)
    h_lo = (h - h_top).astype(BF16)
    a = jnp.dot(h_hi, wr_ref[...], preferred_element_type=F32)
    b = jnp.dot(h_lo, wr_ref[:, :LANES], preferred_element_type=F32)
    lg = a[:, :LANES] + a[:, LANES:] + b
    lane = _lane_iota(lg.shape)
    lanef = lane.astype(F32)
    lg = jnp.where(lane < N_EXPERTS, lg, -jnp.inf)
    m1 = jnp.max(lg, axis=-1, keepdims=True)
    i1 = jnp.min(jnp.where(lg == m1, lanef, float(LANES)), axis=-1, keepdims=True)
    lg2 = jnp.where(lanef == i1, -jnp.inf, lg)
    m2 = jnp.max(lg2, axis=-1, keepdims=True)
    i2 = jnp.min(jnp.where(lg2 == m2, lanef, float(LANES)), axis=-1, keepdims=True)
    e = jnp.exp(m2 - m1)
    w1 = 1.0 / (1.0 + e)
    w2 = e / (1.0 + e)
    out = jnp.where(lane == 0, i1, 0.0)
    out = jnp.where(lane == 1, i2, out)
    out = jnp.where(lane == 2, w1, out)
    return jnp.where(lane == 3, w2, out)


def _pack_bf16_pairs(h):
    bits = lax.bitcast_convert_type(h, jnp.uint32)
    bits = bits + (jnp.uint32(0x7FFF) + ((bits >> 16) & jnp.uint32(1)))
    half = h.shape[1] // 2
    packed = (bits[:, half:] & jnp.uint32(0xFFFF0000)) | (bits[:, :half] >> 16)
    return lax.bitcast_convert_type(packed, F32)


def _unpack_bf16_pairs(words):
    p = lax.bitcast_convert_type(words, jnp.uint32)
    lo = lax.bitcast_convert_type(p << 16, F32)
    hi = lax.bitcast_convert_type(p & jnp.uint32(0xFFFF0000), F32)
    return jnp.concatenate([lo, hi], axis=1).astype(BF16)


def _outproj_kernel(*refs, n_in, router):
    o_refs = refs[:n_in]
    w_refs = refs[n_in:2 * n_in]
    x_ref, g_ref, lng_ref, lnb_ref, sc_ref, sh_ref = refs[2 * n_in:2 * n_in + 6]
    rest = refs[2 * n_in + 6:]
    y = jnp.dot(o_refs[0][...], w_refs[0][...], preferred_element_type=F32)
    for o_r, w_r in zip(o_refs[1:], w_refs[1:]):
        y = y + jnp.dot(o_r[...], w_r[...], preferred_element_type=F32)
    xn = _layernorm_rows(DN_ALPHA * x_ref[...] + g_ref[...] * y, lng_ref[...], lnb_ref[...])
    h2 = xn * (1.0 + sc_ref[...]) + sh_ref[...]
    if router:
        wr_ref, xo_ref, h_ref, r_ref = rest
        r_ref[...] = _top2_router(h2, wr_ref)
        h_ref[...] = _pack_bf16_pairs(h2)
    else:
        xo_ref, h_ref = rest
        h_ref[...] = h2.astype(BF16)
    xo_ref[...] = xn


def _out_proj(o_list, w_list, xs, mods_l, ln_g, ln_b, n_tiles, tiles_per_batch, n_batch, w_router=None):
    d = xs.shape[1]
    tm = ROW_TILE
    t_rows = n_tiles * tm
    grp = lambda t: jnp.minimum(t // tiles_per_batch, n_batch)
    mod = lambda k: pl.BlockSpec((None, None, 1, d), lambda t: (grp(t), k, 0, 0))
    in_specs = [pl.BlockSpec((tm, o.shape[1]), lambda t: (t, 0)) for o in o_list]
    in_specs += [pl.BlockSpec(w.shape, lambda t: (0, 0)) for w in w_list]
    in_specs += [pl.BlockSpec((tm, d), lambda t: (t, 0)), mod(2),
                 pl.BlockSpec((1, d), lambda t: (0, 0)), pl.BlockSpec((1, d), lambda t: (0, 0)),
                 mod(4), mod(3)]
    args = list(o_list) + list(w_list) + [xs, mods_l, ln_g.reshape(1, d), ln_b.reshape(1, d), mods_l, mods_l]
    out_shape = [jax.ShapeDtypeStruct((t_rows, d), F32), jax.ShapeDtypeStruct((t_rows, d), BF16)]
    out_specs = [pl.BlockSpec((tm, d), lambda t: (t, 0)), pl.BlockSpec((tm, d), lambda t: (t, 0))]
    if w_router is not None:
        out_shape[1] = jax.ShapeDtypeStruct((t_rows, d // 2), F32)
        out_specs[1] = pl.BlockSpec((tm, d // 2), lambda t: (t, 0))
        in_specs.append(pl.BlockSpec(w_router.shape, lambda t: (0, 0)))
        args.append(w_router)
        out_shape.append(jax.ShapeDtypeStruct((t_rows, LANES), F32))
        out_specs.append(pl.BlockSpec((tm, LANES), lambda t: (t, 0)))
    return pl.pallas_call(
        functools.partial(_outproj_kernel, n_in=len(o_list), router=w_router is not None),
        grid=(n_tiles,),
        in_specs=in_specs,
        out_specs=out_specs,
        out_shape=out_shape,
        compiler_params=pltpu.CompilerParams(vmem_limit_bytes=VMEM_LIMIT),
    )(*args)


def _swiglu_acc(h, wg_ref, wu_ref, wd_ref, acc_ref, c):
    g = jnp.dot(h, wg_ref[...], preferred_element_type=F32)
    u = jnp.dot(h, wu_ref[...], preferred_element_type=F32)
    a = (g * (1.0 / (1.0 + jnp.exp(-g))) * u).astype(BF16)
    part = jnp.dot(a, wd_ref[...], preferred_element_type=F32)

    @pl.when(c == 0)
    def _():
        acc_ref[...] = part

    @pl.when(c > 0)
    def _():
        acc_ref[...] += part


def _ffn_dense_kernel(h_ref, wg_ref, wu_ref, wd_ref, x_ref, g_ref, lng_ref, lnb_ref, xo_ref, acc_ref):
    c = pl.program_id(1)
    _swiglu_acc(h_ref[...], wg_ref, wu_ref, wd_ref, acc_ref, c)

    @pl.when(c == pl.num_programs(1) - 1)
    def _():
        z = DN_ALPHA * x_ref[...] + g_ref[...] * acc_ref[...]
        xo_ref[...] = _layernorm_rows(z, lng_ref[...], lnb_ref[...])


def _ffn_dense(h2, w_up, w_down, xs, mods_l, ln_g, ln_b, n_tiles, tiles_per_batch, n_batch):
    d = xs.shape[1]
    tm = ROW_TILE
    t_rows = n_tiles * tm
    ff = w_down.shape[0]
    nc = ff // FF_CHUNK
    grp = lambda t: jnp.minimum(t // tiles_per_batch, n_batch)
    return pl.pallas_call(
        _ffn_dense_kernel,
        grid=(n_tiles, nc),
        in_specs=[
            pl.BlockSpec((tm, d), lambda t, c: (t, 0)),
            pl.BlockSpec((d, FF_CHUNK), lambda t, c: (0, c)),
            pl.BlockSpec((d, FF_CHUNK), lambda t, c: (0, nc + c)),
            pl.BlockSpec((FF_CHUNK, d), lambda t, c: (c, 0)),
            pl.BlockSpec((tm, d), lambda t, c: (t, 0)),
            pl.BlockSpec((None, None, 1, d), lambda t, c: (grp(t), 5, 0, 0)),
            pl.BlockSpec((1, d), lambda t, c: (0, 0)),
            pl.BlockSpec((1, d), lambda t, c: (0, 0)),
        ],
        out_specs=pl.BlockSpec((tm, d), lambda t, c: (t, 0)),
        out_shape=jax.ShapeDtypeStruct((t_rows, d), F32),
        scratch_shapes=[pltpu.VMEM((tm, d), F32)],
        compiler_params=pltpu.CompilerParams(vmem_limit_bytes=VMEM_LIMIT),
    )(h2, w_up, w_up, w_down, xs, mods_l, ln_g.reshape(1, d), ln_b.reshape(1, d))


def _ffn_expert_kernel(te_ref, nu_ref, h_ref, wg_ref, wu_ref, wd_ref, y_ref, acc_ref):
    i = pl.program_id(0)
    c = pl.program_id(1)

    @pl.when(i < nu_ref[0])
    def _():
        _swiglu_acc(_unpack_bf16_pairs(h_ref[...]), wg_ref, wu_ref, wd_ref, acc_ref, c)

    @pl.when(c == pl.num_programs(1) - 1)
    def _():
        y_ref[...] = acc_ref[...]


def _ffn_experts(hs, w_up, w_down, layer, tile_expert, n_used):
    p_rows = hs.shape[0]
    d = w_down.shape[-1]
    tm = MOE_TILE
    n_tiles = p_rows // tm
    ef = w_down.shape[2]
    nc = ef // EXPERT_FF_CHUNK
    chunk = lambda i, c, nu: jnp.where(i < nu[0], c, nc - 1)
    grid_spec = pltpu.PrefetchScalarGridSpec(
        num_scalar_prefetch=2,
        grid=(n_tiles, nc),
        in_specs=[
            pl.BlockSpec((tm, d // 2), lambda i, c, te, nu: (i, 0)),
            pl.BlockSpec((None, None, d, EXPERT_FF_CHUNK),
                         lambda i, c, te, nu: (layer, te[i], 0, chunk(i, c, nu))),
            pl.BlockSpec((None, None, d, EXPERT_FF_CHUNK),
                         lambda i, c, te, nu: (layer, te[i], 0, nc + chunk(i, c, nu))),
            pl.BlockSpec((None, None, EXPERT_FF_CHUNK, d),
                         lambda i, c, te, nu: (layer, te[i], chunk(i, c, nu), 0)),
        ],
        out_specs=pl.BlockSpec((tm, d), lambda i, c, te, nu: (i, 0)),
        scratch_shapes=[pltpu.VMEM((tm, d), F32)],
    )
    return pl.pallas_call(
        _ffn_expert_kernel,
        grid_spec=grid_spec,
        out_shape=jax.ShapeDtypeStruct((p_rows, d), F32),
        compiler_params=pltpu.CompilerParams(vmem_limit_bytes=VMEM_LIMIT),
    )(tile_expert, n_used, hs, w_up, w_up, w_down)


def _combine_kernel(y0_ref, y1_ref, r_ref, x_ref, g_ref, lng_ref, lnb_ref, xo_ref):
    r = r_ref[...]
    f = r[:, 2:3] * y0_ref[...] + r[:, 3:4] * y1_ref[...]
    z = DN_ALPHA * x_ref[...] + g_ref[...] * f
    xo_ref[...] = _layernorm_rows(z, lng_ref[...], lnb_ref[...])


def _moe_combine(y0, y1, route, xs, mods_l, ln_g, ln_b, n_tiles, tiles_per_batch, n_batch):
    d = xs.shape[1]
    tm = ROW_TILE
    t_rows = n_tiles * tm
    grp = lambda t: jnp.minimum(t // tiles_per_batch, n_batch)
    row = pl.BlockSpec((tm, d), lambda t: (t, 0))
    return pl.pallas_call(
        _combine_kernel,
        grid=(n_tiles,),
        in_specs=[row, row, pl.BlockSpec((tm, LANES), lambda t: (t, 0)), row,
                  pl.BlockSpec((None, None, 1, d), lambda t: (grp(t), 5, 0, 0)),
                  pl.BlockSpec((1, d), lambda t: (0, 0)), pl.BlockSpec((1, d), lambda t: (0, 0))],
        out_specs=row,
        out_shape=jax.ShapeDtypeStruct((t_rows, d), F32),
        compiler_params=pltpu.CompilerParams(vmem_limit_bytes=VMEM_LIMIT),
    )(y0, y1, route, xs, mods_l, ln_g.reshape(1, d), ln_b.reshape(1, d))


def _routing_plan(route, n_rows):
    tm = MOE_TILE
    e_idx = route[:n_rows, 0:2].astype(jnp.int32).reshape(-1)
    onehot = (e_idx[:, None] == jnp.arange(N_EXPERTS, dtype=jnp.int32)[None, :]).astype(jnp.int32)
    csum = jnp.cumsum(onehot, axis=0)
    counts = csum[-1]
    rank = jnp.sum((csum - onehot) * onehot, axis=1)
    padded = ((counts + tm - 1) // tm) * tm
    ends = jnp.cumsum(padded)
    starts = ends - padded
    dest = starts[e_idx] + rank
    n_tiles = (2 * n_rows + N_EXPERTS * (tm - 1)) // tm
    p_rows = n_tiles * tm
    row_token = jnp.zeros((p_rows,), jnp.int32).at[dest].set(jnp.arange(2 * n_rows, dtype=jnp.int32) // 2)
    tile_start = jnp.arange(n_tiles, dtype=jnp.int32) * tm
    tile_expert = jnp.minimum(jnp.sum((tile_start[:, None] >= ends[None, :]).astype(jnp.int32), axis=1),
                              N_EXPERTS - 1)
    n_used = (ends[-1] // tm).astype(jnp.int32).reshape(1)
    last = tile_expert[jnp.maximum(n_used[0] - 1, 0)]
    tile_expert = jnp.where(jnp.arange(n_tiles) < n_used[0], tile_expert, last).astype(jnp.int32)
    return row_token, dest.reshape(n_rows, 2), tile_expert, n_used


def _rope_tables(n_lat):
    t = jnp.arange(n_lat, dtype=jnp.int32)
    row = (t // GRID_W).astype(F32)
    col = (t % GRID_W).astype(F32)
    inv = ROPE_THETA ** (-jnp.arange(0, ROPE_AXIS_DIM, 2, dtype=F32) / ROPE_AXIS_DIM)
    ar = row[:, None] * inv[None, :]
    ac = col[:, None] * inv[None, :]
    ang = jnp.concatenate([ar, ar, ac, ac], axis=-1)
    cos = jnp.tile(jnp.cos(ang), (1, LANES // HEAD_DIM))
    sin = jnp.tile(jnp.sin(ang), (1, LANES // HEAD_DIM))
    sign = jnp.where((jnp.arange(LANES) % 32) < 16, -1.0, 1.0).astype(F32)
    cos = jnp.concatenate([cos, jnp.ones((ROW_TILE, LANES), F32)], axis=0)
    sin = jnp.concatenate([sin * sign[None, :], jnp.zeros((ROW_TILE, LANES), F32)], axis=0)
    return cos, sin


def _lambda_init(layer):
    return 0.8 - 0.6 * math.exp(-0.3 * layer)


def kernel(x, c, ctx, c_ctx, w_mod, b_mod, ln_g, ln_b, w_in_ab, w_out_ab, sink_a, rpb_b, w_in_c, w_out_c,
           lam_c, subln_c, w_ffn_up, w_ffn_down, w_router, w_exp_up, w_exp_down):
    n_batch, n_lat, d = x.shape
    n_ctx = ctx.shape[1]
    assert d == D_MODEL and n_batch * n_ctx == ROW_TILE and n_lat % ROW_TILE == 0
    assert n_batch + 1 <= 8 and n_lat % C_KCHUNK == 0 and n_ctx % C_QTILE == 0
    rows = n_lat // GRID_W
    assert rows >= 12 and rows % B_QROWS == 0
    tiles_per_batch = n_lat // ROW_TILE
    lat_tiles = n_batch * tiles_per_batch
    lat_rows = n_batch * n_lat

    xs = jnp.concatenate([x.reshape(lat_rows, d), ctx.reshape(n_batch * n_ctx, d)], axis=0)
    cond = jnp.zeros((8, d), F32).at[:n_batch].set(c).at[n_batch].set(c_ctx)
    mods = _mod_vectors(cond, w_mod, b_mod).reshape(DEPTH, 8, 6, 1, d)
    cos_t, sin_t = _rope_tables(n_lat)
    w_exp_up_b = w_exp_up.astype(BF16)
    w_exp_down_b = w_exp_down.astype(BF16)

    perm = np.array([(j + (A_HEADS // 2) * half) * HEAD_DIM + dd
                     for j in range(A_HEADS // 2) for half in range(2) for dd in range(HEAD_DIM)])

    for l in range(DEPTH):
        last = l == DEPTH - 1
        i = l // 2
        all_tiles = lat_tiles + 1
        n_tiles = lat_tiles if last else all_tiles
        mods_l = mods[l]
        if l % 2 == 0:
            w_in = w_in_ab[i]
            w_in = jnp.concatenate([w_in[:, :A_Q][:, perm], w_in[:, A_Q:]], axis=1).astype(BF16)
            segs = [(A_Q, True, ATTN_SCALE * math.log2(math.e)), (A_KV, True, 1.0), (A_KV, False, 1.0),
                    (B_W, False, ATTN_SCALE * math.log2(math.e)), (B_W, False, 1.0), (B_W, False, 1.0)]
            qa, ka, va, qb, kb, vb = _in_proj(xs, mods_l, cos_t, sin_t, w_in, segs, all_tiles,
                                              tiles_per_batch, n_batch)
            sink = sink_a[i].astype(F32) * math.log2(math.e)
            oa = _attn_a(qa, ka, va.T, sink, n_batch, n_lat, n_ctx, not last)
            ob = _attn_b(qb, kb, vb.T, _nbr_bias_table(rpb_b[i], rows), n_batch, n_lat, n_ctx, not last)
            w_out = w_out_ab[i]
            o_list = [oa, ob]
            w_list = [w_out[:A_Q][perm].astype(BF16), w_out[A_Q:].astype(BF16)]
        else:
            segs = [(C_QK, True, ATTN_SCALE * math.log2(math.e)), (C_QK, True, 1.0), (C_OUT, False, 1.0)]
            qc, kc, vc = _in_proj(xs, mods_l, cos_t, sin_t, w_in_c[i].astype(BF16), segs, all_tiles,
                                  tiles_per_batch, n_batch)
            oc = _attn_c(qc, kc, vc.T, lam_c[i].astype(F32), subln_c[i].astype(F32), _lambda_init(l),
                         n_batch, n_lat, n_ctx, not last)
            o_list = [oc]
            w_list = [w_out_c[i].astype(BF16)]

        if l % 2 == 0:
            xs, h2 = _out_proj(o_list, w_list, xs, mods_l, ln_g[l, 0], ln_b[l, 0], n_tiles,
                               tiles_per_batch, n_batch)
            xs = _ffn_dense(h2, w_ffn_up[i].astype(BF16), w_ffn_down[i].astype(BF16), xs, mods_l,
                            ln_g[l, 1], ln_b[l, 1], n_tiles, tiles_per_batch, n_batch)
        else:
            wr = jnp.zeros((d, LANES), F32).at[:, :N_EXPERTS].set(w_router[i])
            wr_top = _bf16_truncate(wr)
            wr = jnp.concatenate([wr_top.astype(BF16), (wr - wr_top).astype(BF16)], axis=1)
            xs, h2, route = _out_proj(o_list, w_list, xs, mods_l, ln_g[l, 0], ln_b[l, 0], n_tiles,
                                      tiles_per_batch, n_batch, w_router=wr)
            n_rows = n_tiles * ROW_TILE
            row_token, dest, tile_expert, n_used = _routing_plan(route, n_rows)
            hs = jnp.take(h2, row_token, axis=0, mode="clip")
            ys = _ffn_experts(hs, w_exp_up_b, w_exp_down_b, i, tile_expert, n_used)
            y0 = jnp.take(ys, dest[:, 0], axis=0, mode="clip")
            y1 = jnp.take(ys, dest[:, 1], axis=0, mode="clip")
            xs = _moe_combine(y0, y1, route, xs, mods_l, ln_g[l, 1], ln_b[l, 1], n_tiles,
                              tiles_per_batch, n_batch)
    return xs[:lat_rows].reshape(n_batch, n_lat, d)
```

```python
import functools
import math

import jax
import jax.numpy as jnp
import numpy as np
from jax import lax
from jax.experimental import pallas as pl
from jax.experimental.pallas import tpu as pltpu

F32 = jnp.float32
BF16 = jnp.bfloat16

D_MODEL = 1024
DEPTH = 4
GRID_W = 64
HEAD_DIM = 64
LANES = 128
ATTN_SCALE = HEAD_DIM ** -0.5
A_BLOCK = 128
A_WINDOW = 128
A_HEADS = 8
A_KV_HEADS = 2
B_HEADS = 8
NA_KH = 8
NA_KW = 16
C_HEADS = 8
ROPE_THETA = 10000.0
ROPE_AXIS_DIM = HEAD_DIM // 2
FF_DIM = 2816
N_EXPERTS = 8
EXPERT_FF = 3584
LN_EPS = 1e-5
DN_ALPHA = (2 * DEPTH) ** 0.25
MASK_VALUE = -1e30
A_Q = A_HEADS * HEAD_DIM
A_KV = A_KV_HEADS * HEAD_DIM
B_W = B_HEADS * HEAD_DIM
C_QK = 1024
C_OUT = 1024

ROW_TILE = 512
MOE_TILE = 512
FF_CHUNK = 1408
EXPERT_FF_CHUNK = 1792
B_QROWS = 2
B_KROWS = B_QROWS + NA_KH
B_PAIRS = 4
C_QTILE = 256
C_KCHUNK = 512
C_UNROLL = 6
C_STEP_HEADS = 2
VMEM_LIMIT = 56 * 1024 * 1024


def _lane_iota(shape):
    return lax.broadcasted_iota(jnp.int32, shape, len(shape) - 1)


def _layernorm_rows(z, g, b):
    mu = jnp.mean(z, axis=-1, keepdims=True)
    zc = z - mu
    var = jnp.mean(zc * zc, axis=-1, keepdims=True)
    return zc * lax.rsqrt(var + LN_EPS) * g + b


def _mod_kernel(c_ref, w_ref, b_ref, o_ref):
    c = c_ref[...]
    s = c * (1.0 / (1.0 + jnp.exp(-c)))
    o_ref[0] = jnp.dot(s, w_ref[0], preferred_element_type=F32,
                       precision=lax.Precision.HIGHEST) + b_ref[0]


def _mod_vectors(cond, w_mod, b_mod):
    depth, d, n6 = w_mod.shape
    tn = 1536
    return pl.pallas_call(
        _mod_kernel,
        grid=(depth, n6 // tn),
        in_specs=[
            pl.BlockSpec((8, d), lambda l, j: (0, 0)),
            pl.BlockSpec((1, d, tn), lambda l, j: (l, 0, j)),
            pl.BlockSpec((1, 1, tn), lambda l, j: (l, 0, j)),
        ],
        out_specs=pl.BlockSpec((1, 8, tn), lambda l, j: (l, 0, j)),
        out_shape=jax.ShapeDtypeStruct((depth, 8, n6), F32),
        compiler_params=pltpu.CompilerParams(vmem_limit_bytes=VMEM_LIMIT),
    )(cond, w_mod, b_mod.reshape(depth, 1, n6))


def _rope_slab(a, cos, sin_signed, low16):
    fwd = pltpu.roll(a, LANES - 16, axis=1)
    bwd = pltpu.roll(a, 16, axis=1)
    return a * cos + jnp.where(low16, fwd, bwd) * sin_signed


def _proj_kernel(x_ref, sc_ref, sh_ref, cos_ref, sin_ref, w_ref, *out_refs, segments):
    x = x_ref[...]
    h = (x * (1.0 + sc_ref[...]) + sh_ref[...]).astype(BF16)
    cos = cos_ref[...]
    sin = sin_ref[...]
    low16 = (_lane_iota(cos.shape) % 32) < 16
    col = 0
    for o_ref, (width, rope, scale) in zip(out_refs, segments):
        for c0 in range(0, width, 256):
            cw = min(256, width - c0)
            acc = jnp.dot(h, w_ref[:, col + c0:col + c0 + cw], preferred_element_type=F32)
            slabs = []
            for s0 in range(0, cw, LANES):
                a = acc[:, s0:s0 + LANES]
                if rope:
                    a = _rope_slab(a, cos, sin, low16)
                if scale != 1.0:
                    a = a * scale
                slabs.append(a.astype(BF16))
            o_ref[:, c0:c0 + cw] = slabs[0] if len(slabs) == 1 else jnp.concatenate(slabs, axis=1)
        col += width


def _in_proj(xs, mods_l, cos_t, sin_t, w, segments, n_tiles, tiles_per_batch, n_batch):
    t_rows, d = xs.shape
    tm = ROW_TILE
    grp = lambda t: jnp.minimum(t // tiles_per_batch, n_batch)
    pos = lambda t: jnp.where(t < n_batch * tiles_per_batch, t % tiles_per_batch, tiles_per_batch)
    out_shape = [jax.ShapeDtypeStruct((t_rows, s[0]), BF16) for s in segments]
    out_specs = [pl.BlockSpec((tm, s[0]), lambda t: (t, 0)) for s in segments]
    return pl.pallas_call(
        functools.partial(_proj_kernel, segments=tuple(segments)),
        grid=(n_tiles,),
        in_specs=[
            pl.BlockSpec((tm, d), lambda t: (t, 0)),
            pl.BlockSpec((None, None, 1, d), lambda t: (grp(t), 1, 0, 0)),
            pl.BlockSpec((None, None, 1, d), lambda t: (grp(t), 0, 0, 0)),
            pl.BlockSpec((tm, LANES), lambda t: (pos(t), 0)),
            pl.BlockSpec((tm, LANES), lambda t: (pos(t), 0)),
            pl.BlockSpec(w.shape, lambda t: (0, 0)),
        ],
        out_specs=out_specs,
        out_shape=out_shape,
        compiler_params=pltpu.CompilerParams(vmem_limit_bytes=VMEM_LIMIT),
    )(xs, mods_l, mods_l, cos_t, sin_t, w)


def _attn_a_kernel(sink_ref, q_ref, k_ref, vt_ref, kc_ref, vct_ref, o_ref, *, n_blocks, n_lat):
    n = pl.program_id(1)
    blk = A_BLOCK
    win = 3 * blk
    start = pl.multiple_of(jnp.clip((n - 1) * blk, 0, n_lat - win), blk)
    keys = jnp.concatenate([k_ref[pl.ds(start, win), :], kc_ref[...]], axis=0)
    vals_t = jnp.concatenate([vt_ref[:, pl.ds(start, win)], vct_ref[...]], axis=1)
    nk = keys.shape[0]
    qbase = jnp.where(n < n_blocks, n * blk, -(1 << 20))
    kidx = lax.broadcasted_iota(jnp.int32, (nk, 2 * blk), 0)
    qidx = lax.broadcasted_iota(jnp.int32, (nk, 2 * blk), 1)
    qpos = qbase + jnp.where(qidx >= blk, qidx - blk, qidx)
    in_band = jnp.abs(start + kidx - qpos) <= A_WINDOW
    bias = jnp.where((kidx >= win) | in_band, 0.0, MASK_VALUE).astype(F32)
    lo = _lane_iota((blk, LANES)) < HEAD_DIM
    zero = jnp.zeros((blk, LANES), BF16)
    col_hi = _lane_iota((1, 2 * blk)) >= blk
    dn = (((1,), (1,)), ((), ()))
    scores = []
    for j in range(A_HEADS // 2):
        qc = q_ref[:, j * LANES:(j + 1) * LANES]
        qq = jnp.concatenate([jnp.where(lo, qc, zero), jnp.where(lo, zero, qc)], axis=0)
        scores.append(lax.dot_general(keys, qq, dn, preferred_element_type=F32) + bias)
    for j, s in enumerate(scores):
        sink = jnp.where(col_hi, sink_ref[j + A_HEADS // 2], sink_ref[j])
        m = jnp.maximum(jnp.max(s, axis=0, keepdims=True), sink)
        p = jnp.exp2(s - m)
        l = jnp.sum(p, axis=0, keepdims=True) + jnp.exp2(sink - m)
        ot = jnp.dot(vals_t, p.astype(BF16), preferred_element_type=F32) * (1.0 / l)
        o = jnp.concatenate([ot[:HEAD_DIM, :blk], ot[HEAD_DIM:, blk:]], axis=0)
        o_ref[:, j * LANES:(j + 1) * LANES] = o.T.astype(BF16)


def _attn_a(q, k, vt, sink, n_batch, n_lat, n_ctx, with_ctx):
    t_rows = q.shape[0]
    blk = A_BLOCK
    nb = n_lat // blk
    ncb = n_ctx // blk if with_ctx else 0
    qrow = lambda b, n: jnp.where(n < nb, b * nb + n, n_batch * nb + b * (n_ctx // blk) + (n - nb))
    ctx_blk = lambda b: (n_batch * n_lat) // n_ctx + b
    grid_spec = pltpu.PrefetchScalarGridSpec(
        num_scalar_prefetch=1,
        grid=(n_batch, nb + ncb),
        in_specs=[
            pl.BlockSpec((blk, A_Q), lambda b, n, s: (qrow(b, n), 0)),
            pl.BlockSpec((n_lat, A_KV), lambda b, n, s: (b, 0)),
            pl.BlockSpec((A_KV, n_lat), lambda b, n, s: (0, b)),
            pl.BlockSpec((n_ctx, A_KV), lambda b, n, s: (ctx_blk(b), 0)),
            pl.BlockSpec((A_KV, n_ctx), lambda b, n, s: (0, ctx_blk(b))),
        ],
        out_specs=pl.BlockSpec((blk, A_Q), lambda b, n, s: (qrow(b, n), 0)),
    )
    return pl.pallas_call(
        functools.partial(_attn_a_kernel, n_blocks=nb, n_lat=n_lat),
        grid_spec=grid_spec,
        out_shape=jax.ShapeDtypeStruct((t_rows if with_ctx else n_batch * n_lat, A_Q), BF16),
        compiler_params=pltpu.CompilerParams(vmem_limit_bytes=VMEM_LIMIT),
    )(sink, q, k, vt, k, vt)


def _attn_b_kernel(q_ref, k_ref, vt_ref, kc_ref, vct_ref, bias_ref, o_ref, *, rows, n_steps):
    r = pl.program_id(2)
    nq = B_QROWS * GRID_W
    nk = B_KROWS * GRID_W
    r0 = jnp.where(r < n_steps, r * B_QROWS, 0)
    ws = jnp.clip(r0 - NA_KH // 2, 0, rows - B_KROWS)
    start = pl.multiple_of(ws * GRID_W, LANES)
    lo = _lane_iota((nq, LANES)) < HEAD_DIM
    zero = jnp.zeros((nq, LANES), BF16)
    dn = (((1,), (1,)), ((), ()))
    scores = []
    for g in range(B_PAIRS):
        cols = slice(g * LANES, (g + 1) * LANES)
        q = q_ref[:, cols]
        qq = jnp.concatenate([jnp.where(lo, q, zero), jnp.where(lo, zero, q)], axis=0)
        s_loc = lax.dot_general(k_ref[pl.ds(start, nk), cols], qq, dn,
                                preferred_element_type=F32) + bias_ref[g]
        s_ctx = lax.dot_general(kc_ref[:, cols], qq, dn, preferred_element_type=F32)
        scores.append((s_loc, s_ctx))
    for g, (s_loc, s_ctx) in enumerate(scores):
        cols = slice(g * LANES, (g + 1) * LANES)
        m = jnp.maximum(jnp.max(s_loc, axis=0, keepdims=True), jnp.max(s_ctx, axis=0, keepdims=True))
        p_loc = jnp.exp2(s_loc - m)
        p_ctx = jnp.exp2(s_ctx - m)
        l = jnp.sum(p_loc, axis=0, keepdims=True) + jnp.sum(p_ctx, axis=0, keepdims=True)
        ot = (jnp.dot(vt_ref[cols, pl.ds(start, nk)], p_loc.astype(BF16), preferred_element_type=F32)
              + jnp.dot(vct_ref[cols, :], p_ctx.astype(BF16), preferred_element_type=F32)) * (1.0 / l)
        o = jnp.concatenate([ot[:HEAD_DIM, :nq], ot[HEAD_DIM:, nq:]], axis=0)
        o_ref[:, cols] = o.T.astype(BF16)


def _nbr_bias_table(rpb, rows):
    n_steps = rows // B_QROWS
    steps = [min(2, n_steps - 1), 0, 1, n_steps - 2, n_steps - 1]
    w = GRID_W
    n_heads = rpb.shape[0]
    c = np.arange(w)[None, :]
    kc = np.arange(w)[:, None]
    cs = np.clip(c - NA_KW // 2, 0, w - NA_KW)
    col_ok = (kc >= cs) & (kc < cs + NA_KW)
    onehot = (((kc - c + NA_KW - 1)[None] == np.arange(2 * NA_KW - 1)[:, None, None]) & col_ok[None])
    toe = jnp.einsum("hrd,dkc->hrkc", rpb.astype(F32), jnp.asarray(onehot, F32),
                     precision=lax.Precision.HIGHEST)
    toe = jnp.where(col_ok[None, None], toe * math.log2(math.e), MASK_VALUE)
    masked = jnp.full((n_heads, w, w), MASK_VALUE, F32)
    tabs = []
    for st in steps:
        r0 = st * B_QROWS
        ws = int(np.clip(r0 - NA_KH // 2, 0, rows - B_KROWS))
        q_cols = []
        for rq in range(B_QROWS):
            r = r0 + rq
            rs = int(np.clip(r - NA_KH // 2, 0, rows - NA_KH))
            blocks = [toe[:, ws + ki - r + NA_KH - 1] if rs <= ws + ki < rs + NA_KH else masked
                      for ki in range(B_KROWS)]
            q_cols.append(jnp.concatenate(blocks, axis=1))
        tabs.append(jnp.concatenate(q_cols, axis=2))
    tabs.append(jnp.full_like(tabs[0], MASK_VALUE))
    tab = jnp.stack(tabs)
    n_var, _, nk, nq = tab.shape
    tab = tab.reshape(n_var, n_heads // 2, 2, nk, nq)
    return jnp.transpose(tab, (0, 1, 3, 2, 4)).reshape(n_var, n_heads // 2, nk, 2 * nq)


def _attn_b(q, k, vt, bias_tab, n_batch, n_lat, n_ctx, with_ctx):
    t_rows = q.shape[0]
    rows = n_lat // GRID_W
    nq = B_QROWS * GRID_W
    nk = B_KROWS * GRID_W
    gw = B_PAIRS * LANES
    n_steps = rows // B_QROWS
    n_cstep = n_ctx // nq if with_ctx else 0
    qrow = lambda b, r: jnp.where(r < n_steps, b * n_steps + r,
                                  n_batch * n_steps + b * (n_ctx // nq) + (r - n_steps))
    ctx_blk = lambda b: (n_batch * n_lat) // n_ctx + b

    def variant(r):
        v = jnp.where(r == 0, 1, 0)
        v = jnp.where(r == 1, 2, v)
        v = jnp.where(r == n_steps - 2, 3, v)
        v = jnp.where(r == n_steps - 1, 4, v)
        return jnp.where(r >= n_steps, 5, v)

    return pl.pallas_call(
        functools.partial(_attn_b_kernel, rows=rows, n_steps=n_steps),
        grid=(n_batch, B_HEADS // (2 * B_PAIRS), n_steps + n_cstep),
        in_specs=[
            pl.BlockSpec((nq, gw), lambda b, j, r: (qrow(b, r), j)),
            pl.BlockSpec((n_lat, gw), lambda b, j, r: (b, j)),
            pl.BlockSpec((gw, n_lat), lambda b, j, r: (j, b)),
            pl.BlockSpec((n_ctx, gw), lambda b, j, r: (ctx_blk(b), j)),
            pl.BlockSpec((gw, n_ctx), lambda b, j, r: (j, ctx_blk(b))),
            pl.BlockSpec((None, B_PAIRS, nk, 2 * nq), lambda b, j, r: (variant(r), j, 0, 0)),
        ],
        out_specs=pl.BlockSpec((nq, gw), lambda b, j, r: (qrow(b, r), j)),
        out_shape=jax.ShapeDtypeStruct((t_rows if with_ctx else n_batch * n_lat, B_W), BF16),
        compiler_params=pltpu.CompilerParams(vmem_limit_bytes=VMEM_LIMIT),
    )(q, k, vt, k, vt, bias_tab)


def _attn_c_kernel(*refs, latent, n_kchunks, lam_init):
    if latent:
        (lam_ref, q_ref, k_ref, vt_ref, kc_ref, vct_ref, g_ref, o_ref,
         s_a, mx_a, s_b, mx_b, s_c, mx_c, m_scr, l_scr, acc_scr) = refs
        slots = ((s_a, mx_a), (s_b, mx_b))
    else:
        lam_ref, q_ref, kc_ref, vct_ref, g_ref, o_prev_ref, o_ref, s_c, mx_c, m_scr, l_scr, acc_scr = refs
    slot_c = (s_c, mx_c)
    tq = q_ref.shape[0]
    tk = C_KCHUNK
    lo = _lane_iota((tq, LANES)) < HEAD_DIM
    zero = jnp.zeros((tq, LANES), BF16)
    q_maps = []
    for hh in range(C_STEP_HEADS):
        q = q_ref[:, hh * LANES:(hh + 1) * LANES]
        q_maps += [jnp.where(lo, q, zero), jnp.where(lo, zero, q)]
    n_chains = len(q_maps)
    dn = (((1,), (1,)), ((), ()))

    def head_cols(ci):
        return slice((ci // 2) * LANES, (ci // 2 + 1) * LANES)

    def scores(kk_of, slot):
        s_ref, mx_ref = slot
        for ci in range(n_chains):
            s = lax.dot_general(kk_of(head_cols(ci)), q_maps[ci], dn, preferred_element_type=F32)
            s_ref[ci] = s
            mx_ref[ci] = jnp.max(s, axis=0, keepdims=True)

    def softmax_pv(slot, vvt_of):
        s_ref, mx_ref = slot
        for ci in range(n_chains):
            s = s_ref[ci]
            m = m_scr[ci]
            m_new = jnp.maximum(m, mx_ref[ci])
            alpha = jnp.exp2(m - m_new)
            p = jnp.exp2(s - m_new)
            l_scr[ci] = alpha * l_scr[ci] + jnp.sum(p, axis=0, keepdims=True)
            acc_scr[ci] = alpha * acc_scr[ci] + jnp.dot(vvt_of(head_cols(ci)), p.astype(BF16),
                                                        preferred_element_type=F32)
            m_scr[ci] = m_new

    def chunk_slice(c):
        return pl.ds(c * tk if isinstance(c, int) else pl.multiple_of(c * tk, tk), tk)

    def k_chunk(c):
        return lambda cols: k_ref[chunk_slice(c), cols]

    def vt_chunk(c):
        return lambda cols: vt_ref[cols, chunk_slice(c)]

    k_ctx = lambda cols: kc_ref[:, cols]
    vt_ctx = lambda cols: vct_ref[cols, :]

    m_scr[...] = jnp.full(m_scr.shape, -jnp.inf, F32)
    l_scr[...] = jnp.zeros(l_scr.shape, F32)
    acc_scr[...] = jnp.zeros(acc_scr.shape, F32)

    def finish():
        softmax_pv(slot_c, vt_ctx)
        lp = lam_ref[...]
        lam = (jnp.exp(jnp.sum(lp[0:1] * lp[1:2], axis=-1, keepdims=True))
               - jnp.exp(jnp.sum(lp[2:3] * lp[3:4], axis=-1, keepdims=True)) + lam_init)
        for hh in range(C_STEP_HEADS):
            c1, c2 = 2 * hh, 2 * hh + 1
            od = acc_scr[c1] * (1.0 / l_scr[c1]) - lam * (acc_scr[c2] * (1.0 / l_scr[c2]))
            ms = jnp.mean(od * od, axis=0, keepdims=True)
            on = (od * lax.rsqrt(ms + LN_EPS)).T
            o_ref[:, hh * LANES:(hh + 1) * LANES] = ((on * g_ref[...]) * (1.0 - lam_init)).astype(BF16)

    if not latent:
        scores(k_ctx, slot_c)
        finish()
        return

    scores(k_chunk(0), slots[0])

    def body(it, carry):
        c0 = C_UNROLL * it
        for u in range(C_UNROLL):
            scores(k_chunk(c0 + u + 1), slots[(u + 1) % 2])
            softmax_pv(slots[u % 2], vt_chunk(c0 + u))
        return carry

    n_it = (n_kchunks - 1) // C_UNROLL
    lax.fori_loop(0, n_it, body, 0)
    for c in range(C_UNROLL * n_it, n_kchunks):
        if c + 1 < n_kchunks:
            scores(k_chunk(c + 1), slots[(c + 1) % 2])
        else:
            scores(k_ctx, slot_c)
        softmax_pv(slots[c % 2], vt_chunk(c))
    finish()


def _attn_c(q, k, vt, lam_p, subln, lam_init, n_batch, n_lat, n_ctx, with_ctx):
    t_rows = q.shape[0]
    hw = C_STEP_HEADS * LANES
    nch = 2 * C_STEP_HEADS
    ctx_blk = lambda b: (n_batch * n_lat) // n_ctx + b
    out_shape = jax.ShapeDtypeStruct((t_rows if with_ctx else n_batch * n_lat, C_OUT), BF16)
    params = pltpu.CompilerParams(vmem_limit_bytes=VMEM_LIMIT)
    stats = lambda tq: [pltpu.VMEM((nch, 1, tq), F32), pltpu.VMEM((nch, 1, tq), F32),
                        pltpu.VMEM((nch, LANES, tq), F32)]
    slot = lambda nk, tq: [pltpu.VMEM((nch, nk, tq), F32), pltpu.VMEM((nch, 1, tq), F32)]
    lam_spec = pl.BlockSpec((4, HEAD_DIM), lambda b, h, i: (0, 0))
    kc_spec = pl.BlockSpec((n_ctx, hw), lambda b, h, i: (ctx_blk(b), h))
    vct_spec = pl.BlockSpec((hw, n_ctx), lambda b, h, i: (h, ctx_blk(b)))
    g_spec = pl.BlockSpec((1, LANES), lambda b, h, i: (0, 0))
    g = subln.reshape(1, LANES)

    tq = C_QTILE
    n_qlat = n_lat // tq
    o = pl.pallas_call(
        functools.partial(_attn_c_kernel, latent=True, n_kchunks=n_lat // C_KCHUNK, lam_init=lam_init),
        grid=(n_batch, C_HEADS // C_STEP_HEADS, n_qlat),
        in_specs=[
            lam_spec,
            pl.BlockSpec((tq, hw), lambda b, h, i: (b * n_qlat + i, h)),
            pl.BlockSpec((n_lat, hw), lambda b, h, i: (b, h)),
            pl.BlockSpec((hw, n_lat), lambda b, h, i: (h, b)),
            kc_spec, vct_spec, g_spec,
        ],
        out_specs=pl.BlockSpec((tq, hw), lambda b, h, i: (b * n_qlat + i, h)),
        out_shape=out_shape,
        scratch_shapes=(slot(C_KCHUNK, tq) + slot(C_KCHUNK, tq) + slot(n_ctx, tq) + stats(tq)),
        compiler_params=params,
    )(lam_p, q, k, vt, k, vt, g)
    if not with_ctx:
        return o
    return pl.pallas_call(
        functools.partial(_attn_c_kernel, latent=False, n_kchunks=0, lam_init=lam_init),
        grid=(n_batch, C_HEADS // C_STEP_HEADS, 1),
        in_specs=[
            lam_spec,
            pl.BlockSpec((n_ctx, hw), lambda b, h, i: (ctx_blk(b), h)),
            kc_spec, vct_spec, g_spec,
            pl.BlockSpec(memory_space=pl.ANY),
        ],
        out_specs=pl.BlockSpec((n_ctx, hw), lambda b, h, i: (ctx_blk(b), h)),
        out_shape=out_shape,
        scratch_shapes=slot(n_ctx, n_ctx) + stats(n_ctx),
        input_output_aliases={5: 0},
        compiler_params=params,
    )(lam_p, q, k, vt, g, o)


def _bf16_truncate(x):
    bits = lax.bitcast_convert_type(x, jnp.uint32) & jnp.uint32(0xFFFF0000)
    return lax.bitcast_convert_type(bits, F32)


def _top2_router(h, wr_ref):
    h_top = _bf16_truncate(h)
    h_hi = h_top.astype(BF16)
    h_lo = (h - h_top).astype(BF16)
    a = jnp.dot(h_hi, wr_ref[...], preferred_element_type=F32)
    b = jnp.dot(h_lo, wr_ref[:, :LANES], preferred_element_type=F32)
    lg = a[:, :LANES] + a[:, LANES:] + b
    lane = _lane_iota(lg.shape)
    lanef = lane.astype(F32)
    lg = jnp.where(lane < N_EXPERTS, lg, -jnp.inf)
    m1 = jnp.max(lg, axis=-1, keepdims=True)
    i1 = jnp.min(jnp.where(lg == m1, lanef, float(LANES)), axis=-1, keepdims=True)
    lg2 = jnp.where(lanef == i1, -jnp.inf, lg)
    m2 = jnp.max(lg2, axis=-1, keepdims=True)
    i2 = jnp.min(jnp.where(lg2 == m2, lanef, float(LANES)), axis=-1, keepdims=True)
    e = jnp.exp(m2 - m1)
    w1 = 1.0 / (1.0 + e)
    w2 = e / (1.0 + e)
    out = jnp.where(lane == 0, i1, 0.0)
    out = jnp.where(lane == 1, i2, out)
    out = jnp.where(lane == 2, w1, out)
    return jnp.where(lane == 3, w2, out)


def _pack_bf16_pairs(h):
    bits = lax.bitcast_convert_type(h, jnp.uint32)
    bits = bits + (jnp.uint32(0x7FFF) + ((bits >> 16) & jnp.uint32(1)))
    half = h.shape[1] // 2
    packed = (bits[:, half:] & jnp.uint32(0xFFFF0000)) | (bits[:, :half] >> 16)
    return lax.bitcast_convert_type(packed, F32)


def _unpack_bf16_pairs(words):
    p = lax.bitcast_convert_type(words, jnp.uint32)
    lo = lax.bitcast_convert_type(p << 16, F32)
    hi = lax.bitcast_convert_type(p & jnp.uint32(0xFFFF0000), F32)
    return jnp.concatenate([lo, hi], axis=1).astype(BF16)


def _outproj_kernel(*refs, n_in, router):
    o_refs = refs[:n_in]
    w_refs = refs[n_in:2 * n_in]
    x_ref, g_ref, lng_ref, lnb_ref, sc_ref, sh_ref = refs[2 * n_in:2 * n_in + 6]
    rest = refs[2 * n_in + 6:]
    y = jnp.dot(o_refs[0][...], w_refs[0][...], preferred_element_type=F32)
    for o_r, w_r in zip(o_refs[1:], w_refs[1:]):
        y = y + jnp.dot(o_r[...], w_r[...], preferred_element_type=F32)
    xn = _layernorm_rows(DN_ALPHA * x_ref[...] + g_ref[...] * y, lng_ref[...], lnb_ref[...])
    h2 = xn * (1.0 + sc_ref[...]) + sh_ref[...]
    if router:
        wr_ref, xo_ref, h_ref, r_ref = rest
        r_ref[...] = _top2_router(h2, wr_ref)
        h_ref[...] = _pack_bf16_pairs(h2)
    else:
        xo_ref, h_ref = rest
        h_ref[...] = h2.astype(BF16)
    xo_ref[...] = xn


def _out_proj(o_list, w_list, xs, mods_l, ln_g, ln_b, n_tiles, tiles_per_batch, n_batch, w_router=None):
    d = xs.shape[1]
    tm = ROW_TILE
    t_rows = n_tiles * tm
    grp = lambda t: jnp.minimum(t // tiles_per_batch, n_batch)
    mod = lambda k: pl.BlockSpec((None, None, 1, d), lambda t: (grp(t), k, 0, 0))
    in_specs = [pl.BlockSpec((tm, o.shape[1]), lambda t: (t, 0)) for o in o_list]
    in_specs += [pl.BlockSpec(w.shape, lambda t: (0, 0)) for w in w_list]
    in_specs += [pl.BlockSpec((tm, d), lambda t: (t, 0)), mod(2),
                 pl.BlockSpec((1, d), lambda t: (0, 0)), pl.BlockSpec((1, d), lambda t: (0, 0)),
                 mod(4), mod(3)]
    args = list(o_list) + list(w_list) + [xs, mods_l, ln_g.reshape(1, d), ln_b.reshape(1, d), mods_l, mods_l]
    out_shape = [jax.ShapeDtypeStruct((t_rows, d), F32), jax.ShapeDtypeStruct((t_rows, d), BF16)]
    out_specs = [pl.BlockSpec((tm, d), lambda t: (t, 0)), pl.BlockSpec((tm, d), lambda t: (t, 0))]
    if w_router is not None:
        out_shape[1] = jax.ShapeDtypeStruct((t_rows, d // 2), F32)
        out_specs[1] = pl.BlockSpec((tm, d // 2), lambda t: (t, 0))
        in_specs.append(pl.BlockSpec(w_router.shape, lambda t: (0, 0)))
        args.append(w_router)
        out_shape.append(jax.ShapeDtypeStruct((t_rows, LANES), F32))
        out_specs.append(pl.BlockSpec((tm, LANES), lambda t: (t, 0)))
    return pl.pallas_call(
        functools.partial(_outproj_kernel, n_in=len(o_list), router=w_router is not None),
        grid=(n_tiles,),
        in_specs=in_specs,
        out_specs=out_specs,
        out_shape=out_shape,
        compiler_params=pltpu.CompilerParams(vmem_limit_bytes=VMEM_LIMIT),
    )(*args)


def _swiglu_acc(h, wg_ref, wu_ref, wd_ref, acc_ref, c):
    g = jnp.dot(h, wg_ref[...], preferred_element_type=F32)
    u = jnp.dot(h, wu_ref[...], preferred_element_type=F32)
    a = (g * (1.0 / (1.0 + jnp.exp(-g))) * u).astype(BF16)
    part = jnp.dot(a, wd_ref[...], preferred_element_type=F32)

    @pl.when(c == 0)
    def _():
        acc_ref[...] = part

    @pl.when(c > 0)
    def _():
        acc_ref[...] += part


def _ffn_dense_kernel(h_ref, wg_ref, wu_ref, wd_ref, x_ref, g_ref, lng_ref, lnb_ref, xo_ref, acc_ref):
    c = pl.program_id(1)
    _swiglu_acc(h_ref[...], wg_ref, wu_ref, wd_ref, acc_ref, c)

    @pl.when(c == pl.num_programs(1) - 1)
    def _():
        z = DN_ALPHA * x_ref[...] + g_ref[...] * acc_ref[...]
        xo_ref[...] = _layernorm_rows(z, lng_ref[...], lnb_ref[...])


def _ffn_dense(h2, w_up, w_down, xs, mods_l, ln_g, ln_b, n_tiles, tiles_per_batch, n_batch):
    d = xs.shape[1]
    tm = ROW_TILE
    t_rows = n_tiles * tm
    ff = w_down.shape[0]
    nc = ff // FF_CHUNK
    grp = lambda t: jnp.minimum(t // tiles_per_batch, n_batch)
    return pl.pallas_call(
        _ffn_dense_kernel,
        grid=(n_tiles, nc),
        in_specs=[
            pl.BlockSpec((tm, d), lambda t, c: (t, 0)),
            pl.BlockSpec((d, FF_CHUNK), lambda t, c: (0, c)),
            pl.BlockSpec((d, FF_CHUNK), lambda t, c: (0, nc + c)),
            pl.BlockSpec((FF_CHUNK, d), lambda t, c: (c, 0)),
            pl.BlockSpec((tm, d), lambda t, c: (t, 0)),
            pl.BlockSpec((None, None, 1, d), lambda t, c: (grp(t), 5, 0, 0)),
            pl.BlockSpec((1, d), lambda t, c: (0, 0)),
            pl.BlockSpec((1, d), lambda t, c: (0, 0)),
        ],
        out_specs=pl.BlockSpec((tm, d), lambda t, c: (t, 0)),
        out_shape=jax.ShapeDtypeStruct((t_rows, d), F32),
        scratch_shapes=[pltpu.VMEM((tm, d), F32)],
        compiler_params=pltpu.CompilerParams(vmem_limit_bytes=VMEM_LIMIT),
    )(h2, w_up, w_up, w_down, xs, mods_l, ln_g.reshape(1, d), ln_b.reshape(1, d))


def _ffn_expert_kernel(te_ref, nu_ref, h_ref, wg_ref, wu_ref, wd_ref, y_ref, acc_ref):
    i = pl.program_id(0)
    c = pl.program_id(1)

    @pl.when(i < nu_ref[0])
    def _():
        _swiglu_acc(_unpack_bf16_pairs(h_ref[...]), wg_ref, wu_ref, wd_ref, acc_ref, c)

    @pl.when(c == pl.num_programs(1) - 1)
    def _():
        y_ref[...] = acc_ref[...]


def _ffn_experts(hs, w_up, w_down, layer, tile_expert, n_used):
    p_rows = hs.shape[0]
    d = w_down.shape[-1]
    tm = MOE_TILE
    n_tiles = p_rows // tm
    ef = w_down.shape[2]
    nc = ef // EXPERT_FF_CHUNK
    chunk = lambda i, c, nu: jnp.where(i < nu[0], c, nc - 1)
    grid_spec = pltpu.PrefetchScalarGridSpec(
        num_scalar_prefetch=2,
        grid=(n_tiles, nc),
        in_specs=[
            pl.BlockSpec((tm, d // 2), lambda i, c, te, nu: (i, 0)),
            pl.BlockSpec((None, None, d, EXPERT_FF_CHUNK),
                         lambda i, c, te, nu: (layer, te[i], 0, chunk(i, c, nu))),
            pl.BlockSpec((None, None, d, EXPERT_FF_CHUNK),
                         lambda i, c, te, nu: (layer, te[i], 0, nc + chunk(i, c, nu))),
            pl.BlockSpec((None, None, EXPERT_FF_CHUNK, d),
                         lambda i, c, te, nu: (layer, te[i], chunk(i, c, nu), 0)),
        ],
        out_specs=pl.BlockSpec((tm, d), lambda i, c, te, nu: (i, 0)),
        scratch_shapes=[pltpu.VMEM((tm, d), F32)],
    )
    return pl.pallas_call(
        _ffn_expert_kernel,
        grid_spec=grid_spec,
        out_shape=jax.ShapeDtypeStruct((p_rows, d), F32),
        compiler_params=pltpu.CompilerParams(vmem_limit_bytes=VMEM_LIMIT),
    )(tile_expert, n_used, hs, w_up, w_up, w_down)


def _combine_kernel(y0_ref, y1_ref, r_ref, x_ref, g_ref, lng_ref, lnb_ref, xo_ref):
    r = r_ref[...]
    f = r[:, 2:3] * y0_ref[...] + r[:, 3:4] * y1_ref[...]
    z = DN_ALPHA * x_ref[...] + g_ref[...] * f
    xo_ref[...] = _layernorm_rows(z, lng_ref[...], lnb_ref[...])


def _moe_combine(y0, y1, route, xs, mods_l, ln_g, ln_b, n_tiles, tiles_per_batch, n_batch):
    d = xs.shape[1]
    tm = ROW_TILE
    t_rows = n_tiles * tm
    grp = lambda t: jnp.minimum(t // tiles_per_batch, n_batch)
    row = pl.BlockSpec((tm, d), lambda t: (t, 0))
    return pl.pallas_call(
        _combine_kernel,
        grid=(n_tiles,),
        in_specs=[row, row, pl.BlockSpec((tm, LANES), lambda t: (t, 0)), row,
                  pl.BlockSpec((None, None, 1, d), lambda t: (grp(t), 5, 0, 0)),
                  pl.BlockSpec((1, d), lambda t: (0, 0)), pl.BlockSpec((1, d), lambda t: (0, 0))],
        out_specs=row,
        out_shape=jax.ShapeDtypeStruct((t_rows, d), F32),
        compiler_params=pltpu.CompilerParams(vmem_limit_bytes=VMEM_LIMIT),
    )(y0, y1, route, xs, mods_l, ln_g.reshape(1, d), ln_b.reshape(1, d))


def _routing_plan(route, n_rows):
    tm = MOE_TILE
    e_idx = route[:n_rows, 0:2].astype(jnp.int32).reshape(-1)
    onehot = (e_idx[:, None] == jnp.arange(N_EXPERTS, dtype=jnp.int32)[None, :]).astype(jnp.int32)
    csum = jnp.cumsum(onehot, axis=0)
    counts = csum[-1]
    rank = jnp.sum((csum - onehot) * onehot, axis=1)
    padded = ((counts + tm - 1) // tm) * tm
    ends = jnp.cumsum(padded)
    starts = ends - padded
    dest = starts[e_idx] + rank
    n_tiles = (2 * n_rows + N_EXPERTS * (tm - 1)) // tm
    p_rows = n_tiles * tm
    row_token = jnp.zeros((p_rows,), jnp.int32).at[dest].set(jnp.arange(2 * n_rows, dtype=jnp.int32) // 2)
    tile_start = jnp.arange(n_tiles, dtype=jnp.int32) * tm
    tile_expert = jnp.minimum(jnp.sum((tile_start[:, None] >= ends[None, :]).astype(jnp.int32), axis=1),
                              N_EXPERTS - 1)
    n_used = (ends[-1] // tm).astype(jnp.int32).reshape(1)
    last = tile_expert[jnp.maximum(n_used[0] - 1, 0)]
    tile_expert = jnp.where(jnp.arange(n_tiles) < n_used[0], tile_expert, last).astype(jnp.int32)
    return row_token, dest.reshape(n_rows, 2), tile_expert, n_used


def _rope_tables(n_lat):
    t = jnp.arange(n_lat, dtype=jnp.int32)
    row = (t // GRID_W).astype(F32)
    col = (t % GRID_W).astype(F32)
    inv = ROPE_THETA ** (-jnp.arange(0, ROPE_AXIS_DIM, 2, dtype=F32) / ROPE_AXIS_DIM)
    ar = row[:, None] * inv[None, :]
    ac = col[:, None] * inv[None, :]
    ang = jnp.concatenate([ar, ar, ac, ac], axis=-1)
    cos = jnp.tile(jnp.cos(ang), (1, LANES // HEAD_DIM))
    sin = jnp.tile(jnp.sin(ang), (1, LANES // HEAD_DIM))
    sign = jnp.where((jnp.arange(LANES) % 32) < 16, -1.0, 1.0).astype(F32)
    cos = jnp.concatenate([cos, jnp.ones((ROW_TILE, LANES), F32)], axis=0)
    sin = jnp.concatenate([sin * sign[None, :], jnp.zeros((ROW_TILE, LANES), F32)], axis=0)
    return cos, sin


def _lambda_init(layer):
    return 0.8 - 0.6 * math.exp(-0.3 * layer)


def kernel(x, c, ctx, c_ctx, w_mod, b_mod, ln_g, ln_b, w_in_ab, w_out_ab, sink_a, rpb_b, w_in_c, w_out_c,
           lam_c, subln_c, w_ffn_up, w_ffn_down, w_router, w_exp_up, w_exp_down):
    n_batch, n_lat, d = x.shape
    n_ctx = ctx.shape[1]
    assert d == D_MODEL and n_batch * n_ctx == ROW_TILE and n_lat % ROW_TILE == 0
    assert n_batch + 1 <= 8 and n_lat % C_KCHUNK == 0 and n_lat % C_QTILE == 0 and n_ctx % LANES == 0
    rows = n_lat // GRID_W
    assert rows >= 12 and rows % B_QROWS == 0
    tiles_per_batch = n_lat // ROW_TILE
    lat_tiles = n_batch * tiles_per_batch
    lat_rows = n_batch * n_lat

    xs = jnp.concatenate([x.reshape(lat_rows, d), ctx.reshape(n_batch * n_ctx, d)], axis=0)
    cond = jnp.zeros((8, d), F32).at[:n_batch].set(c).at[n_batch].set(c_ctx)
    mods = _mod_vectors(cond, w_mod, b_mod).reshape(DEPTH, 8, 6, 1, d)
    cos_t, sin_t = _rope_tables(n_lat)
    w_exp_up_b = w_exp_up.astype(BF16)
    w_exp_down_b = w_exp_down.astype(BF16)

    perm = np.array([(j + (A_HEADS // 2) * half) * HEAD_DIM + dd
                     for j in range(A_HEADS // 2) for half in range(2) for dd in range(HEAD_DIM)])

    for l in range(DEPTH):
        last = l == DEPTH - 1
        i = l // 2
        all_tiles = lat_tiles + 1
        n_tiles = lat_tiles if last else all_tiles
        mods_l = mods[l]
        if l % 2 == 0:
            w_in = w_in_ab[i]
            w_in = jnp.concatenate([w_in[:, :A_Q][:, perm], w_in[:, A_Q:]], axis=1).astype(BF16)
            segs = [(A_Q, True, ATTN_SCALE * math.log2(math.e)), (A_KV, True, 1.0), (A_KV, False, 1.0),
                    (B_W, False, ATTN_SCALE * math.log2(math.e)), (B_W, False, 1.0), (B_W, False, 1.0)]
            qa, ka, va, qb, kb, vb = _in_proj(xs, mods_l, cos_t, sin_t, w_in, segs, all_tiles,
                                              tiles_per_batch, n_batch)
            sink = sink_a[i].astype(F32) * math.log2(math.e)
            oa = _attn_a(qa, ka, va.T, sink, n_batch, n_lat, n_ctx, not last)
            ob = _attn_b(qb, kb, vb.T, _nbr_bias_table(rpb_b[i], rows), n_batch, n_lat, n_ctx, not last)
            w_out = w_out_ab[i]
            o_list = [oa, ob]
            w_list = [w_out[:A_Q][perm].astype(BF16), w_out[A_Q:].astype(BF16)]
        else:
            segs = [(C_QK, True, ATTN_SCALE * math.log2(math.e)), (C_QK, True, 1.0), (C_OUT, False, 1.0)]
            qc, kc, vc = _in_proj(xs, mods_l, cos_t, sin_t, w_in_c[i].astype(BF16), segs, all_tiles,
                                  tiles_per_batch, n_batch)
            oc = _attn_c(qc, kc, vc.T, lam_c[i].astype(F32), subln_c[i].astype(F32), _lambda_init(l),
                         n_batch, n_lat, n_ctx, not last)
            o_list = [oc]
            w_list = [w_out_c[i].astype(BF16)]

        if l % 2 == 0:
            xs, h2 = _out_proj(o_list, w_list, xs, mods_l, ln_g[l, 0], ln_b[l, 0], n_tiles,
                               tiles_per_batch, n_batch)
            xs = _ffn_dense(h2, w_ffn_up[i].astype(BF16), w_ffn_down[i].astype(BF16), xs, mods_l,
                            ln_g[l, 1], ln_b[l, 1], n_tiles, tiles_per_batch, n_batch)
        else:
            wr = jnp.zeros((d, LANES), F32).at[:, :N_EXPERTS].set(w_router[i])
            wr_top = _bf16_truncate(wr)
            wr = jnp.concatenate([wr_top.astype(BF16), (wr - wr_top).astype(BF16)], axis=1)
            xs, h2, route = _out_proj(o_list, w_list, xs, mods_l, ln_g[l, 0], ln_b[l, 0], n_tiles,
                                      tiles_per_batch, n_batch, w_router=wr)
            n_rows = n_tiles * ROW_TILE
            row_token, dest, tile_expert, n_used = _routing_plan(route, n_rows)
            hs = jnp.take(h2, row_token, axis=0, mode="clip")
            ys = _ffn_experts(hs, w_exp_up_b, w_exp_down_b, i, tile_expert, n_used)
            y0 = jnp.take(ys, dest[:, 0], axis=0, mode="clip")
            y1 = jnp.take(ys, dest[:, 1], axis=0, mode="clip")
            xs = _moe_combine(y0, y1, route, xs, mods_l, ln_g[l, 1], ln_b[l, 1], n_tiles,
                              tiles_per_batch, n_batch)
    return xs[:lat_rows].reshape(n_batch, n_lat, d)
```

```python
import functools
import math

import jax
import jax.numpy as jnp
import numpy as np
from jax import lax
from jax.experimental import pallas as pl
from jax.experimental.pallas import tpu as pltpu

F32 = jnp.float32
BF16 = jnp.bfloat16

D_MODEL = 1024
DEPTH = 4
GRID_W = 64
HEAD_DIM = 64
LANES = 128
ATTN_SCALE = HEAD_DIM ** -0.5
A_BLOCK = 128
A_WINDOW = 128
A_HEADS = 8
A_KV_HEADS = 2
B_HEADS = 8
NA_KH = 8
NA_KW = 16
C_HEADS = 8
ROPE_THETA = 10000.0
ROPE_AXIS_DIM = HEAD_DIM // 2
FF_DIM = 2816
N_EXPERTS = 8
EXPERT_FF = 3584
LN_EPS = 1e-5
DN_ALPHA = (2 * DEPTH) ** 0.25
MASK_VALUE = -1e30
A_Q = A_HEADS * HEAD_DIM
A_KV = A_KV_HEADS * HEAD_DIM
B_W = B_HEADS * HEAD_DIM
C_QK = 1024
C_OUT = 1024

ROW_TILE = 512
MOE_TILE = 512
FF_CHUNK = 1408
EXPERT_FF_CHUNK = 1792
B_QROWS = 2
B_KROWS = B_QROWS + NA_KH
B_PAIRS = 4
C_QTILE = 256
C_KCHUNK = 512
C_UNROLL = 6
C_STEP_HEADS = 2
VMEM_LIMIT = 56 * 1024 * 1024


def _lane_iota(shape):
    return lax.broadcasted_iota(jnp.int32, shape, len(shape) - 1)


def _layernorm_rows(z, g, b):
    mu = jnp.mean(z, axis=-1, keepdims=True)
    zc = z - mu
    var = jnp.mean(zc * zc, axis=-1, keepdims=True)
    return zc * lax.rsqrt(var + LN_EPS) * g + b


def _mod_kernel(c_ref, w_ref, b_ref, o_ref):
    c = c_ref[...]
    s = c * (1.0 / (1.0 + jnp.exp(-c)))
    o_ref[0] = jnp.dot(s, w_ref[0], preferred_element_type=F32,
                       precision=lax.Precision.HIGHEST) + b_ref[0]


def _mod_vectors(cond, w_mod, b_mod):
    depth, d, n6 = w_mod.shape
    tn = 1536
    return pl.pallas_call(
        _mod_kernel,
        grid=(depth, n6 // tn),
        in_specs=[
            pl.BlockSpec((8, d), lambda l, j: (0, 0)),
            pl.BlockSpec((1, d, tn), lambda l, j: (l, 0, j)),
            pl.BlockSpec((1, 1, tn), lambda l, j: (l, 0, j)),
        ],
        out_specs=pl.BlockSpec((1, 8, tn), lambda l, j: (l, 0, j)),
        out_shape=jax.ShapeDtypeStruct((depth, 8, n6), F32),
        compiler_params=pltpu.CompilerParams(vmem_limit_bytes=VMEM_LIMIT),
    )(cond, w_mod, b_mod.reshape(depth, 1, n6))


def _rope_slab(a, cos, sin_signed, low16):
    fwd = pltpu.roll(a, LANES - 16, axis=1)
    bwd = pltpu.roll(a, 16, axis=1)
    return a * cos + jnp.where(low16, fwd, bwd) * sin_signed


def _proj_kernel(x_ref, sc_ref, sh_ref, cos_ref, sin_ref, w_ref, *out_refs, segments):
    x = x_ref[...]
    h = (x * (1.0 + sc_ref[...]) + sh_ref[...]).astype(BF16)
    cos = cos_ref[...]
    sin = sin_ref[...]
    low16 = (_lane_iota(cos.shape) % 32) < 16
    col = 0
    for o_ref, (width, rope, scale, transposed) in zip(out_refs, segments):
        for c0 in range(0, width, 256):
            cw = min(256, width - c0)
            acc = jnp.dot(h, w_ref[:, col + c0:col + c0 + cw], preferred_element_type=F32)
            if transposed:
                o_ref[c0:c0 + cw, :] = acc.T.astype(BF16)
                continue
            slabs = []
            for s0 in range(0, cw, LANES):
                a = acc[:, s0:s0 + LANES]
                if rope:
                    a = _rope_slab(a, cos, sin, low16)
                if scale != 1.0:
                    a = a * scale
                slabs.append(a.astype(BF16))
            o_ref[:, c0:c0 + cw] = slabs[0] if len(slabs) == 1 else jnp.concatenate(slabs, axis=1)
        col += width


def _in_proj(xs, mods_l, cos_t, sin_t, w, segments, n_tiles, tiles_per_batch, n_batch):
    t_rows, d = xs.shape
    tm = ROW_TILE
    grp = lambda t: jnp.minimum(t // tiles_per_batch, n_batch)
    pos = lambda t: jnp.where(t < n_batch * tiles_per_batch, t % tiles_per_batch, tiles_per_batch)
    out_shape = [jax.ShapeDtypeStruct((s[0], t_rows) if s[3] else (t_rows, s[0]), BF16) for s in segments]
    out_specs = [pl.BlockSpec((s[0], tm), lambda t: (0, t)) if s[3] else pl.BlockSpec((tm, s[0]), lambda t: (t, 0))
                 for s in segments]
    return pl.pallas_call(
        functools.partial(_proj_kernel, segments=tuple(segments)),
        grid=(n_tiles,),
        in_specs=[
            pl.BlockSpec((tm, d), lambda t: (t, 0)),
            pl.BlockSpec((None, None, 1, d), lambda t: (grp(t), 1, 0, 0)),
            pl.BlockSpec((None, None, 1, d), lambda t: (grp(t), 0, 0, 0)),
            pl.BlockSpec((tm, LANES), lambda t: (pos(t), 0)),
            pl.BlockSpec((tm, LANES), lambda t: (pos(t), 0)),
            pl.BlockSpec(w.shape, lambda t: (0, 0)),
        ],
        out_specs=out_specs,
        out_shape=out_shape,
        compiler_params=pltpu.CompilerParams(vmem_limit_bytes=VMEM_LIMIT),
    )(xs, mods_l, mods_l, cos_t, sin_t, w)


def _attn_a_kernel(sink_ref, q_ref, k_ref, vt_ref, kc_ref, vct_ref, o_ref, *, n_blocks, n_lat):
    n = pl.program_id(1)
    blk = A_BLOCK
    win = 3 * blk
    start = pl.multiple_of(jnp.clip((n - 1) * blk, 0, n_lat - win), blk)
    keys = jnp.concatenate([k_ref[pl.ds(start, win), :], kc_ref[...]], axis=0)
    vals_t = jnp.concatenate([vt_ref[:, pl.ds(start, win)], vct_ref[...]], axis=1)
    nk = keys.shape[0]
    qbase = jnp.where(n < n_blocks, n * blk, -(1 << 20))
    kidx = lax.broadcasted_iota(jnp.int32, (nk, 2 * blk), 0)
    qidx = lax.broadcasted_iota(jnp.int32, (nk, 2 * blk), 1)
    qpos = qbase + jnp.where(qidx >= blk, qidx - blk, qidx)
    in_band = jnp.abs(start + kidx - qpos) <= A_WINDOW
    bias = jnp.where((kidx >= win) | in_band, 0.0, MASK_VALUE).astype(F32)
    lo = _lane_iota((blk, LANES)) < HEAD_DIM
    zero = jnp.zeros((blk, LANES), BF16)
    col_hi = _lane_iota((1, 2 * blk)) >= blk
    dn = (((1,), (1,)), ((), ()))
    scores = []
    for j in range(A_HEADS // 2):
        qc = q_ref[:, j * LANES:(j + 1) * LANES]
        qq = jnp.concatenate([jnp.where(lo, qc, zero), jnp.where(lo, zero, qc)], axis=0)
        scores.append(lax.dot_general(keys, qq, dn, preferred_element_type=F32) + bias)
    for j, s in enumerate(scores):
        sink = jnp.where(col_hi, sink_ref[j + A_HEADS // 2], sink_ref[j])
        m = jnp.maximum(jnp.max(s, axis=0, keepdims=True), sink)
        p = jnp.exp2(s - m)
        l = jnp.sum(p, axis=0, keepdims=True) + jnp.exp2(sink - m)
        ot = jnp.dot(vals_t, p.astype(BF16), preferred_element_type=F32) * (1.0 / l)
        o = jnp.concatenate([ot[:HEAD_DIM, :blk], ot[HEAD_DIM:, blk:]], axis=0)
        o_ref[:, j * LANES:(j + 1) * LANES] = o.T.astype(BF16)


def _attn_a(q, k, vt, sink, n_batch, n_lat, n_ctx, with_ctx):
    t_rows = q.shape[0]
    blk = A_BLOCK
    nb = n_lat // blk
    ncb = n_ctx // blk if with_ctx else 0
    qrow = lambda b, n: jnp.where(n < nb, b * nb + n, n_batch * nb + b * (n_ctx // blk) + (n - nb))
    ctx_blk = lambda b: (n_batch * n_lat) // n_ctx + b
    grid_spec = pltpu.PrefetchScalarGridSpec(
        num_scalar_prefetch=1,
        grid=(n_batch, nb + ncb),
        in_specs=[
            pl.BlockSpec((blk, A_Q), lambda b, n, s: (qrow(b, n), 0)),
            pl.BlockSpec((n_lat, A_KV), lambda b, n, s: (b, 0)),
            pl.BlockSpec((A_KV, n_lat), lambda b, n, s: (0, b)),
            pl.BlockSpec((n_ctx, A_KV), lambda b, n, s: (ctx_blk(b), 0)),
            pl.BlockSpec((A_KV, n_ctx), lambda b, n, s: (0, ctx_blk(b))),
        ],
        out_specs=pl.BlockSpec((blk, A_Q), lambda b, n, s: (qrow(b, n), 0)),
    )
    return pl.pallas_call(
        functools.partial(_attn_a_kernel, n_blocks=nb, n_lat=n_lat),
        grid_spec=grid_spec,
        out_shape=jax.ShapeDtypeStruct((t_rows if with_ctx else n_batch * n_lat, A_Q), BF16),
        compiler_params=pltpu.CompilerParams(vmem_limit_bytes=VMEM_LIMIT),
    )(sink, q, k, vt, k, vt)


def _attn_b_kernel(q_ref, k_ref, vt_ref, kc_ref, vct_ref, bias_ref, o_ref, *, rows, n_steps):
    r = pl.program_id(2)
    nq = B_QROWS * GRID_W
    nk = B_KROWS * GRID_W
    r0 = jnp.where(r < n_steps, r * B_QROWS, 0)
    ws = jnp.clip(r0 - NA_KH // 2, 0, rows - B_KROWS)
    start = pl.multiple_of(ws * GRID_W, LANES)
    lo = _lane_iota((nq, LANES)) < HEAD_DIM
    zero = jnp.zeros((nq, LANES), BF16)
    dn = (((1,), (1,)), ((), ()))
    scores = []
    for g in range(B_PAIRS):
        cols = slice(g * LANES, (g + 1) * LANES)
        q = q_ref[:, cols]
        qq = jnp.concatenate([jnp.where(lo, q, zero), jnp.where(lo, zero, q)], axis=0)
        s_loc = lax.dot_general(k_ref[pl.ds(start, nk), cols], qq, dn,
                                preferred_element_type=F32) + bias_ref[g]
        s_ctx = lax.dot_general(kc_ref[:, cols], qq, dn, preferred_element_type=F32)
        scores.append((s_loc, s_ctx))
    for g, (s_loc, s_ctx) in enumerate(scores):
        cols = slice(g * LANES, (g + 1) * LANES)
        m = jnp.maximum(jnp.max(s_loc, axis=0, keepdims=True), jnp.max(s_ctx, axis=0, keepdims=True))
        p_loc = jnp.exp2(s_loc - m)
        p_ctx = jnp.exp2(s_ctx - m)
        l = jnp.sum(p_loc, axis=0, keepdims=True) + jnp.sum(p_ctx, axis=0, keepdims=True)
        ot = (jnp.dot(vt_ref[cols, pl.ds(start, nk)], p_loc.astype(BF16), preferred_element_type=F32)
              + jnp.dot(vct_ref[cols, :], p_ctx.astype(BF16), preferred_element_type=F32)) * (1.0 / l)
        o = jnp.concatenate([ot[:HEAD_DIM, :nq], ot[HEAD_DIM:, nq:]], axis=0)
        o_ref[:, cols] = o.T.astype(BF16)


def _nbr_bias_table(rpb, rows):
    n_steps = rows // B_QROWS
    steps = [min(2, n_steps - 1), 0, 1, n_steps - 2, n_steps - 1]
    w = GRID_W
    n_heads = rpb.shape[0]
    c = np.arange(w)[None, :]
    kc = np.arange(w)[:, None]
    cs = np.clip(c - NA_KW // 2, 0, w - NA_KW)
    col_ok = (kc >= cs) & (kc < cs + NA_KW)
    onehot = (((kc - c + NA_KW - 1)[None] == np.arange(2 * NA_KW - 1)[:, None, None]) & col_ok[None])
    toe = jnp.einsum("hrd,dkc->hrkc", rpb.astype(F32), jnp.asarray(onehot, F32),
                     precision=lax.Precision.HIGHEST)
    toe = jnp.where(col_ok[None, None], toe * math.log2(math.e), MASK_VALUE)
    masked = jnp.full((n_heads, w, w), MASK_VALUE, F32)
    tabs = []
    for st in steps:
        r0 = st * B_QROWS
        ws = int(np.clip(r0 - NA_KH // 2, 0, rows - B_KROWS))
        q_cols = []
        for rq in range(B_QROWS):
            r = r0 + rq
            rs = int(np.clip(r - NA_KH // 2, 0, rows - NA_KH))
            blocks = [toe[:, ws + ki - r + NA_KH - 1] if rs <= ws + ki < rs + NA_KH else masked
                      for ki in range(B_KROWS)]
            q_cols.append(jnp.concatenate(blocks, axis=1))
        tabs.append(jnp.concatenate(q_cols, axis=2))
    tabs.append(jnp.full_like(tabs[0], MASK_VALUE))
    tab = jnp.stack(tabs)
    n_var, _, nk, nq = tab.shape
    tab = tab.reshape(n_var, n_heads // 2, 2, nk, nq)
    return jnp.transpose(tab, (0, 1, 3, 2, 4)).reshape(n_var, n_heads // 2, nk, 2 * nq)


def _attn_b(q, k, vt, bias_tab, n_batch, n_lat, n_ctx, with_ctx):
    t_rows = q.shape[0]
    rows = n_lat // GRID_W
    nq = B_QROWS * GRID_W
    nk = B_KROWS * GRID_W
    gw = B_PAIRS * LANES
    n_steps = rows // B_QROWS
    n_cstep = n_ctx // nq if with_ctx else 0
    qrow = lambda b, r: jnp.where(r < n_steps, b * n_steps + r,
                                  n_batch * n_steps + b * (n_ctx // nq) + (r - n_steps))
    ctx_blk = lambda b: (n_batch * n_lat) // n_ctx + b

    def variant(r):
        v = jnp.where(r == 0, 1, 0)
        v = jnp.where(r == 1, 2, v)
        v = jnp.where(r == n_steps - 2, 3, v)
        v = jnp.where(r == n_steps - 1, 4, v)
        return jnp.where(r >= n_steps, 5, v)

    return pl.pallas_call(
        functools.partial(_attn_b_kernel, rows=rows, n_steps=n_steps),
        grid=(n_batch, B_HEADS // (2 * B_PAIRS), n_steps + n_cstep),
        in_specs=[
            pl.BlockSpec((nq, gw), lambda b, j, r: (qrow(b, r), j)),
            pl.BlockSpec((n_lat, gw), lambda b, j, r: (b, j)),
            pl.BlockSpec((gw, n_lat), lambda b, j, r: (j, b)),
            pl.BlockSpec((n_ctx, gw), lambda b, j, r: (ctx_blk(b), j)),
            pl.BlockSpec((gw, n_ctx), lambda b, j, r: (j, ctx_blk(b))),
            pl.BlockSpec((None, B_PAIRS, nk, 2 * nq), lambda b, j, r: (variant(r), j, 0, 0)),
        ],
        out_specs=pl.BlockSpec((nq, gw), lambda b, j, r: (qrow(b, r), j)),
        out_shape=jax.ShapeDtypeStruct((t_rows if with_ctx else n_batch * n_lat, B_W), BF16),
        compiler_params=pltpu.CompilerParams(vmem_limit_bytes=VMEM_LIMIT),
    )(q, k, vt, k, vt, bias_tab)


def _attn_c_kernel(*refs, latent, n_kchunks, lam_init):
    if latent:
        (lam_ref, q_ref, k_ref, vt_ref, kc_ref, vct_ref, g_ref, o_ref,
         s_a, mx_a, s_b, mx_b, s_c, mx_c, m_scr, l_scr, acc_scr) = refs
        slots = ((s_a, mx_a), (s_b, mx_b))
    else:
        lam_ref, q_ref, kc_ref, vct_ref, g_ref, o_prev_ref, o_ref, s_c, mx_c, m_scr, l_scr, acc_scr = refs
    slot_c = (s_c, mx_c)
    tq = q_ref.shape[0]
    tk = C_KCHUNK
    lo = _lane_iota((tq, LANES)) < HEAD_DIM
    zero = jnp.zeros((tq, LANES), BF16)
    q_maps = []
    for hh in range(C_STEP_HEADS):
        q = q_ref[:, hh * LANES:(hh + 1) * LANES]
        q_maps += [jnp.where(lo, q, zero), jnp.where(lo, zero, q)]
    n_chains = len(q_maps)
    dn = (((1,), (1,)), ((), ()))

    def head_cols(ci):
        return slice((ci // 2) * LANES, (ci // 2 + 1) * LANES)

    def scores(kk_of, slot):
        s_ref, mx_ref = slot
        for ci in range(n_chains):
            s = lax.dot_general(kk_of(head_cols(ci)), q_maps[ci], dn, preferred_element_type=F32)
            s_ref[ci] = s
            mx_ref[ci] = jnp.max(s, axis=0, keepdims=True)

    def softmax_pv(slot, vvt_of):
        s_ref, mx_ref = slot
        for ci in range(n_chains):
            s = s_ref[ci]
            m = m_scr[ci]
            m_new = jnp.maximum(m, mx_ref[ci])
            alpha = jnp.exp2(m - m_new)
            p = jnp.exp2(s - m_new)
            l_scr[ci] = alpha * l_scr[ci] + jnp.sum(p, axis=0, keepdims=True)
            acc_scr[ci] = alpha * acc_scr[ci] + jnp.dot(vvt_of(head_cols(ci)), p.astype(BF16),
                                                        preferred_element_type=F32)
            m_scr[ci] = m_new

    def chunk_slice(c):
        return pl.ds(c * tk if isinstance(c, int) else pl.multiple_of(c * tk, tk), tk)

    def k_chunk(c):
        return lambda cols: k_ref[chunk_slice(c), cols]

    def vt_chunk(c):
        return lambda cols: vt_ref[cols, chunk_slice(c)]

    k_ctx = lambda cols: kc_ref[:, cols]
    vt_ctx = lambda cols: vct_ref[cols, :]

    m_scr[...] = jnp.full(m_scr.shape, -jnp.inf, F32)
    l_scr[...] = jnp.zeros(l_scr.shape, F32)
    acc_scr[...] = jnp.zeros(acc_scr.shape, F32)

    def finish():
        softmax_pv(slot_c, vt_ctx)
        lp = lam_ref[...]
        lam = (jnp.exp(jnp.sum(lp[0:1] * lp[1:2], axis=-1, keepdims=True))
               - jnp.exp(jnp.sum(lp[2:3] * lp[3:4], axis=-1, keepdims=True)) + lam_init)
        for hh in range(C_STEP_HEADS):
            c1, c2 = 2 * hh, 2 * hh + 1
            od = acc_scr[c1] * (1.0 / l_scr[c1]) - lam * (acc_scr[c2] * (1.0 / l_scr[c2]))
            ms = jnp.mean(od * od, axis=0, keepdims=True)
            on = (od * lax.rsqrt(ms + LN_EPS)).T
            o_ref[:, hh * LANES:(hh + 1) * LANES] = ((on * g_ref[...]) * (1.0 - lam_init)).astype(BF16)

    if not latent:
        scores(k_ctx, slot_c)
        finish()
        return

    scores(k_chunk(0), slots[0])

    def body(it, carry):
        c0 = C_UNROLL * it
        for u in range(C_UNROLL):
            scores(k_chunk(c0 + u + 1), slots[(u + 1) % 2])
            softmax_pv(slots[u % 2], vt_chunk(c0 + u))
        return carry

    n_it = (n_kchunks - 1) // C_UNROLL
    lax.fori_loop(0, n_it, body, 0)
    for c in range(C_UNROLL * n_it, n_kchunks):
        if c + 1 < n_kchunks:
            scores(k_chunk(c + 1), slots[(c + 1) % 2])
        else:
            scores(k_ctx, slot_c)
        softmax_pv(slots[c % 2], vt_chunk(c))
    finish()


def _attn_c(q, k, vt, lam_p, subln, lam_init, n_batch, n_lat, n_ctx, with_ctx):
    t_rows = q.shape[0]
    hw = C_STEP_HEADS * LANES
    nch = 2 * C_STEP_HEADS
    ctx_blk = lambda b: (n_batch * n_lat) // n_ctx + b
    out_shape = jax.ShapeDtypeStruct((t_rows if with_ctx else n_batch * n_lat, C_OUT), BF16)
    params = pltpu.CompilerParams(vmem_limit_bytes=VMEM_LIMIT)
    stats = lambda tq: [pltpu.VMEM((nch, 1, tq), F32), pltpu.VMEM((nch, 1, tq), F32),
                        pltpu.VMEM((nch, LANES, tq), F32)]
    slot = lambda nk, tq: [pltpu.VMEM((nch, nk, tq), F32), pltpu.VMEM((nch, 1, tq), F32)]
    lam_spec = pl.BlockSpec((4, HEAD_DIM), lambda b, h, i: (0, 0))
    kc_spec = pl.BlockSpec((n_ctx, hw), lambda b, h, i: (ctx_blk(b), h))
    vct_spec = pl.BlockSpec((hw, n_ctx), lambda b, h, i: (h, ctx_blk(b)))
    g_spec = pl.BlockSpec((1, LANES), lambda b, h, i: (0, 0))
    g = subln.reshape(1, LANES)

    tq = C_QTILE
    n_qlat = n_lat // tq
    o = pl.pallas_call(
        functools.partial(_attn_c_kernel, latent=True, n_kchunks=n_lat // C_KCHUNK, lam_init=lam_init),
        grid=(n_batch, C_HEADS // C_STEP_HEADS, n_qlat),
        in_specs=[
            lam_spec,
            pl.BlockSpec((tq, hw), lambda b, h, i: (b * n_qlat + i, h)),
            pl.BlockSpec((n_lat, hw), lambda b, h, i: (b, h)),
            pl.BlockSpec((hw, n_lat), lambda b, h, i: (h, b)),
            kc_spec, vct_spec, g_spec,
        ],
        out_specs=pl.BlockSpec((tq, hw), lambda b, h, i: (b * n_qlat + i, h)),
        out_shape=out_shape,
        scratch_shapes=(slot(C_KCHUNK, tq) + slot(C_KCHUNK, tq) + slot(n_ctx, tq) + stats(tq)),
        compiler_params=params,
    )(lam_p, q, k, vt, k, vt, g)
    if not with_ctx:
        return o
    return pl.pallas_call(
        functools.partial(_attn_c_kernel, latent=False, n_kchunks=0, lam_init=lam_init),
        grid=(n_batch, C_HEADS // C_STEP_HEADS, 1),
        in_specs=[
            lam_spec,
            pl.BlockSpec((n_ctx, hw), lambda b, h, i: (ctx_blk(b), h)),
            kc_spec, vct_spec, g_spec,
            pl.BlockSpec(memory_space=pl.ANY),
        ],
        out_specs=pl.BlockSpec((n_ctx, hw), lambda b, h, i: (ctx_blk(b), h)),
        out_shape=out_shape,
        scratch_shapes=slot(n_ctx, n_ctx) + stats(n_ctx),
        input_output_aliases={5: 0},
        compiler_params=params,
    )(lam_p, q, k, vt, g, o)


def _bf16_truncate(x):
    bits = lax.bitcast_convert_type(x, jnp.uint32) & jnp.uint32(0xFFFF0000)
    return lax.bitcast_convert_type(bits, F32)


def _top2_router(h, wr_ref):
    h_top = _bf16_truncate(h)
    h_hi = h_top.astype(BF16)
    h_lo = (h - h_top).astype(BF16)
    a = jnp.dot(h_hi, wr_ref[...], preferred_element_type=F32)
    b = jnp.dot(h_lo, wr_ref[:, :LANES], preferred_element_type=F32)
    lg = a[:, :LANES] + a[:, LANES:] + b
    lane = _lane_iota(lg.shape)
    lanef = lane.astype(F32)
    lg = jnp.where(lane < N_EXPERTS, lg, -jnp.inf)
    m1 = jnp.max(lg, axis=-1, keepdims=True)
    i1 = jnp.min(jnp.where(lg == m1, lanef, float(LANES)), axis=-1, keepdims=True)
    lg2 = jnp.where(lanef == i1, -jnp.inf, lg)
    m2 = jnp.max(lg2, axis=-1, keepdims=True)
    i2 = jnp.min(jnp.where(lg2 == m2, lanef, float(LANES)), axis=-1, keepdims=True)
    e = jnp.exp(m2 - m1)
    w1 = 1.0 / (1.0 + e)
    w2 = e / (1.0 + e)
    out = jnp.where(lane == 0, i1, 0.0)
    out = jnp.where(lane == 1, i2, out)
    out = jnp.where(lane == 2, w1, out)
    return jnp.where(lane == 3, w2, out)


def _pack_bf16_pairs(h):
    bits = lax.bitcast_convert_type(h, jnp.uint32)
    bits = bits + (jnp.uint32(0x7FFF) + ((bits >> 16) & jnp.uint32(1)))
    half = h.shape[1] // 2
    packed = (bits[:, half:] & jnp.uint32(0xFFFF0000)) | (bits[:, :half] >> 16)
    return lax.bitcast_convert_type(packed, F32)


def _unpack_bf16_pairs(words):
    p = lax.bitcast_convert_type(words, jnp.uint32)
    lo = lax.bitcast_convert_type(p << 16, F32)
    hi = lax.bitcast_convert_type(p & jnp.uint32(0xFFFF0000), F32)
    return jnp.concatenate([lo, hi], axis=1).astype(BF16)


def _outproj_kernel(*refs, n_in, router):
    o_refs = refs[:n_in]
    w_refs = refs[n_in:2 * n_in]
    x_ref, g_ref, lng_ref, lnb_ref, sc_ref, sh_ref = refs[2 * n_in:2 * n_in + 6]
    rest = refs[2 * n_in + 6:]
    y = jnp.dot(o_refs[0][...], w_refs[0][...], preferred_element_type=F32)
    for o_r, w_r in zip(o_refs[1:], w_refs[1:]):
        y = y + jnp.dot(o_r[...], w_r[...], preferred_element_type=F32)
    xn = _layernorm_rows(DN_ALPHA * x_ref[...] + g_ref[...] * y, lng_ref[...], lnb_ref[...])
    h2 = xn * (1.0 + sc_ref[...]) + sh_ref[...]
    if router:
        wr_ref, xo_ref, h_ref, r_ref = rest
        r_ref[...] = _top2_router(h2, wr_ref)
        h_ref[...] = _pack_bf16_pairs(h2)
    else:
        xo_ref, h_ref = rest
        h_ref[...] = h2.astype(BF16)
    xo_ref[...] = xn


def _out_proj(o_list, w_list, xs, mods_l, ln_g, ln_b, n_tiles, tiles_per_batch, n_batch, w_router=None):
    d = xs.shape[1]
    tm = ROW_TILE
    t_rows = n_tiles * tm
    grp = lambda t: jnp.minimum(t // tiles_per_batch, n_batch)
    mod = lambda k: pl.BlockSpec((None, None, 1, d), lambda t: (grp(t), k, 0, 0))
    in_specs = [pl.BlockSpec((tm, o.shape[1]), lambda t: (t, 0)) for o in o_list]
    in_specs += [pl.BlockSpec(w.shape, lambda t: (0, 0)) for w in w_list]
    in_specs += [pl.BlockSpec((tm, d), lambda t: (t, 0)), mod(2),
                 pl.BlockSpec((1, d), lambda t: (0, 0)), pl.BlockSpec((1, d), lambda t: (0, 0)),
                 mod(4), mod(3)]
    args = list(o_list) + list(w_list) + [xs, mods_l, ln_g.reshape(1, d), ln_b.reshape(1, d), mods_l, mods_l]
    out_shape = [jax.ShapeDtypeStruct((t_rows, d), F32), jax.ShapeDtypeStruct((t_rows, d), BF16)]
    out_specs = [pl.BlockSpec((tm, d), lambda t: (t, 0)), pl.BlockSpec((tm, d), lambda t: (t, 0))]
    if w_router is not None:
        out_shape[1] = jax.ShapeDtypeStruct((t_rows, d // 2), F32)
        out_specs[1] = pl.BlockSpec((tm, d // 2), lambda t: (t, 0))
        in_specs.append(pl.BlockSpec(w_router.shape, lambda t: (0, 0)))
        args.append(w_router)
        out_shape.append(jax.ShapeDtypeStruct((t_rows, LANES), F32))
        out_specs.append(pl.BlockSpec((tm, LANES), lambda t: (t, 0)))
    return pl.pallas_call(
        functools.partial(_outproj_kernel, n_in=len(o_list), router=w_router is not None),
        grid=(n_tiles,),
        in_specs=in_specs,
        out_specs=out_specs,
        out_shape=out_shape,
        compiler_params=pltpu.CompilerParams(vmem_limit_bytes=VMEM_LIMIT),
    )(*args)


def _swiglu_acc(h, wg_ref, wu_ref, wd_ref, acc_ref, c):
    g = jnp.dot(h, wg_ref[...], preferred_element_type=F32)
    u = jnp.dot(h, wu_ref[...], preferred_element_type=F32)
    a = (g * (1.0 / (1.0 + jnp.exp(-g))) * u).astype(BF16)
    part = jnp.dot(a, wd_ref[...], preferred_element_type=F32)

    @pl.when(c == 0)
    def _():
        acc_ref[...] = part

    @pl.when(c > 0)
    def _():
        acc_ref[...] += part


def _ffn_dense_kernel(h_ref, wg_ref, wu_ref, wd_ref, x_ref, g_ref, lng_ref, lnb_ref, xo_ref, acc_ref):
    c = pl.program_id(1)
    _swiglu_acc(h_ref[...], wg_ref, wu_ref, wd_ref, acc_ref, c)

    @pl.when(c == pl.num_programs(1) - 1)
    def _():
        z = DN_ALPHA * x_ref[...] + g_ref[...] * acc_ref[...]
        xo_ref[...] = _layernorm_rows(z, lng_ref[...], lnb_ref[...])


def _ffn_dense(h2, w_up, w_down, xs, mods_l, ln_g, ln_b, n_tiles, tiles_per_batch, n_batch):
    d = xs.shape[1]
    tm = ROW_TILE
    t_rows = n_tiles * tm
    ff = w_down.shape[0]
    nc = ff // FF_CHUNK
    grp = lambda t: jnp.minimum(t // tiles_per_batch, n_batch)
    return pl.pallas_call(
        _ffn_dense_kernel,
        grid=(n_tiles, nc),
        in_specs=[
            pl.BlockSpec((tm, d), lambda t, c: (t, 0)),
            pl.BlockSpec((d, FF_CHUNK), lambda t, c: (0, c)),
            pl.BlockSpec((d, FF_CHUNK), lambda t, c: (0, nc + c)),
            pl.BlockSpec((FF_CHUNK, d), lambda t, c: (c, 0)),
            pl.BlockSpec((tm, d), lambda t, c: (t, 0)),
            pl.BlockSpec((None, None, 1, d), lambda t, c: (grp(t), 5, 0, 0)),
            pl.BlockSpec((1, d), lambda t, c: (0, 0)),
            pl.BlockSpec((1, d), lambda t, c: (0, 0)),
        ],
        out_specs=pl.BlockSpec((tm, d), lambda t, c: (t, 0)),
        out_shape=jax.ShapeDtypeStruct((t_rows, d), F32),
        scratch_shapes=[pltpu.VMEM((tm, d), F32)],
        compiler_params=pltpu.CompilerParams(vmem_limit_bytes=VMEM_LIMIT),
    )(h2, w_up, w_up, w_down, xs, mods_l, ln_g.reshape(1, d), ln_b.reshape(1, d))


def _ffn_expert_kernel(te_ref, nu_ref, h_ref, wg_ref, wu_ref, wd_ref, y_ref, acc_ref):
    i = pl.program_id(0)
    c = pl.program_id(1)

    @pl.when(i < nu_ref[0])
    def _():
        _swiglu_acc(_unpack_bf16_pairs(h_ref[...]), wg_ref, wu_ref, wd_ref, acc_ref, c)

    @pl.when(c == pl.num_programs(1) - 1)
    def _():
        y_ref[...] = acc_ref[...]


def _ffn_experts(hs, w_up, w_down, layer, tile_expert, n_used):
    p_rows = hs.shape[0]
    d = w_down.shape[-1]
    tm = MOE_TILE
    n_tiles = p_rows // tm
    ef = w_down.shape[2]
    nc = ef // EXPERT_FF_CHUNK
    chunk = lambda i, c, nu: jnp.where(i < nu[0], c, nc - 1)
    grid_spec = pltpu.PrefetchScalarGridSpec(
        num_scalar_prefetch=2,
        grid=(n_tiles, nc),
        in_specs=[
            pl.BlockSpec((tm, d // 2), lambda i, c, te, nu: (i, 0)),
            pl.BlockSpec((None, None, d, EXPERT_FF_CHUNK),
                         lambda i, c, te, nu: (layer, te[i], 0, chunk(i, c, nu))),
            pl.BlockSpec((None, None, d, EXPERT_FF_CHUNK),
                         lambda i, c, te, nu: (layer, te[i], 0, nc + chunk(i, c, nu))),
            pl.BlockSpec((None, None, EXPERT_FF_CHUNK, d),
                         lambda i, c, te, nu: (layer, te[i], chunk(i, c, nu), 0)),
        ],
        out_specs=pl.BlockSpec((tm, d), lambda i, c, te, nu: (i, 0)),
        scratch_shapes=[pltpu.VMEM((tm, d), F32)],
    )
    return pl.pallas_call(
        _ffn_expert_kernel,
        grid_spec=grid_spec,
        out_shape=jax.ShapeDtypeStruct((p_rows, d), F32),
        compiler_params=pltpu.CompilerParams(vmem_limit_bytes=VMEM_LIMIT),
    )(tile_expert, n_used, hs, w_up, w_up, w_down)


def _combine_kernel(y0_ref, y1_ref, r_ref, x_ref, g_ref, lng_ref, lnb_ref, xo_ref):
    r = r_ref[...]
    f = r[:, 2:3] * y0_ref[...] + r[:, 3:4] * y1_ref[...]
    z = DN_ALPHA * x_ref[...] + g_ref[...] * f
    xo_ref[...] = _layernorm_rows(z, lng_ref[...], lnb_ref[...])


def _moe_combine(y0, y1, route, xs, mods_l, ln_g, ln_b, n_tiles, tiles_per_batch, n_batch):
    d = xs.shape[1]
    tm = ROW_TILE
    t_rows = n_tiles * tm
    grp = lambda t: jnp.minimum(t // tiles_per_batch, n_batch)
    row = pl.BlockSpec((tm, d), lambda t: (t, 0))
    return pl.pallas_call(
        _combine_kernel,
        grid=(n_tiles,),
        in_specs=[row, row, pl.BlockSpec((tm, LANES), lambda t: (t, 0)), row,
                  pl.BlockSpec((None, None, 1, d), lambda t: (grp(t), 5, 0, 0)),
                  pl.BlockSpec((1, d), lambda t: (0, 0)), pl.BlockSpec((1, d), lambda t: (0, 0))],
        out_specs=row,
        out_shape=jax.ShapeDtypeStruct((t_rows, d), F32),
        compiler_params=pltpu.CompilerParams(vmem_limit_bytes=VMEM_LIMIT),
    )(y0, y1, route, xs, mods_l, ln_g.reshape(1, d), ln_b.reshape(1, d))


def _routing_plan(route, n_rows):
    tm = MOE_TILE
    e_idx = route[:n_rows, 0:2].astype(jnp.int32).reshape(-1)
    onehot = (e_idx[:, None] == jnp.arange(N_EXPERTS, dtype=jnp.int32)[None, :]).astype(jnp.int32)
    csum = jnp.cumsum(onehot, axis=0)
    counts = csum[-1]
    rank = jnp.sum((csum - onehot) * onehot, axis=1)
    padded = ((counts + tm - 1) // tm) * tm
    ends = jnp.cumsum(padded)
    starts = ends - padded
    dest = starts[e_idx] + rank
    n_tiles = (2 * n_rows + N_EXPERTS * (tm - 1)) // tm
    p_rows = n_tiles * tm
    row_token = jnp.zeros((p_rows,), jnp.int32).at[dest].set(jnp.arange(2 * n_rows, dtype=jnp.int32) // 2)
    tile_start = jnp.arange(n_tiles, dtype=jnp.int32) * tm
    tile_expert = jnp.minimum(jnp.sum((tile_start[:, None] >= ends[None, :]).astype(jnp.int32), axis=1),
                              N_EXPERTS - 1)
    n_used = (ends[-1] // tm).astype(jnp.int32).reshape(1)
    last = tile_expert[jnp.maximum(n_used[0] - 1, 0)]
    tile_expert = jnp.where(jnp.arange(n_tiles) < n_used[0], tile_expert, last).astype(jnp.int32)
    return row_token, dest.reshape(n_rows, 2), tile_expert, n_used


def _rope_tables(n_lat):
    t = jnp.arange(n_lat, dtype=jnp.int32)
    row = (t // GRID_W).astype(F32)
    col = (t % GRID_W).astype(F32)
    inv = ROPE_THETA ** (-jnp.arange(0, ROPE_AXIS_DIM, 2, dtype=F32) / ROPE_AXIS_DIM)
    ar = row[:, None] * inv[None, :]
    ac = col[:, None] * inv[None, :]
    ang = jnp.concatenate([ar, ar, ac, ac], axis=-1)
    cos = jnp.tile(jnp.cos(ang), (1, LANES // HEAD_DIM))
    sin = jnp.tile(jnp.sin(ang), (1, LANES // HEAD_DIM))
    sign = jnp.where((jnp.arange(LANES) % 32) < 16, -1.0, 1.0).astype(F32)
    cos = jnp.concatenate([cos, jnp.ones((ROW_TILE, LANES), F32)], axis=0)
    sin = jnp.concatenate([sin * sign[None, :], jnp.zeros((ROW_TILE, LANES), F32)], axis=0)
    return cos, sin


def _lambda_init(layer):
    return 0.8 - 0.6 * math.exp(-0.3 * layer)


def kernel(x, c, ctx, c_ctx, w_mod, b_mod, ln_g, ln_b, w_in_ab, w_out_ab, sink_a, rpb_b, w_in_c, w_out_c,
           lam_c, subln_c, w_ffn_up, w_ffn_down, w_router, w_exp_up, w_exp_down):
    n_batch, n_lat, d = x.shape
    n_ctx = ctx.shape[1]
    assert d == D_MODEL and n_batch * n_ctx == ROW_TILE and n_lat % ROW_TILE == 0
    assert n_batch + 1 <= 8 and n_lat % C_KCHUNK == 0 and n_lat % C_QTILE == 0 and n_ctx % LANES == 0
    rows = n_lat // GRID_W
    assert rows >= 12 and rows % B_QROWS == 0
    tiles_per_batch = n_lat // ROW_TILE
    lat_tiles = n_batch * tiles_per_batch
    lat_rows = n_batch * n_lat

    xs = jnp.concatenate([x.reshape(lat_rows, d), ctx.reshape(n_batch * n_ctx, d)], axis=0)
    cond = jnp.zeros((8, d), F32).at[:n_batch].set(c).at[n_batch].set(c_ctx)
    mods = _mod_vectors(cond, w_mod, b_mod).reshape(DEPTH, 8, 6, 1, d)
    cos_t, sin_t = _rope_tables(n_lat)
    w_exp_up_b = w_exp_up.astype(BF16)
    w_exp_down_b = w_exp_down.astype(BF16)

    perm = np.array([(j + (A_HEADS // 2) * half) * HEAD_DIM + dd
                     for j in range(A_HEADS // 2) for half in range(2) for dd in range(HEAD_DIM)])

    for l in range(DEPTH):
        last = l == DEPTH - 1
        i = l // 2
        all_tiles = lat_tiles + 1
        n_tiles = lat_tiles if last else all_tiles
        mods_l = mods[l]
        if l % 2 == 0:
            w_in = w_in_ab[i]
            w_in = jnp.concatenate([w_in[:, :A_Q][:, perm], w_in[:, A_Q:]], axis=1).astype(BF16)
            q_scale = ATTN_SCALE * math.log2(math.e)
            segs = [(A_Q, True, q_scale, False), (A_KV, True, 1.0, False), (A_KV, False, 1.0, True),
                    (B_W, False, q_scale, False), (B_W, False, 1.0, False), (B_W, False, 1.0, True)]
            qa, ka, va, qb, kb, vb = _in_proj(xs, mods_l, cos_t, sin_t, w_in, segs, all_tiles,
                                              tiles_per_batch, n_batch)
            sink = sink_a[i].astype(F32) * math.log2(math.e)
            oa = _attn_a(qa, ka, va, sink, n_batch, n_lat, n_ctx, not last)
            ob = _attn_b(qb, kb, vb, _nbr_bias_table(rpb_b[i], rows), n_batch, n_lat, n_ctx, not last)
            w_out = w_out_ab[i]
            o_list = [oa, ob]
            w_list = [w_out[:A_Q][perm].astype(BF16), w_out[A_Q:].astype(BF16)]
        else:
            segs = [(C_QK, True, ATTN_SCALE * math.log2(math.e), False), (C_QK, True, 1.0, False),
                    (C_OUT, False, 1.0, True)]
            qc, kc, vc = _in_proj(xs, mods_l, cos_t, sin_t, w_in_c[i].astype(BF16), segs, all_tiles,
                                  tiles_per_batch, n_batch)
            oc = _attn_c(qc, kc, vc, lam_c[i].astype(F32), subln_c[i].astype(F32), _lambda_init(l),
                         n_batch, n_lat, n_ctx, not last)
            o_list = [oc]
            w_list = [w_out_c[i].astype(BF16)]

        if l % 2 == 0:
            xs, h2 = _out_proj(o_list, w_list, xs, mods_l, ln_g[l, 0], ln_b[l, 0], n_tiles,
                               tiles_per_batch, n_batch)
            xs = _ffn_dense(h2, w_ffn_up[i].astype(BF16), w_ffn_down[i].astype(BF16), xs, mods_l,
                            ln_g[l, 1], ln_b[l, 1], n_tiles, tiles_per_batch, n_batch)
        else:
            wr = jnp.zeros((d, LANES), F32).at[:, :N_EXPERTS].set(w_router[i])
            wr_top = _bf16_truncate(wr)
            wr = jnp.concatenate([wr_top.astype(BF16), (wr - wr_top).astype(BF16)], axis=1)
            xs, h2, route = _out_proj(o_list, w_list, xs, mods_l, ln_g[l, 0], ln_b[l, 0], n_tiles,
                                      tiles_per_batch, n_batch, w_router=wr)
            n_rows = n_tiles * ROW_TILE
            row_token, dest, tile_expert, n_used = _routing_plan(route, n_rows)
            hs = jnp.take(h2, row_token, axis=0, mode="clip")
            ys = _ffn_experts(hs, w_exp_up_b, w_exp_down_b, i, tile_expert, n_used)
            y0 = jnp.take(ys, dest[:, 0], axis=0, mode="clip")
            y1 = jnp.take(ys, dest[:, 1], axis=0, mode="clip")
            xs = _moe_combine(y0, y1, route, xs, mods_l, ln_g[l, 1], ln_b[l, 1], n_tiles,
                              tiles_per_batch, n_batch)
    return xs[:lat_rows].reshape(n_batch, n_lat, d)
```

```python
import functools
import math

import jax
import jax.numpy as jnp
import numpy as np
from jax import lax
from jax.experimental import pallas as pl
from jax.experimental.pallas import tpu as pltpu

F32 = jnp.float32
BF16 = jnp.bfloat16

D_MODEL = 1024
DEPTH = 4
GRID_W = 64
HEAD_DIM = 64
LANES = 128
ATTN_SCALE = HEAD_DIM ** -0.5
A_BLOCK = 128
A_WINDOW = 128
A_HEADS = 8
A_KV_HEADS = 2
B_HEADS = 8
NA_KH = 8
NA_KW = 16
C_HEADS = 8
ROPE_THETA = 10000.0
ROPE_AXIS_DIM = HEAD_DIM // 2
FF_DIM = 2816
N_EXPERTS = 8
EXPERT_FF = 3584
LN_EPS = 1e-5
DN_ALPHA = (2 * DEPTH) ** 0.25
MASK_VALUE = -1e30
A_Q = A_HEADS * HEAD_DIM
A_KV = A_KV_HEADS * HEAD_DIM
B_W = B_HEADS * HEAD_DIM
C_QK = 1024
C_OUT = 1024

ROW_TILE = 512
MOE_TILE = 512
FF_CHUNK = 1408
EXPERT_FF_CHUNK = 1792
B_QROWS = 2
B_KROWS = B_QROWS + NA_KH
B_PAIRS = 4
C_QTILE = 256
C_KCHUNK = 512
C_UNROLL = 6
C_STEP_HEADS = 2
VMEM_LIMIT = 56 * 1024 * 1024


def _lane_iota(shape):
    return lax.broadcasted_iota(jnp.int32, shape, len(shape) - 1)


def _layernorm_rows(z, g, b):
    mu = jnp.mean(z, axis=-1, keepdims=True)
    zc = z - mu
    var = jnp.mean(zc * zc, axis=-1, keepdims=True)
    return zc * lax.rsqrt(var + LN_EPS) * g + b


def _mod_kernel(c_ref, w_ref, b_ref, o_ref):
    c = c_ref[...]
    s = c * (1.0 / (1.0 + jnp.exp(-c)))
    o_ref[0] = jnp.dot(s, w_ref[0], preferred_element_type=F32,
                       precision=lax.Precision.HIGHEST) + b_ref[0]


def _mod_vectors(cond, w_mod, b_mod):
    depth, d, n6 = w_mod.shape
    tn = 1536
    return pl.pallas_call(
        _mod_kernel,
        grid=(depth, n6 // tn),
        in_specs=[
            pl.BlockSpec((8, d), lambda l, j: (0, 0)),
            pl.BlockSpec((1, d, tn), lambda l, j: (l, 0, j)),
            pl.BlockSpec((1, 1, tn), lambda l, j: (l, 0, j)),
        ],
        out_specs=pl.BlockSpec((1, 8, tn), lambda l, j: (l, 0, j)),
        out_shape=jax.ShapeDtypeStruct((depth, 8, n6), F32),
        compiler_params=pltpu.CompilerParams(vmem_limit_bytes=VMEM_LIMIT),
    )(cond, w_mod, b_mod.reshape(depth, 1, n6))


def _rope_slab(a, cos, sin_signed, low16):
    fwd = pltpu.roll(a, LANES - 16, axis=1)
    bwd = pltpu.roll(a, 16, axis=1)
    return a * cos + jnp.where(low16, fwd, bwd) * sin_signed


def _proj_kernel(x_ref, sc_ref, sh_ref, cos_ref, sin_ref, w_ref, *out_refs, segments):
    x = x_ref[...]
    h = (x * (1.0 + sc_ref[...]) + sh_ref[...]).astype(BF16)
    cos = cos_ref[...]
    sin = sin_ref[...]
    low16 = (_lane_iota(cos.shape) % 32) < 16
    col = 0
    for o_ref, (width, rope, scale, transposed) in zip(out_refs, segments):
        for c0 in range(0, width, 256):
            cw = min(256, width - c0)
            acc = jnp.dot(h, w_ref[:, col + c0:col + c0 + cw], preferred_element_type=F32)
            if transposed:
                o_ref[c0:c0 + cw, :] = acc.T.astype(BF16)
                continue
            slabs = []
            for s0 in range(0, cw, LANES):
                a = acc[:, s0:s0 + LANES]
                if rope:
                    a = _rope_slab(a, cos, sin, low16)
                if scale != 1.0:
                    a = a * scale
                slabs.append(a.astype(BF16))
            o_ref[:, c0:c0 + cw] = slabs[0] if len(slabs) == 1 else jnp.concatenate(slabs, axis=1)
        col += width


def _in_proj(xs, mods_l, cos_t, sin_t, w, segments, n_tiles, tiles_per_batch, n_batch):
    t_rows, d = xs.shape
    tm = ROW_TILE
    grp = lambda t: jnp.minimum(t // tiles_per_batch, n_batch)
    pos = lambda t: jnp.where(t < n_batch * tiles_per_batch, t % tiles_per_batch, tiles_per_batch)
    out_shape = [jax.ShapeDtypeStruct((s[0], t_rows) if s[3] else (t_rows, s[0]), BF16) for s in segments]
    out_specs = [pl.BlockSpec((s[0], tm), lambda t: (0, t)) if s[3] else pl.BlockSpec((tm, s[0]), lambda t: (t, 0))
                 for s in segments]
    return pl.pallas_call(
        functools.partial(_proj_kernel, segments=tuple(segments)),
        grid=(n_tiles,),
        in_specs=[
            pl.BlockSpec((tm, d), lambda t: (t, 0)),
            pl.BlockSpec((None, None, 1, d), lambda t: (grp(t), 1, 0, 0)),
            pl.BlockSpec((None, None, 1, d), lambda t: (grp(t), 0, 0, 0)),
            pl.BlockSpec((tm, LANES), lambda t: (pos(t), 0)),
            pl.BlockSpec((tm, LANES), lambda t: (pos(t), 0)),
            pl.BlockSpec(w.shape, lambda t: (0, 0)),
        ],
        out_specs=out_specs,
        out_shape=out_shape,
        compiler_params=pltpu.CompilerParams(vmem_limit_bytes=VMEM_LIMIT),
    )(xs, mods_l, mods_l, cos_t, sin_t, w)


def _attn_a_kernel(sink_ref, q_ref, k_ref, vt_ref, kc_ref, vct_ref, o_ref, *, n_blocks, n_lat):
    n = pl.program_id(1)
    blk = A_BLOCK
    win = 3 * blk
    start = pl.multiple_of(jnp.clip((n - 1) * blk, 0, n_lat - win), blk)
    keys = jnp.concatenate([k_ref[pl.ds(start, win), :], kc_ref[...]], axis=0)
    vals_t = jnp.concatenate([vt_ref[:, pl.ds(start, win)], vct_ref[...]], axis=1)
    nk = keys.shape[0]
    qbase = jnp.where(n < n_blocks, n * blk, -(1 << 20))
    kidx = lax.broadcasted_iota(jnp.int32, (nk, 2 * blk), 0)
    qidx = lax.broadcasted_iota(jnp.int32, (nk, 2 * blk), 1)
    qpos = qbase + jnp.where(qidx >= blk, qidx - blk, qidx)
    in_band = jnp.abs(start + kidx - qpos) <= A_WINDOW
    bias = jnp.where((kidx >= win) | in_band, 0.0, MASK_VALUE).astype(F32)
    lo = _lane_iota((blk, LANES)) < HEAD_DIM
    zero = jnp.zeros((blk, LANES), BF16)
    col_hi = _lane_iota((1, 2 * blk)) >= blk
    dn = (((1,), (1,)), ((), ()))
    scores = []
    for j in range(A_HEADS // 2):
        qc = q_ref[:, j * LANES:(j + 1) * LANES]
        qq = jnp.concatenate([jnp.where(lo, qc, zero), jnp.where(lo, zero, qc)], axis=0)
        scores.append(lax.dot_general(keys, qq, dn, preferred_element_type=F32) + bias)
    for j, s in enumerate(scores):
        sink = jnp.where(col_hi, sink_ref[j + A_HEADS // 2], sink_ref[j])
        m = jnp.maximum(jnp.max(s, axis=0, keepdims=True), sink)
        p = jnp.exp2(s - m)
        l = jnp.sum(p, axis=0, keepdims=True) + jnp.exp2(sink - m)
        ot = jnp.dot(vals_t, p.astype(BF16), preferred_element_type=F32) * (1.0 / l)
        o = jnp.concatenate([ot[:HEAD_DIM, :blk], ot[HEAD_DIM:, blk:]], axis=0)
        o_ref[:, j * LANES:(j + 1) * LANES] = o.T.astype(BF16)


def _attn_a(q, k, vt, sink, n_batch, n_lat, n_ctx, with_ctx):
    t_rows = q.shape[0]
    blk = A_BLOCK
    nb = n_lat // blk
    ncb = n_ctx // blk if with_ctx else 0
    qrow = lambda b, n: jnp.where(n < nb, b * nb + n, n_batch * nb + b * (n_ctx // blk) + (n - nb))
    ctx_blk = lambda b: (n_batch * n_lat) // n_ctx + b
    grid_spec = pltpu.PrefetchScalarGridSpec(
        num_scalar_prefetch=1,
        grid=(n_batch, nb + ncb),
        in_specs=[
            pl.BlockSpec((blk, A_Q), lambda b, n, s: (qrow(b, n), 0)),
            pl.BlockSpec((n_lat, A_KV), lambda b, n, s: (b, 0)),
            pl.BlockSpec((A_KV, n_lat), lambda b, n, s: (0, b)),
            pl.BlockSpec((n_ctx, A_KV), lambda b, n, s: (ctx_blk(b), 0)),
            pl.BlockSpec((A_KV, n_ctx), lambda b, n, s: (0, ctx_blk(b))),
        ],
        out_specs=pl.BlockSpec((blk, A_Q), lambda b, n, s: (qrow(b, n), 0)),
    )
    return pl.pallas_call(
        functools.partial(_attn_a_kernel, n_blocks=nb, n_lat=n_lat),
        grid_spec=grid_spec,
        out_shape=jax.ShapeDtypeStruct((t_rows if with_ctx else n_batch * n_lat, A_Q), BF16),
        compiler_params=pltpu.CompilerParams(vmem_limit_bytes=VMEM_LIMIT),
    )(sink, q, k, vt, k, vt)


def _attn_b_kernel(q_ref, k_ref, vt_ref, kc_ref, vct_ref, bias_ref, o_ref, *, rows, n_steps):
    r = pl.program_id(2)
    nq = B_QROWS * GRID_W
    nk = B_KROWS * GRID_W
    r0 = jnp.where(r < n_steps, r * B_QROWS, 0)
    ws = jnp.clip(r0 - NA_KH // 2, 0, rows - B_KROWS)
    start = pl.multiple_of(ws * GRID_W, LANES)
    lo = _lane_iota((nq, LANES)) < HEAD_DIM
    zero = jnp.zeros((nq, LANES), BF16)
    dn = (((1,), (1,)), ((), ()))
    scores = []
    for g in range(B_PAIRS):
        cols = slice(g * LANES, (g + 1) * LANES)
        q = q_ref[:, cols]
        qq = jnp.concatenate([jnp.where(lo, q, zero), jnp.where(lo, zero, q)], axis=0)
        s_loc = lax.dot_general(k_ref[pl.ds(start, nk), cols], qq, dn,
                                preferred_element_type=F32) + bias_ref[g]
        s_ctx = lax.dot_general(kc_ref[:, cols], qq, dn, preferred_element_type=F32)
        scores.append((s_loc, s_ctx))
    for g, (s_loc, s_ctx) in enumerate(scores):
        cols = slice(g * LANES, (g + 1) * LANES)
        m = jnp.maximum(jnp.max(s_loc, axis=0, keepdims=True), jnp.max(s_ctx, axis=0, keepdims=True))
        p_loc = jnp.exp2(s_loc - m)
        p_ctx = jnp.exp2(s_ctx - m)
        l = jnp.sum(p_loc, axis=0, keepdims=True) + jnp.sum(p_ctx, axis=0, keepdims=True)
        ot = (jnp.dot(vt_ref[cols, pl.ds(start, nk)], p_loc.astype(BF16), preferred_element_type=F32)
              + jnp.dot(vct_ref[cols, :], p_ctx.astype(BF16), preferred_element_type=F32)) * (1.0 / l)
        o = jnp.concatenate([ot[:HEAD_DIM, :nq], ot[HEAD_DIM:, nq:]], axis=0)
        o_ref[:, cols] = o.T.astype(BF16)


def _nbr_bias_table(rpb, rows):
    n_steps = rows // B_QROWS
    steps = [min(2, n_steps - 1), 0, 1, n_steps - 2, n_steps - 1]
    w = GRID_W
    n_heads = rpb.shape[0]
    c = np.arange(w)[None, :]
    kc = np.arange(w)[:, None]
    cs = np.clip(c - NA_KW // 2, 0, w - NA_KW)
    col_ok = (kc >= cs) & (kc < cs + NA_KW)
    onehot = (((kc - c + NA_KW - 1)[None] == np.arange(2 * NA_KW - 1)[:, None, None]) & col_ok[None])
    toe = jnp.einsum("hrd,dkc->hrkc", rpb.astype(F32), jnp.asarray(onehot, F32),
                     precision=lax.Precision.HIGHEST)
    toe = jnp.where(col_ok[None, None], toe * math.log2(math.e), MASK_VALUE)
    masked = jnp.full((n_heads, w, w), MASK_VALUE, F32)
    tabs = []
    for st in steps:
        r0 = st * B_QROWS
        ws = int(np.clip(r0 - NA_KH // 2, 0, rows - B_KROWS))
        q_cols = []
        for rq in range(B_QROWS):
            r = r0 + rq
            rs = int(np.clip(r - NA_KH // 2, 0, rows - NA_KH))
            blocks = [toe[:, ws + ki - r + NA_KH - 1] if rs <= ws + ki < rs + NA_KH else masked
                      for ki in range(B_KROWS)]
            q_cols.append(jnp.concatenate(blocks, axis=1))
        tabs.append(jnp.concatenate(q_cols, axis=2))
    tabs.append(jnp.full_like(tabs[0], MASK_VALUE))
    tab = jnp.stack(tabs)
    n_var, _, nk, nq = tab.shape
    tab = tab.reshape(n_var, n_heads // 2, 2, nk, nq)
    return jnp.transpose(tab, (0, 1, 3, 2, 4)).reshape(n_var, n_heads // 2, nk, 2 * nq)


def _attn_b(q, k, vt, bias_tab, n_batch, n_lat, n_ctx, with_ctx):
    t_rows = q.shape[0]
    rows = n_lat // GRID_W
    nq = B_QROWS * GRID_W
    nk = B_KROWS * GRID_W
    gw = B_PAIRS * LANES
    n_steps = rows // B_QROWS
    n_cstep = n_ctx // nq if with_ctx else 0
    qrow = lambda b, r: jnp.where(r < n_steps, b * n_steps + r,
                                  n_batch * n_steps + b * (n_ctx // nq) + (r - n_steps))
    ctx_blk = lambda b: (n_batch * n_lat) // n_ctx + b

    def variant(r):
        v = jnp.where(r == 0, 1, 0)
        v = jnp.where(r == 1, 2, v)
        v = jnp.where(r == n_steps - 2, 3, v)
        v = jnp.where(r == n_steps - 1, 4, v)
        return jnp.where(r >= n_steps, 5, v)

    return pl.pallas_call(
        functools.partial(_attn_b_kernel, rows=rows, n_steps=n_steps),
        grid=(n_batch, B_HEADS // (2 * B_PAIRS), n_steps + n_cstep),
        in_specs=[
            pl.BlockSpec((nq, gw), lambda b, j, r: (qrow(b, r), j)),
            pl.BlockSpec((n_lat, gw), lambda b, j, r: (b, j)),
            pl.BlockSpec((gw, n_lat), lambda b, j, r: (j, b)),
            pl.BlockSpec((n_ctx, gw), lambda b, j, r: (ctx_blk(b), j)),
            pl.BlockSpec((gw, n_ctx), lambda b, j, r: (j, ctx_blk(b))),
            pl.BlockSpec((None, B_PAIRS, nk, 2 * nq), lambda b, j, r: (variant(r), j, 0, 0)),
        ],
        out_specs=pl.BlockSpec((nq, gw), lambda b, j, r: (qrow(b, r), j)),
        out_shape=jax.ShapeDtypeStruct((t_rows if with_ctx else n_batch * n_lat, B_W), BF16),
        compiler_params=pltpu.CompilerParams(vmem_limit_bytes=VMEM_LIMIT),
    )(q, k, vt, k, vt, bias_tab)


def _attn_c_kernel(*refs, latent, n_kchunks, lam_init):
    if latent:
        (lam_ref, q_ref, k_ref, vt_ref, kc_ref, vct_ref, g_ref, o_ref,
         s_a, mx_a, s_b, mx_b, s_c, mx_c, m_scr, l_scr, acc_scr) = refs
        slots = ((s_a, mx_a), (s_b, mx_b))
    else:
        lam_ref, q_ref, kc_ref, vct_ref, g_ref, o_prev_ref, o_ref, s_c, mx_c, m_scr, l_scr, acc_scr = refs
    slot_c = (s_c, mx_c)
    tq = q_ref.shape[0]
    tk = C_KCHUNK
    lo = _lane_iota((tq, LANES)) < HEAD_DIM
    zero = jnp.zeros((tq, LANES), BF16)
    q_maps = []
    for hh in range(C_STEP_HEADS):
        q = q_ref[:, hh * LANES:(hh + 1) * LANES]
        q_maps += [jnp.where(lo, q, zero), jnp.where(lo, zero, q)]
    n_chains = len(q_maps)
    dn = (((1,), (1,)), ((), ()))

    def head_cols(ci):
        return slice((ci // 2) * LANES, (ci // 2 + 1) * LANES)

    def scores(kk_of, slot):
        s_ref, mx_ref = slot
        for ci in range(n_chains):
            s = lax.dot_general(kk_of(head_cols(ci)), q_maps[ci], dn, preferred_element_type=F32)
            s_ref[ci] = s
            mx_ref[ci] = jnp.max(s, axis=0, keepdims=True)

    def softmax_pv(slot, vvt_of):
        s_ref, mx_ref = slot
        for ci in range(n_chains):
            s = s_ref[ci]
            m = m_scr[ci]
            m_new = jnp.maximum(m, mx_ref[ci])
            alpha = jnp.exp2(m - m_new)
            p = jnp.exp2(s - m_new)
            l_scr[ci] = alpha * l_scr[ci] + jnp.sum(p, axis=0, keepdims=True)
            acc_scr[ci] = alpha * acc_scr[ci] + jnp.dot(vvt_of(head_cols(ci)), p.astype(BF16),
                                                        preferred_element_type=F32)
            m_scr[ci] = m_new

    def chunk_slice(c):
        return pl.ds(c * tk if isinstance(c, int) else pl.multiple_of(c * tk, tk), tk)

    def k_chunk(c):
        return lambda cols: k_ref[chunk_slice(c), cols]

    def vt_chunk(c):
        return lambda cols: vt_ref[cols, chunk_slice(c)]

    k_ctx = lambda cols: kc_ref[:, cols]
    vt_ctx = lambda cols: vct_ref[cols, :]

    m_scr[...] = jnp.full(m_scr.shape, -jnp.inf, F32)
    l_scr[...] = jnp.zeros(l_scr.shape, F32)
    acc_scr[...] = jnp.zeros(acc_scr.shape, F32)

    def finish():
        softmax_pv(slot_c, vt_ctx)
        lp = lam_ref[...]
        lam = (jnp.exp(jnp.sum(lp[0:1] * lp[1:2], axis=-1, keepdims=True))
               - jnp.exp(jnp.sum(lp[2:3] * lp[3:4], axis=-1, keepdims=True)) + lam_init)
        for hh in range(C_STEP_HEADS):
            c1, c2 = 2 * hh, 2 * hh + 1
            od = acc_scr[c1] * (1.0 / l_scr[c1]) - lam * (acc_scr[c2] * (1.0 / l_scr[c2]))
            ms = jnp.mean(od * od, axis=0, keepdims=True)
            on = (od * lax.rsqrt(ms + LN_EPS)).T
            o_ref[:, hh * LANES:(hh + 1) * LANES] = ((on * g_ref[...]) * (1.0 - lam_init)).astype(BF16)

    if not latent:
        scores(k_ctx, slot_c)
        finish()
        return

    scores(k_chunk(0), slots[0])

    def body(it, carry):
        c0 = C_UNROLL * it
        for u in range(C_UNROLL):
            scores(k_chunk(c0 + u + 1), slots[(u + 1) % 2])
            softmax_pv(slots[u % 2], vt_chunk(c0 + u))
        return carry

    n_it = (n_kchunks - 1) // C_UNROLL
    lax.fori_loop(0, n_it, body, 0)
    for c in range(C_UNROLL * n_it, n_kchunks):
        if c + 1 < n_kchunks:
            scores(k_chunk(c + 1), slots[(c + 1) % 2])
        else:
            scores(k_ctx, slot_c)
        softmax_pv(slots[c % 2], vt_chunk(c))
    finish()


def _attn_c(q, k, vt, lam_p, subln, lam_init, n_batch, n_lat, n_ctx, with_ctx):
    t_rows = q.shape[0]
    hw = C_STEP_HEADS * LANES
    nch = 2 * C_STEP_HEADS
    ctx_blk = lambda b: (n_batch * n_lat) // n_ctx + b
    out_shape = jax.ShapeDtypeStruct((t_rows if with_ctx else n_batch * n_lat, C_OUT), BF16)
    params = pltpu.CompilerParams(vmem_limit_bytes=VMEM_LIMIT)
    stats = lambda tq: [pltpu.VMEM((nch, 1, tq), F32), pltpu.VMEM((nch, 1, tq), F32),
                        pltpu.VMEM((nch, LANES, tq), F32)]
    slot = lambda nk, tq: [pltpu.VMEM((nch, nk, tq), F32), pltpu.VMEM((nch, 1, tq), F32)]
    lam_spec = pl.BlockSpec((4, HEAD_DIM), lambda b, h, i: (0, 0))
    kc_spec = pl.BlockSpec((n_ctx, hw), lambda b, h, i: (ctx_blk(b), h))
    vct_spec = pl.BlockSpec((hw, n_ctx), lambda b, h, i: (h, ctx_blk(b)))
    g_spec = pl.BlockSpec((1, LANES), lambda b, h, i: (0, 0))
    g = subln.reshape(1, LANES)

    tq = C_QTILE
    n_qlat = n_lat // tq
    o = pl.pallas_call(
        functools.partial(_attn_c_kernel, latent=True, n_kchunks=n_lat // C_KCHUNK, lam_init=lam_init),
        grid=(n_batch, C_HEADS // C_STEP_HEADS, n_qlat),
        in_specs=[
            lam_spec,
            pl.BlockSpec((tq, hw), lambda b, h, i: (b * n_qlat + i, h)),
            pl.BlockSpec((n_lat, hw), lambda b, h, i: (b, h)),
            pl.BlockSpec((hw, n_lat), lambda b, h, i: (h, b)),
            kc_spec, vct_spec, g_spec,
        ],
        out_specs=pl.BlockSpec((tq, hw), lambda b, h, i: (b * n_qlat + i, h)),
        out_shape=out_shape,
        scratch_shapes=(slot(C_KCHUNK, tq) + slot(C_KCHUNK, tq) + slot(n_ctx, tq) + stats(tq)),
        compiler_params=params,
    )(lam_p, q, k, vt, k, vt, g)
    if not with_ctx:
        return o
    return pl.pallas_call(
        functools.partial(_attn_c_kernel, latent=False, n_kchunks=0, lam_init=lam_init),
        grid=(n_batch, C_HEADS // C_STEP_HEADS, 1),
        in_specs=[
            lam_spec,
            pl.BlockSpec((n_ctx, hw), lambda b, h, i: (ctx_blk(b), h)),
            kc_spec, vct_spec, g_spec,
            pl.BlockSpec(memory_space=pl.ANY),
        ],
        out_specs=pl.BlockSpec((n_ctx, hw), lambda b, h, i: (ctx_blk(b), h)),
        out_shape=out_shape,
        scratch_shapes=slot(n_ctx, n_ctx) + stats(n_ctx),
        input_output_aliases={5: 0},
        compiler_params=params,
    )(lam_p, q, k, vt, g, o)


def _bf16_truncate(x):
    bits = lax.bitcast_convert_type(x, jnp.uint32) & jnp.uint32(0xFFFF0000)
    return lax.bitcast_convert_type(bits, F32)


def _top2_router(h, wr_ref):
    h_top = _bf16_truncate(h)
    h_hi = h_top.astype(BF16)
    h_lo = (h - h_top).astype(BF16)
    a = jnp.dot(h_hi, wr_ref[...], preferred_element_type=F32)
    b = jnp.dot(h_lo, wr_ref[:, :LANES], preferred_element_type=F32)
    lg = a[:, :LANES] + a[:, LANES:] + b
    lane = _lane_iota(lg.shape)
    lanef = lane.astype(F32)
    lg = jnp.where(lane < N_EXPERTS, lg, -jnp.inf)
    m1 = jnp.max(lg, axis=-1, keepdims=True)
    i1 = jnp.min(jnp.where(lg == m1, lanef, float(LANES)), axis=-1, keepdims=True)
    lg2 = jnp.where(lanef == i1, -jnp.inf, lg)
    m2 = jnp.max(lg2, axis=-1, keepdims=True)
    i2 = jnp.min(jnp.where(lg2 == m2, lanef, float(LANES)), axis=-1, keepdims=True)
    e = jnp.exp(m2 - m1)
    w1 = 1.0 / (1.0 + e)
    w2 = e / (1.0 + e)
    out = jnp.where(lane == 0, i1, 0.0)
    out = jnp.where(lane == 1, i2, out)
    out = jnp.where(lane == 2, w1, out)
    return jnp.where(lane == 3, w2, out)


def _pack_bf16_pairs(h):
    bits = lax.bitcast_convert_type(h, jnp.uint32)
    bits = bits + (jnp.uint32(0x7FFF) + ((bits >> 16) & jnp.uint32(1)))
    half = h.shape[1] // 2
    packed = (bits[:, half:] & jnp.uint32(0xFFFF0000)) | (bits[:, :half] >> 16)
    return lax.bitcast_convert_type(packed, F32)


def _unpack_bf16_pairs(words):
    p = lax.bitcast_convert_type(words, jnp.uint32)
    lo = lax.bitcast_convert_type(p << 16, F32)
    hi = lax.bitcast_convert_type(p & jnp.uint32(0xFFFF0000), F32)
    return jnp.concatenate([lo, hi], axis=1)


def _outproj_kernel(*refs, n_in, router):
    o_refs = refs[:n_in]
    w_refs = refs[n_in:2 * n_in]
    x_ref, g_ref, lng_ref, lnb_ref, sc_ref, sh_ref = refs[2 * n_in:2 * n_in + 6]
    rest = refs[2 * n_in + 6:]
    y = jnp.dot(o_refs[0][...], w_refs[0][...], preferred_element_type=F32)
    for o_r, w_r in zip(o_refs[1:], w_refs[1:]):
        y = y + jnp.dot(o_r[...], w_r[...], preferred_element_type=F32)
    xn = _layernorm_rows(DN_ALPHA * x_ref[...] + g_ref[...] * y, lng_ref[...], lnb_ref[...])
    h2 = xn * (1.0 + sc_ref[...]) + sh_ref[...]
    if router:
        wr_ref, xo_ref, h_ref, r_ref = rest
        r_ref[...] = _top2_router(h2, wr_ref)
        h_ref[...] = _pack_bf16_pairs(h2)
    else:
        xo_ref, h_ref = rest
        h_ref[...] = h2.astype(BF16)
    xo_ref[...] = xn


def _out_proj(o_list, w_list, xs, mods_l, ln_g, ln_b, n_tiles, tiles_per_batch, n_batch, w_router=None):
    d = xs.shape[1]
    tm = ROW_TILE
    t_rows = n_tiles * tm
    grp = lambda t: jnp.minimum(t // tiles_per_batch, n_batch)
    mod = lambda k: pl.BlockSpec((None, None, 1, d), lambda t: (grp(t), k, 0, 0))
    in_specs = [pl.BlockSpec((tm, o.shape[1]), lambda t: (t, 0)) for o in o_list]
    in_specs += [pl.BlockSpec(w.shape, lambda t: (0, 0)) for w in w_list]
    in_specs += [pl.BlockSpec((tm, d), lambda t: (t, 0)), mod(2),
                 pl.BlockSpec((1, d), lambda t: (0, 0)), pl.BlockSpec((1, d), lambda t: (0, 0)),
                 mod(4), mod(3)]
    args = list(o_list) + list(w_list) + [xs, mods_l, ln_g.reshape(1, d), ln_b.reshape(1, d), mods_l, mods_l]
    out_shape = [jax.ShapeDtypeStruct((t_rows, d), F32), jax.ShapeDtypeStruct((t_rows, d), BF16)]
    out_specs = [pl.BlockSpec((tm, d), lambda t: (t, 0)), pl.BlockSpec((tm, d), lambda t: (t, 0))]
    if w_router is not None:
        out_shape[1] = jax.ShapeDtypeStruct((t_rows, d // 2), F32)
        out_specs[1] = pl.BlockSpec((tm, d // 2), lambda t: (t, 0))
        in_specs.append(pl.BlockSpec(w_router.shape, lambda t: (0, 0)))
        args.append(w_router)
        out_shape.append(jax.ShapeDtypeStruct((t_rows, LANES), F32))
        out_specs.append(pl.BlockSpec((tm, LANES), lambda t: (t, 0)))
    return pl.pallas_call(
        functools.partial(_outproj_kernel, n_in=len(o_list), router=w_router is not None),
        grid=(n_tiles,),
        in_specs=in_specs,
        out_specs=out_specs,
        out_shape=out_shape,
        compiler_params=pltpu.CompilerParams(vmem_limit_bytes=VMEM_LIMIT),
    )(*args)


def _swiglu_acc(h, wg_ref, wu_ref, wd_ref, acc_ref, c):
    g = jnp.dot(h, wg_ref[...], preferred_element_type=F32)
    u = jnp.dot(h, wu_ref[...], preferred_element_type=F32)
    a = (g * (1.0 / (1.0 + jnp.exp(-g))) * u).astype(BF16)
    part = jnp.dot(a, wd_ref[...], preferred_element_type=F32)

    @pl.when(c == 0)
    def _():
        acc_ref[...] = part

    @pl.when(c > 0)
    def _():
        acc_ref[...] += part


def _ffn_dense_kernel(h_ref, wg_ref, wu_ref, wd_ref, x_ref, g_ref, lng_ref, lnb_ref, xo_ref, acc_ref):
    c = pl.program_id(1)
    _swiglu_acc(h_ref[...], wg_ref, wu_ref, wd_ref, acc_ref, c)

    @pl.when(c == pl.num_programs(1) - 1)
    def _():
        z = DN_ALPHA * x_ref[...] + g_ref[...] * acc_ref[...]
        xo_ref[...] = _layernorm_rows(z, lng_ref[...], lnb_ref[...])


def _ffn_dense(h2, w_up, w_down, xs, mods_l, ln_g, ln_b, n_tiles, tiles_per_batch, n_batch):
    d = xs.shape[1]
    tm = ROW_TILE
    t_rows = n_tiles * tm
    ff = w_down.shape[0]
    nc = ff // FF_CHUNK
    grp = lambda t: jnp.minimum(t // tiles_per_batch, n_batch)
    return pl.pallas_call(
        _ffn_dense_kernel,
        grid=(n_tiles, nc),
        in_specs=[
            pl.BlockSpec((tm, d), lambda t, c: (t, 0)),
            pl.BlockSpec((d, FF_CHUNK), lambda t, c: (0, c)),
            pl.BlockSpec((d, FF_CHUNK), lambda t, c: (0, nc + c)),
            pl.BlockSpec((FF_CHUNK, d), lambda t, c: (c, 0)),
            pl.BlockSpec((tm, d), lambda t, c: (t, 0)),
            pl.BlockSpec((None, None, 1, d), lambda t, c: (grp(t), 5, 0, 0)),
            pl.BlockSpec((1, d), lambda t, c: (0, 0)),
            pl.BlockSpec((1, d), lambda t, c: (0, 0)),
        ],
        out_specs=pl.BlockSpec((tm, d), lambda t, c: (t, 0)),
        out_shape=jax.ShapeDtypeStruct((t_rows, d), F32),
        scratch_shapes=[pltpu.VMEM((tm, d), F32)],
        compiler_params=pltpu.CompilerParams(vmem_limit_bytes=VMEM_LIMIT),
    )(h2, w_up, w_up, w_down, xs, mods_l, ln_g.reshape(1, d), ln_b.reshape(1, d))


def _ffn_expert_kernel(te_ref, nu_ref, h_ref, wg_ref, wu_ref, wd_ref, y_ref, acc_ref):
    i = pl.program_id(0)
    c = pl.program_id(1)

    @pl.when(i < nu_ref[0])
    def _():
        _swiglu_acc(_unpack_bf16_pairs(h_ref[...]).astype(BF16), wg_ref, wu_ref, wd_ref, acc_ref, c)

    @pl.when(c == pl.num_programs(1) - 1)
    def _():
        y_ref[...] = _pack_bf16_pairs(acc_ref[...])


def _ffn_experts(hs, w_up, w_down, layer, tile_expert, n_used):
    p_rows = hs.shape[0]
    d = w_down.shape[-1]
    tm = MOE_TILE
    n_tiles = p_rows // tm
    ef = w_down.shape[2]
    nc = ef // EXPERT_FF_CHUNK
    chunk = lambda i, c, nu: jnp.where(i < nu[0], c, nc - 1)
    grid_spec = pltpu.PrefetchScalarGridSpec(
        num_scalar_prefetch=2,
        grid=(n_tiles, nc),
        in_specs=[
            pl.BlockSpec((tm, d // 2), lambda i, c, te, nu: (i, 0)),
            pl.BlockSpec((None, None, d, EXPERT_FF_CHUNK),
                         lambda i, c, te, nu: (layer, te[i], 0, chunk(i, c, nu))),
            pl.BlockSpec((None, None, d, EXPERT_FF_CHUNK),
                         lambda i, c, te, nu: (layer, te[i], 0, nc + chunk(i, c, nu))),
            pl.BlockSpec((None, None, EXPERT_FF_CHUNK, d),
                         lambda i, c, te, nu: (layer, te[i], chunk(i, c, nu), 0)),
        ],
        out_specs=pl.BlockSpec((tm, d // 2), lambda i, c, te, nu: (i, 0)),
        scratch_shapes=[pltpu.VMEM((tm, d), F32)],
    )
    return pl.pallas_call(
        _ffn_expert_kernel,
        grid_spec=grid_spec,
        out_shape=jax.ShapeDtypeStruct((p_rows, d // 2), F32),
        compiler_params=pltpu.CompilerParams(vmem_limit_bytes=VMEM_LIMIT),
    )(tile_expert, n_used, hs, w_up, w_up, w_down)


def _combine_kernel(y0_ref, y1_ref, r_ref, x_ref, g_ref, lng_ref, lnb_ref, xo_ref):
    r = r_ref[...]
    f = r[:, 2:3] * _unpack_bf16_pairs(y0_ref[...]) + r[:, 3:4] * _unpack_bf16_pairs(y1_ref[...])
    z = DN_ALPHA * x_ref[...] + g_ref[...] * f
    xo_ref[...] = _layernorm_rows(z, lng_ref[...], lnb_ref[...])


def _moe_combine(y0, y1, route, xs, mods_l, ln_g, ln_b, n_tiles, tiles_per_batch, n_batch):
    d = xs.shape[1]
    tm = ROW_TILE
    t_rows = n_tiles * tm
    grp = lambda t: jnp.minimum(t // tiles_per_batch, n_batch)
    row = pl.BlockSpec((tm, d), lambda t: (t, 0))
    packed = pl.BlockSpec((tm, d // 2), lambda t: (t, 0))
    return pl.pallas_call(
        _combine_kernel,
        grid=(n_tiles,),
        in_specs=[packed, packed, pl.BlockSpec((tm, LANES), lambda t: (t, 0)), row,
                  pl.BlockSpec((None, None, 1, d), lambda t: (grp(t), 5, 0, 0)),
                  pl.BlockSpec((1, d), lambda t: (0, 0)), pl.BlockSpec((1, d), lambda t: (0, 0))],
        out_specs=row,
        out_shape=jax.ShapeDtypeStruct((t_rows, d), F32),
        compiler_params=pltpu.CompilerParams(vmem_limit_bytes=VMEM_LIMIT),
    )(y0, y1, route, xs, mods_l, ln_g.reshape(1, d), ln_b.reshape(1, d))


def _routing_plan(route, n_rows):
    tm = MOE_TILE
    e_idx = route[:n_rows, 0:2].astype(jnp.int32).reshape(-1)
    onehot = (e_idx[:, None] == jnp.arange(N_EXPERTS, dtype=jnp.int32)[None, :]).astype(jnp.int32)
    csum = jnp.cumsum(onehot, axis=0)
    counts = csum[-1]
    rank = jnp.sum((csum - onehot) * onehot, axis=1)
    padded = ((counts + tm - 1) // tm) * tm
    ends = jnp.cumsum(padded)
    starts = ends - padded
    dest = starts[e_idx] + rank
    n_tiles = (2 * n_rows + N_EXPERTS * (tm - 1)) // tm
    p_rows = n_tiles * tm
    row_token = jnp.zeros((p_rows,), jnp.int32).at[dest].set(jnp.arange(2 * n_rows, dtype=jnp.int32) // 2)
    tile_start = jnp.arange(n_tiles, dtype=jnp.int32) * tm
    tile_expert = jnp.minimum(jnp.sum((tile_start[:, None] >= ends[None, :]).astype(jnp.int32), axis=1),
                              N_EXPERTS - 1)
    n_used = (ends[-1] // tm).astype(jnp.int32).reshape(1)
    last = tile_expert[jnp.maximum(n_used[0] - 1, 0)]
    tile_expert = jnp.where(jnp.arange(n_tiles) < n_used[0], tile_expert, last).astype(jnp.int32)
    return row_token, dest.reshape(n_rows, 2), tile_expert, n_used


def _rope_tables(n_lat):
    t = jnp.arange(n_lat, dtype=jnp.int32)
    row = (t // GRID_W).astype(F32)
    col = (t % GRID_W).astype(F32)
    inv = ROPE_THETA ** (-jnp.arange(0, ROPE_AXIS_DIM, 2, dtype=F32) / ROPE_AXIS_DIM)
    ar = row[:, None] * inv[None, :]
    ac = col[:, None] * inv[None, :]
    ang = jnp.concatenate([ar, ar, ac, ac], axis=-1)
    cos = jnp.tile(jnp.cos(ang), (1, LANES // HEAD_DIM))
    sin = jnp.tile(jnp.sin(ang), (1, LANES // HEAD_DIM))
    sign = jnp.where((jnp.arange(LANES) % 32) < 16, -1.0, 1.0).astype(F32)
    cos = jnp.concatenate([cos, jnp.ones((ROW_TILE, LANES), F32)], axis=0)
    sin = jnp.concatenate([sin * sign[None, :], jnp.zeros((ROW_TILE, LANES), F32)], axis=0)
    return cos, sin


def _lambda_init(layer):
    return 0.8 - 0.6 * math.exp(-0.3 * layer)


def kernel(x, c, ctx, c_ctx, w_mod, b_mod, ln_g, ln_b, w_in_ab, w_out_ab, sink_a, rpb_b, w_in_c, w_out_c,
           lam_c, subln_c, w_ffn_up, w_ffn_down, w_router, w_exp_up, w_exp_down):
    n_batch, n_lat, d = x.shape
    n_ctx = ctx.shape[1]
    assert d == D_MODEL and n_batch * n_ctx == ROW_TILE and n_lat % ROW_TILE == 0
    assert n_batch + 1 <= 8 and n_lat % C_KCHUNK == 0 and n_lat % C_QTILE == 0 and n_ctx % LANES == 0
    rows = n_lat // GRID_W
    assert rows >= 12 and rows % B_QROWS == 0
    tiles_per_batch = n_lat // ROW_TILE
    lat_tiles = n_batch * tiles_per_batch
    lat_rows = n_batch * n_lat

    xs = jnp.concatenate([x.reshape(lat_rows, d), ctx.reshape(n_batch * n_ctx, d)], axis=0)
    cond = jnp.zeros((8, d), F32).at[:n_batch].set(c).at[n_batch].set(c_ctx)
    mods = _mod_vectors(cond, w_mod, b_mod).reshape(DEPTH, 8, 6, 1, d)
    cos_t, sin_t = _rope_tables(n_lat)
    w_exp_up_b = w_exp_up.astype(BF16)
    w_exp_down_b = w_exp_down.astype(BF16)

    perm = np.array([(j + (A_HEADS // 2) * half) * HEAD_DIM + dd
                     for j in range(A_HEADS // 2) for half in range(2) for dd in range(HEAD_DIM)])

    for l in range(DEPTH):
        last = l == DEPTH - 1
        i = l // 2
        all_tiles = lat_tiles + 1
        n_tiles = lat_tiles if last else all_tiles
        mods_l = mods[l]
        if l % 2 == 0:
            w_in = w_in_ab[i]
            w_in = jnp.concatenate([w_in[:, :A_Q][:, perm], w_in[:, A_Q:]], axis=1).astype(BF16)
            q_scale = ATTN_SCALE * math.log2(math.e)
            segs = [(A_Q, True, q_scale, False), (A_KV, True, 1.0, False), (A_KV, False, 1.0, True),
                    (B_W, False, q_scale, False), (B_W, False, 1.0, False), (B_W, False, 1.0, True)]
            qa, ka, va, qb, kb, vb = _in_proj(xs, mods_l, cos_t, sin_t, w_in, segs, all_tiles,
                                              tiles_per_batch, n_batch)
            sink = sink_a[i].astype(F32) * math.log2(math.e)
            oa = _attn_a(qa, ka, va, sink, n_batch, n_lat, n_ctx, not last)
            ob = _attn_b(qb, kb, vb, _nbr_bias_table(rpb_b[i], rows), n_batch, n_lat, n_ctx, not last)
            w_out = w_out_ab[i]
            o_list = [oa, ob]
            w_list = [w_out[:A_Q][perm].astype(BF16), w_out[A_Q:].astype(BF16)]
        else:
            segs = [(C_QK, True, ATTN_SCALE * math.log2(math.e), False), (C_QK, True, 1.0, False),
                    (C_OUT, False, 1.0, True)]
            qc, kc, vc = _in_proj(xs, mods_l, cos_t, sin_t, w_in_c[i].astype(BF16), segs, all_tiles,
                                  tiles_per_batch, n_batch)
            oc = _attn_c(qc, kc, vc, lam_c[i].astype(F32), subln_c[i].astype(F32), _lambda_init(l),
                         n_batch, n_lat, n_ctx, not last)
            o_list = [oc]
            w_list = [w_out_c[i].astype(BF16)]

        if l % 2 == 0:
            xs, h2 = _out_proj(o_list, w_list, xs, mods_l, ln_g[l, 0], ln_b[l, 0], n_tiles,
                               tiles_per_batch, n_batch)
            xs = _ffn_dense(h2, w_ffn_up[i].astype(BF16), w_ffn_down[i].astype(BF16), xs, mods_l,
                            ln_g[l, 1], ln_b[l, 1], n_tiles, tiles_per_batch, n_batch)
        else:
            wr = jnp.zeros((d, LANES), F32).at[:, :N_EXPERTS].set(w_router[i])
            wr_top = _bf16_truncate(wr)
            wr = jnp.concatenate([wr_top.astype(BF16), (wr - wr_top).astype(BF16)], axis=1)
            xs, h2, route = _out_proj(o_list, w_list, xs, mods_l, ln_g[l, 0], ln_b[l, 0], n_tiles,
                                      tiles_per_batch, n_batch, w_router=wr)
            n_rows = n_tiles * ROW_TILE
            row_token, dest, tile_expert, n_used = _routing_plan(route, n_rows)
            hs = jnp.take(h2, row_token, axis=0, mode="clip")
            ys = _ffn_experts(hs, w_exp_up_b, w_exp_down_b, i, tile_expert, n_used)
            y0 = jnp.take(ys, dest[:, 0], axis=0, mode="clip")
            y1 = jnp.take(ys, dest[:, 1], axis=0, mode="clip")
            xs = _moe_combine(y0, y1, route, xs, mods_l, ln_g[l, 1], ln_b[l, 1], n_tiles,
                              tiles_per_batch, n_batch)
    return xs[:lat_rows].reshape(n_batch, n_lat, d)
```

```python
import functools
import math

import jax
import jax.numpy as jnp
import numpy as np
from jax import lax
from jax.experimental import pallas as pl
from jax.experimental.pallas import tpu as pltpu

F32 = jnp.float32
BF16 = jnp.bfloat16

D_MODEL = 1024
DEPTH = 4
GRID_W = 64
HEAD_DIM = 64
LANES = 128
ATTN_SCALE = HEAD_DIM ** -0.5
A_BLOCK = 128
A_WINDOW = 128
A_HEADS = 8
A_KV_HEADS = 2
B_HEADS = 8
NA_KH = 8
NA_KW = 16
C_HEADS = 8
ROPE_THETA = 10000.0
ROPE_AXIS_DIM = HEAD_DIM // 2
FF_DIM = 2816
N_EXPERTS = 8
EXPERT_FF = 3584
LN_EPS = 1e-5
DN_ALPHA = (2 * DEPTH) ** 0.25
MASK_VALUE = -1e30
A_Q = A_HEADS * HEAD_DIM
A_KV = A_KV_HEADS * HEAD_DIM
B_W = B_HEADS * HEAD_DIM
C_QK = 1024
C_OUT = 1024

ROW_TILE = 512
MOE_TILE = 512
FF_CHUNK = 1408
EXPERT_FF_CHUNK = 1792
B_QROWS = 2
B_KROWS = B_QROWS + NA_KH
B_PAIRS = 4
C_QTILE = 256
C_KCHUNK = 512
C_UNROLL = 6
C_STEP_HEADS = 2
VMEM_LIMIT = 56 * 1024 * 1024


def _lane_iota(shape):
    return lax.broadcasted_iota(jnp.int32, shape, len(shape) - 1)


def _layernorm_rows(z, g, b):
    mu = jnp.mean(z, axis=-1, keepdims=True)
    zc = z - mu
    var = jnp.mean(zc * zc, axis=-1, keepdims=True)
    return zc * lax.rsqrt(var + LN_EPS) * g + b


def _mod_kernel(c_ref, w_ref, b_ref, o_ref):
    c = c_ref[...]
    s = c * (1.0 / (1.0 + jnp.exp(-c)))
    o_ref[0] = jnp.dot(s, w_ref[0], preferred_element_type=F32,
                       precision=lax.Precision.HIGHEST) + b_ref[0]


def _mod_vectors(cond, w_mod, b_mod):
    depth, d, n6 = w_mod.shape
    tn = 1536
    return pl.pallas_call(
        _mod_kernel,
        grid=(depth, n6 // tn),
        in_specs=[
            pl.BlockSpec((8, d), lambda l, j: (0, 0)),
            pl.BlockSpec((1, d, tn), lambda l, j: (l, 0, j)),
            pl.BlockSpec((1, 1, tn), lambda l, j: (l, 0, j)),
        ],
        out_specs=pl.BlockSpec((1, 8, tn), lambda l, j: (l, 0, j)),
        out_shape=jax.ShapeDtypeStruct((depth, 8, n6), F32),
        compiler_params=pltpu.CompilerParams(vmem_limit_bytes=VMEM_LIMIT),
    )(cond, w_mod, b_mod.reshape(depth, 1, n6))


def _rope_slab(a, cos, sin_signed, low16):
    fwd = pltpu.roll(a, LANES - 16, axis=1)
    bwd = pltpu.roll(a, 16, axis=1)
    return a * cos + jnp.where(low16, fwd, bwd) * sin_signed


def _proj_kernel(x_ref, sc_ref, sh_ref, cos_ref, sin_ref, w_ref, *out_refs, segments):
    x = x_ref[...]
    h = (x * (1.0 + sc_ref[...]) + sh_ref[...]).astype(BF16)
    cos = cos_ref[...]
    sin = sin_ref[...]
    low16 = (_lane_iota(cos.shape) % 32) < 16
    col = 0
    for o_ref, (width, rope, scale, transposed) in zip(out_refs, segments):
        for c0 in range(0, width, 256):
            cw = min(256, width - c0)
            acc = jnp.dot(h, w_ref[:, col + c0:col + c0 + cw], preferred_element_type=F32)
            if transposed:
                o_ref[c0:c0 + cw, :] = acc.T.astype(BF16)
                continue
            slabs = []
            for s0 in range(0, cw, LANES):
                a = acc[:, s0:s0 + LANES]
                if rope:
                    a = _rope_slab(a, cos, sin, low16)
                if scale != 1.0:
                    a = a * scale
                slabs.append(a.astype(BF16))
            o_ref[:, c0:c0 + cw] = slabs[0] if len(slabs) == 1 else jnp.concatenate(slabs, axis=1)
        col += width


def _in_proj(xs, mods_l, cos_t, sin_t, w, segments, n_tiles, tiles_per_batch, n_batch):
    t_rows, d = xs.shape
    tm = ROW_TILE
    grp = lambda t: jnp.minimum(t // tiles_per_batch, n_batch)
    pos = lambda t: jnp.where(t < n_batch * tiles_per_batch, t % tiles_per_batch, tiles_per_batch)
    out_shape = [jax.ShapeDtypeStruct((s[0], t_rows) if s[3] else (t_rows, s[0]), BF16) for s in segments]
    out_specs = [pl.BlockSpec((s[0], tm), lambda t: (0, t)) if s[3] else pl.BlockSpec((tm, s[0]), lambda t: (t, 0))
                 for s in segments]
    return pl.pallas_call(
        functools.partial(_proj_kernel, segments=tuple(segments)),
        grid=(n_tiles,),
        in_specs=[
            pl.BlockSpec((tm, d), lambda t: (t, 0)),
            pl.BlockSpec((None, None, 1, d), lambda t: (grp(t), 1, 0, 0)),
            pl.BlockSpec((None, None, 1, d), lambda t: (grp(t), 0, 0, 0)),
            pl.BlockSpec((tm, LANES), lambda t: (pos(t), 0)),
            pl.BlockSpec((tm, LANES), lambda t: (pos(t), 0)),
            pl.BlockSpec(w.shape, lambda t: (0, 0)),
        ],
        out_specs=out_specs,
        out_shape=out_shape,
        compiler_params=pltpu.CompilerParams(vmem_limit_bytes=VMEM_LIMIT),
    )(xs, mods_l, mods_l, cos_t, sin_t, w)


def _attn_a_kernel(sink_ref, q_ref, k_ref, vt_ref, kc_ref, vct_ref, o_ref, *, n_blocks, n_lat):
    n = pl.program_id(1)
    blk = A_BLOCK
    win = 3 * blk
    start = pl.multiple_of(jnp.clip((n - 1) * blk, 0, n_lat - win), blk)
    keys = jnp.concatenate([k_ref[pl.ds(start, win), :], kc_ref[...]], axis=0)
    vals_t = jnp.concatenate([vt_ref[:, pl.ds(start, win)], vct_ref[...]], axis=1)
    nk = keys.shape[0]
    qbase = jnp.where(n < n_blocks, n * blk, -(1 << 20))
    kidx = lax.broadcasted_iota(jnp.int32, (nk, 2 * blk), 0)
    qidx = lax.broadcasted_iota(jnp.int32, (nk, 2 * blk), 1)
    qpos = qbase + jnp.where(qidx >= blk, qidx - blk, qidx)
    in_band = jnp.abs(start + kidx - qpos) <= A_WINDOW
    bias = jnp.where((kidx >= win) | in_band, 0.0, MASK_VALUE).astype(F32)
    lo = _lane_iota((blk, LANES)) < HEAD_DIM
    zero = jnp.zeros((blk, LANES), BF16)
    col_hi = _lane_iota((1, 2 * blk)) >= blk
    dn = (((1,), (1,)), ((), ()))
    scores = []
    for j in range(A_HEADS // 2):
        qc = q_ref[:, j * LANES:(j + 1) * LANES]
        qq = jnp.concatenate([jnp.where(lo, qc, zero), jnp.where(lo, zero, qc)], axis=0)
        scores.append(lax.dot_general(keys, qq, dn, preferred_element_type=F32) + bias)
    for j, s in enumerate(scores):
        sink = jnp.where(col_hi, sink_ref[j + A_HEADS // 2], sink_ref[j])
        m = jnp.maximum(jnp.max(s, axis=0, keepdims=True), sink)
        p = jnp.exp2(s - m)
        l = jnp.sum(p, axis=0, keepdims=True) + jnp.exp2(sink - m)
        ot = jnp.dot(vals_t, p.astype(BF16), preferred_element_type=F32) * (1.0 / l)
        o = jnp.concatenate([ot[:HEAD_DIM, :blk], ot[HEAD_DIM:, blk:]], axis=0)
        o_ref[:, j * LANES:(j + 1) * LANES] = o.T.astype(BF16)


def _attn_a(q, k, vt, sink, n_batch, n_lat, n_ctx, with_ctx):
    t_rows = q.shape[0]
    blk = A_BLOCK
    nb = n_lat // blk
    ncb = n_ctx // blk if with_ctx else 0
    qrow = lambda b, n: jnp.where(n < nb, b * nb + n, n_batch * nb + b * (n_ctx // blk) + (n - nb))
    ctx_blk = lambda b: (n_batch * n_lat) // n_ctx + b
    grid_spec = pltpu.PrefetchScalarGridSpec(
        num_scalar_prefetch=1,
        grid=(n_batch, nb + ncb),
        in_specs=[
            pl.BlockSpec((blk, A_Q), lambda b, n, s: (qrow(b, n), 0)),
            pl.BlockSpec((n_lat, A_KV), lambda b, n, s: (b, 0)),
            pl.BlockSpec((A_KV, n_lat), lambda b, n, s: (0, b)),
            pl.BlockSpec((n_ctx, A_KV), lambda b, n, s: (ctx_blk(b), 0)),
            pl.BlockSpec((A_KV, n_ctx), lambda b, n, s: (0, ctx_blk(b))),
        ],
        out_specs=pl.BlockSpec((blk, A_Q), lambda b, n, s: (qrow(b, n), 0)),
    )
    return pl.pallas_call(
        functools.partial(_attn_a_kernel, n_blocks=nb, n_lat=n_lat),
        grid_spec=grid_spec,
        out_shape=jax.ShapeDtypeStruct((t_rows if with_ctx else n_batch * n_lat, A_Q), BF16),
        compiler_params=pltpu.CompilerParams(vmem_limit_bytes=VMEM_LIMIT),
    )(sink, q, k, vt, k, vt)


def _attn_b_kernel(q_ref, k_ref, vt_ref, kc_ref, vct_ref, bias_ref, o_ref, *, rows, n_steps):
    r = pl.program_id(2)
    nq = B_QROWS * GRID_W
    nk = B_KROWS * GRID_W
    r0 = jnp.where(r < n_steps, r * B_QROWS, 0)
    ws = jnp.clip(r0 - NA_KH // 2, 0, rows - B_KROWS)
    start = pl.multiple_of(ws * GRID_W, LANES)
    lo = _lane_iota((nq, LANES)) < HEAD_DIM
    zero = jnp.zeros((nq, LANES), BF16)
    dn = (((1,), (1,)), ((), ()))
    scores = []
    for g in range(B_PAIRS):
        cols = slice(g * LANES, (g + 1) * LANES)
        q = q_ref[:, cols]
        qq = jnp.concatenate([jnp.where(lo, q, zero), jnp.where(lo, zero, q)], axis=0)
        s_loc = lax.dot_general(k_ref[pl.ds(start, nk), cols], qq, dn,
                                preferred_element_type=F32) + bias_ref[g]
        s_ctx = lax.dot_general(kc_ref[:, cols], qq, dn, preferred_element_type=F32)
        scores.append((s_loc, s_ctx))
    for g, (s_loc, s_ctx) in enumerate(scores):
        cols = slice(g * LANES, (g + 1) * LANES)
        m = jnp.maximum(jnp.max(s_loc, axis=0, keepdims=True), jnp.max(s_ctx, axis=0, keepdims=True))
        p_loc = jnp.exp2(s_loc - m)
        p_ctx = jnp.exp2(s_ctx - m)
        l = jnp.sum(p_loc, axis=0, keepdims=True) + jnp.sum(p_ctx, axis=0, keepdims=True)
        ot = (jnp.dot(vt_ref[cols, pl.ds(start, nk)], p_loc.astype(BF16), preferred_element_type=F32)
              + jnp.dot(vct_ref[cols, :], p_ctx.astype(BF16), preferred_element_type=F32)) * (1.0 / l)
        o = jnp.concatenate([ot[:HEAD_DIM, :nq], ot[HEAD_DIM:, nq:]], axis=0)
        o_ref[:, cols] = o.T.astype(BF16)


def _nbr_bias_table(rpb, rows):
    n_steps = rows // B_QROWS
    steps = [min(2, n_steps - 1), 0, 1, n_steps - 2, n_steps - 1]
    w = GRID_W
    n_heads = rpb.shape[0]
    c = np.arange(w)[None, :]
    kc = np.arange(w)[:, None]
    cs = np.clip(c - NA_KW // 2, 0, w - NA_KW)
    col_ok = (kc >= cs) & (kc < cs + NA_KW)
    onehot = (((kc - c + NA_KW - 1)[None] == np.arange(2 * NA_KW - 1)[:, None, None]) & col_ok[None])
    toe = jnp.einsum("hrd,dkc->hrkc", rpb.astype(F32), jnp.asarray(onehot, F32),
                     precision=lax.Precision.HIGHEST)
    toe = jnp.where(col_ok[None, None], toe * math.log2(math.e), MASK_VALUE)
    masked = jnp.full((n_heads, w, w), MASK_VALUE, F32)
    tabs = []
    for st in steps:
        r0 = st * B_QROWS
        ws = int(np.clip(r0 - NA_KH // 2, 0, rows - B_KROWS))
        q_cols = []
        for rq in range(B_QROWS):
            r = r0 + rq
            rs = int(np.clip(r - NA_KH // 2, 0, rows - NA_KH))
            blocks = [toe[:, ws + ki - r + NA_KH - 1] if rs <= ws + ki < rs + NA_KH else masked
                      for ki in range(B_KROWS)]
            q_cols.append(jnp.concatenate(blocks, axis=1))
        tabs.append(jnp.concatenate(q_cols, axis=2))
    tabs.append(jnp.full_like(tabs[0], MASK_VALUE))
    tab = jnp.stack(tabs)
    n_var, _, nk, nq = tab.shape
    tab = tab.reshape(n_var, n_heads // 2, 2, nk, nq)
    return jnp.transpose(tab, (0, 1, 3, 2, 4)).reshape(n_var, n_heads // 2, nk, 2 * nq)


def _attn_b(q, k, vt, bias_tab, n_batch, n_lat, n_ctx, with_ctx):
    t_rows = q.shape[0]
    rows = n_lat // GRID_W
    nq = B_QROWS * GRID_W
    nk = B_KROWS * GRID_W
    gw = B_PAIRS * LANES
    n_steps = rows // B_QROWS
    n_cstep = n_ctx // nq if with_ctx else 0
    qrow = lambda b, r: jnp.where(r < n_steps, b * n_steps + r,
                                  n_batch * n_steps + b * (n_ctx // nq) + (r - n_steps))
    ctx_blk = lambda b: (n_batch * n_lat) // n_ctx + b

    def variant(r):
        v = jnp.where(r == 0, 1, 0)
        v = jnp.where(r == 1, 2, v)
        v = jnp.where(r == n_steps - 2, 3, v)
        v = jnp.where(r == n_steps - 1, 4, v)
        return jnp.where(r >= n_steps, 5, v)

    return pl.pallas_call(
        functools.partial(_attn_b_kernel, rows=rows, n_steps=n_steps),
        grid=(n_batch, B_HEADS // (2 * B_PAIRS), n_steps + n_cstep),
        in_specs=[
            pl.BlockSpec((nq, gw), lambda b, j, r: (qrow(b, r), j)),
            pl.BlockSpec((n_lat, gw), lambda b, j, r: (b, j)),
            pl.BlockSpec((gw, n_lat), lambda b, j, r: (j, b)),
            pl.BlockSpec((n_ctx, gw), lambda b, j, r: (ctx_blk(b), j)),
            pl.BlockSpec((gw, n_ctx), lambda b, j, r: (j, ctx_blk(b))),
            pl.BlockSpec((None, B_PAIRS, nk, 2 * nq), lambda b, j, r: (variant(r), j, 0, 0)),
        ],
        out_specs=pl.BlockSpec((nq, gw), lambda b, j, r: (qrow(b, r), j)),
        out_shape=jax.ShapeDtypeStruct((t_rows if with_ctx else n_batch * n_lat, B_W), BF16),
        compiler_params=pltpu.CompilerParams(vmem_limit_bytes=VMEM_LIMIT),
    )(q, k, vt, k, vt, bias_tab)


def _attn_c_kernel(*refs, latent, n_kchunks, lam_init, n_cast=0):
    if latent:
        lam_ref, q_ref, k_ref, vt_ref, kc_ref, vct_ref, g_ref = refs[:7]
        cast_in = refs[7:7 + n_cast]
        o_ref = refs[7 + n_cast]
        cast_out = refs[8 + n_cast:8 + 2 * n_cast]
        s_a, mx_a, s_b, mx_b, s_c, mx_c, m_scr, l_scr, acc_scr = refs[8 + 2 * n_cast:]
        slots = ((s_a, mx_a), (s_b, mx_b))
        for src, dst in zip(cast_in, cast_out):
            dst[...] = src[...].astype(BF16)
    else:
        lam_ref, q_ref, kc_ref, vct_ref, g_ref, o_prev_ref, o_ref, s_c, mx_c, m_scr, l_scr, acc_scr = refs
    slot_c = (s_c, mx_c)
    tq = q_ref.shape[0]
    tk = C_KCHUNK
    lo = _lane_iota((tq, LANES)) < HEAD_DIM
    zero = jnp.zeros((tq, LANES), BF16)
    q_maps = []
    for hh in range(C_STEP_HEADS):
        q = q_ref[:, hh * LANES:(hh + 1) * LANES]
        q_maps += [jnp.where(lo, q, zero), jnp.where(lo, zero, q)]
    n_chains = len(q_maps)
    dn = (((1,), (1,)), ((), ()))

    def head_cols(ci):
        return slice((ci // 2) * LANES, (ci // 2 + 1) * LANES)

    def scores(kk_of, slot):
        s_ref, mx_ref = slot
        for ci in range(n_chains):
            s = lax.dot_general(kk_of(head_cols(ci)), q_maps[ci], dn, preferred_element_type=F32)
            s_ref[ci] = s
            mx_ref[ci] = jnp.max(s, axis=0, keepdims=True)

    def softmax_pv(slot, vvt_of):
        s_ref, mx_ref = slot
        for ci in range(n_chains):
            s = s_ref[ci]
            m = m_scr[ci]
            m_new = jnp.maximum(m, mx_ref[ci])
            alpha = jnp.exp2(m - m_new)
            p = jnp.exp2(s - m_new)
            l_scr[ci] = alpha * l_scr[ci] + jnp.sum(p, axis=0, keepdims=True)
            acc_scr[ci] = alpha * acc_scr[ci] + jnp.dot(vvt_of(head_cols(ci)), p.astype(BF16),
                                                        preferred_element_type=F32)
            m_scr[ci] = m_new

    def chunk_slice(c):
        return pl.ds(c * tk if isinstance(c, int) else pl.multiple_of(c * tk, tk), tk)

    def k_chunk(c):
        return lambda cols: k_ref[chunk_slice(c), cols]

    def vt_chunk(c):
        return lambda cols: vt_ref[cols, chunk_slice(c)]

    k_ctx = lambda cols: kc_ref[:, cols]
    vt_ctx = lambda cols: vct_ref[cols, :]

    m_scr[...] = jnp.full(m_scr.shape, -jnp.inf, F32)
    l_scr[...] = jnp.zeros(l_scr.shape, F32)
    acc_scr[...] = jnp.zeros(acc_scr.shape, F32)

    def finish():
        softmax_pv(slot_c, vt_ctx)
        lp = lam_ref[...]
        lam = (jnp.exp(jnp.sum(lp[0:1] * lp[1:2], axis=-1, keepdims=True))
               - jnp.exp(jnp.sum(lp[2:3] * lp[3:4], axis=-1, keepdims=True)) + lam_init)
        for hh in range(C_STEP_HEADS):
            c1, c2 = 2 * hh, 2 * hh + 1
            od = acc_scr[c1] * (1.0 / l_scr[c1]) - lam * (acc_scr[c2] * (1.0 / l_scr[c2]))
            ms = jnp.mean(od * od, axis=0, keepdims=True)
            on = (od * lax.rsqrt(ms + LN_EPS)).T
            o_ref[:, hh * LANES:(hh + 1) * LANES] = ((on * g_ref[...]) * (1.0 - lam_init)).astype(BF16)

    if not latent:
        scores(k_ctx, slot_c)
        finish()
        return

    scores(k_chunk(0), slots[0])

    def body(it, carry):
        c0 = C_UNROLL * it
        for u in range(C_UNROLL):
            scores(k_chunk(c0 + u + 1), slots[(u + 1) % 2])
            softmax_pv(slots[u % 2], vt_chunk(c0 + u))
        return carry

    n_it = (n_kchunks - 1) // C_UNROLL
    lax.fori_loop(0, n_it, body, 0)
    for c in range(C_UNROLL * n_it, n_kchunks):
        if c + 1 < n_kchunks:
            scores(k_chunk(c + 1), slots[(c + 1) % 2])
        else:
            scores(k_ctx, slot_c)
        softmax_pv(slots[c % 2], vt_chunk(c))
    finish()


def _attn_c(q, k, vt, lam_p, subln, lam_init, n_batch, n_lat, n_ctx, with_ctx, cast_jobs=()):
    t_rows = q.shape[0]
    hw = C_STEP_HEADS * LANES
    nch = 2 * C_STEP_HEADS
    ctx_blk = lambda b: (n_batch * n_lat) // n_ctx + b
    out_shape = jax.ShapeDtypeStruct((t_rows if with_ctx else n_batch * n_lat, C_OUT), BF16)
    params = pltpu.CompilerParams(vmem_limit_bytes=VMEM_LIMIT)
    stats = lambda tq: [pltpu.VMEM((nch, 1, tq), F32), pltpu.VMEM((nch, 1, tq), F32),
                        pltpu.VMEM((nch, LANES, tq), F32)]
    slot = lambda nk, tq: [pltpu.VMEM((nch, nk, tq), F32), pltpu.VMEM((nch, 1, tq), F32)]
    lam_spec = pl.BlockSpec((4, HEAD_DIM), lambda b, h, i: (0, 0))
    kc_spec = pl.BlockSpec((n_ctx, hw), lambda b, h, i: (ctx_blk(b), h))
    vct_spec = pl.BlockSpec((hw, n_ctx), lambda b, h, i: (h, ctx_blk(b)))
    g_spec = pl.BlockSpec((1, LANES), lambda b, h, i: (0, 0))
    g = subln.reshape(1, LANES)

    tq = C_QTILE
    n_qlat = n_lat // tq
    n_hgrp = C_HEADS // C_STEP_HEADS
    n_steps = n_batch * n_hgrp * n_qlat
    step = lambda b, h, i: (b * n_hgrp + h) * n_qlat + i
    cast_specs = []
    for w in cast_jobs:
        assert w.shape[0] % n_steps == 0 and (w.shape[0] // n_steps) % 16 == 0
        cast_specs.append(pl.BlockSpec((w.shape[0] // n_steps, w.shape[1]), lambda b, h, i: (step(b, h, i), 0)))
    outs = pl.pallas_call(
        functools.partial(_attn_c_kernel, latent=True, n_kchunks=n_lat // C_KCHUNK, lam_init=lam_init,
                          n_cast=len(cast_jobs)),
        grid=(n_batch, n_hgrp, n_qlat),
        in_specs=[
            lam_spec,
            pl.BlockSpec((tq, hw), lambda b, h, i: (b * n_qlat + i, h)),
            pl.BlockSpec((n_lat, hw), lambda b, h, i: (b, h)),
            pl.BlockSpec((hw, n_lat), lambda b, h, i: (h, b)),
            kc_spec, vct_spec, g_spec,
        ] + cast_specs,
        out_specs=[pl.BlockSpec((tq, hw), lambda b, h, i: (b * n_qlat + i, h))] + cast_specs,
        out_shape=[out_shape] + [jax.ShapeDtypeStruct(w.shape, BF16) for w in cast_jobs],
        scratch_shapes=(slot(C_KCHUNK, tq) + slot(C_KCHUNK, tq) + slot(n_ctx, tq) + stats(tq)),
        compiler_params=params,
    )(lam_p, q, k, vt, k, vt, g, *cast_jobs)
    o, casted = outs[0], list(outs[1:])
    if not with_ctx:
        return o, casted
    o = pl.pallas_call(
        functools.partial(_attn_c_kernel, latent=False, n_kchunks=0, lam_init=lam_init),
        grid=(n_batch, C_HEADS // C_STEP_HEADS, 1),
        in_specs=[
            lam_spec,
            pl.BlockSpec((n_ctx, hw), lambda b, h, i: (ctx_blk(b), h)),
            kc_spec, vct_spec, g_spec,
            pl.BlockSpec(memory_space=pl.ANY),
        ],
        out_specs=pl.BlockSpec((n_ctx, hw), lambda b, h, i: (ctx_blk(b), h)),
        out_shape=out_shape,
        scratch_shapes=slot(n_ctx, n_ctx) + stats(n_ctx),
        input_output_aliases={5: 0},
        compiler_params=params,
    )(lam_p, q, k, vt, g, o)
    return o, casted


def _bf16_truncate(x):
    bits = lax.bitcast_convert_type(x, jnp.uint32) & jnp.uint32(0xFFFF0000)
    return lax.bitcast_convert_type(bits, F32)


def _top2_router(h, wr_ref):
    h_top = _bf16_truncate(h)
    h_hi = h_top.astype(BF16)
    h_lo = (h - h_top).astype(BF16)
    a = jnp.dot(h_hi, wr_ref[...], preferred_element_type=F32)
    b = jnp.dot(h_lo, wr_ref[:, :LANES], preferred_element_type=F32)
    lg = a[:, :LANES] + a[:, LANES:] + b
    lane = _lane_iota(lg.shape)
    lanef = lane.astype(F32)
    lg = jnp.where(lane < N_EXPERTS, lg, -jnp.inf)
    m1 = jnp.max(lg, axis=-1, keepdims=True)
    i1 = jnp.min(jnp.where(lg == m1, lanef, float(LANES)), axis=-1, keepdims=True)
    lg2 = jnp.where(lanef == i1, -jnp.inf, lg)
    m2 = jnp.max(lg2, axis=-1, keepdims=True)
    i2 = jnp.min(jnp.where(lg2 == m2, lanef, float(LANES)), axis=-1, keepdims=True)
    e = jnp.exp(m2 - m1)
    w1 = 1.0 / (1.0 + e)
    w2 = e / (1.0 + e)
    out = jnp.where(lane == 0, i1, 0.0)
    out = jnp.where(lane == 1, i2, out)
    out = jnp.where(lane == 2, w1, out)
    return jnp.where(lane == 3, w2, out)


def _pack_bf16_pairs(h):
    bits = lax.bitcast_convert_type(h, jnp.uint32)
    bits = bits + (jnp.uint32(0x7FFF) + ((bits >> 16) & jnp.uint32(1)))
    half = h.shape[1] // 2
    packed = (bits[:, half:] & jnp.uint32(0xFFFF0000)) | (bits[:, :half] >> 16)
    return lax.bitcast_convert_type(packed, F32)


def _unpack_bf16_pairs(words):
    p = lax.bitcast_convert_type(words, jnp.uint32)
    lo = lax.bitcast_convert_type(p << 16, F32)
    hi = lax.bitcast_convert_type(p & jnp.uint32(0xFFFF0000), F32)
    return jnp.concatenate([lo, hi], axis=1)


def _outproj_kernel(*refs, n_in, router):
    o_refs = refs[:n_in]
    w_refs = refs[n_in:2 * n_in]
    x_ref, g_ref, lng_ref, lnb_ref, sc_ref, sh_ref = refs[2 * n_in:2 * n_in + 6]
    rest = refs[2 * n_in + 6:]
    y = jnp.dot(o_refs[0][...], w_refs[0][...], preferred_element_type=F32)
    for o_r, w_r in zip(o_refs[1:], w_refs[1:]):
        y = y + jnp.dot(o_r[...], w_r[...], preferred_element_type=F32)
    xn = _layernorm_rows(DN_ALPHA * x_ref[...] + g_ref[...] * y, lng_ref[...], lnb_ref[...])
    h2 = xn * (1.0 + sc_ref[...]) + sh_ref[...]
    if router:
        wr_ref, xo_ref, h_ref, r_ref = rest
        r_ref[...] = _top2_router(h2, wr_ref)
        h_ref[...] = _pack_bf16_pairs(h2)
    else:
        xo_ref, h_ref = rest
        h_ref[...] = h2.astype(BF16)
    xo_ref[...] = xn


def _out_proj(o_list, w_list, xs, mods_l, ln_g, ln_b, n_tiles, tiles_per_batch, n_batch, w_router=None):
    d = xs.shape[1]
    tm = ROW_TILE
    t_rows = n_tiles * tm
    grp = lambda t: jnp.minimum(t // tiles_per_batch, n_batch)
    mod = lambda k: pl.BlockSpec((None, None, 1, d), lambda t: (grp(t), k, 0, 0))
    in_specs = [pl.BlockSpec((tm, o.shape[1]), lambda t: (t, 0)) for o in o_list]
    in_specs += [pl.BlockSpec(w.shape, lambda t: (0, 0)) for w in w_list]
    in_specs += [pl.BlockSpec((tm, d), lambda t: (t, 0)), mod(2),
                 pl.BlockSpec((1, d), lambda t: (0, 0)), pl.BlockSpec((1, d), lambda t: (0, 0)),
                 mod(4), mod(3)]
    args = list(o_list) + list(w_list) + [xs, mods_l, ln_g.reshape(1, d), ln_b.reshape(1, d), mods_l, mods_l]
    out_shape = [jax.ShapeDtypeStruct((t_rows, d), F32), jax.ShapeDtypeStruct((t_rows, d), BF16)]
    out_specs = [pl.BlockSpec((tm, d), lambda t: (t, 0)), pl.BlockSpec((tm, d), lambda t: (t, 0))]
    if w_router is not None:
        out_shape[1] = jax.ShapeDtypeStruct((t_rows, d // 2), F32)
        out_specs[1] = pl.BlockSpec((tm, d // 2), lambda t: (t, 0))
        in_specs.append(pl.BlockSpec(w_router.shape, lambda t: (0, 0)))
        args.append(w_router)
        out_shape.append(jax.ShapeDtypeStruct((t_rows, LANES), F32))
        out_specs.append(pl.BlockSpec((tm, LANES), lambda t: (t, 0)))
    return pl.pallas_call(
        functools.partial(_outproj_kernel, n_in=len(o_list), router=w_router is not None),
        grid=(n_tiles,),
        in_specs=in_specs,
        out_specs=out_specs,
        out_shape=out_shape,
        compiler_params=pltpu.CompilerParams(vmem_limit_bytes=VMEM_LIMIT),
    )(*args)


def _swiglu_acc(h, wg_ref, wu_ref, wd_ref, acc_ref, c):
    g = jnp.dot(h, wg_ref[...], preferred_element_type=F32)
    u = jnp.dot(h, wu_ref[...], preferred_element_type=F32)
    a = (g * (1.0 / (1.0 + jnp.exp(-g))) * u).astype(BF16)
    part = jnp.dot(a, wd_ref[...], preferred_element_type=F32)

    @pl.when(c == 0)
    def _():
        acc_ref[...] = part

    @pl.when(c > 0)
    def _():
        acc_ref[...] += part


def _ffn_dense_kernel(h_ref, wg_ref, wu_ref, wd_ref, x_ref, g_ref, lng_ref, lnb_ref, xo_ref, acc_ref):
    c = pl.program_id(1)
    _swiglu_acc(h_ref[...], wg_ref, wu_ref, wd_ref, acc_ref, c)

    @pl.when(c == pl.num_programs(1) - 1)
    def _():
        z = DN_ALPHA * x_ref[...] + g_ref[...] * acc_ref[...]
        xo_ref[...] = _layernorm_rows(z, lng_ref[...], lnb_ref[...])


def _ffn_dense(h2, w_up, w_down, xs, mods_l, ln_g, ln_b, n_tiles, tiles_per_batch, n_batch):
    d = xs.shape[1]
    tm = ROW_TILE
    t_rows = n_tiles * tm
    ff = w_down.shape[0]
    nc = ff // FF_CHUNK
    grp = lambda t: jnp.minimum(t // tiles_per_batch, n_batch)
    return pl.pallas_call(
        _ffn_dense_kernel,
        grid=(n_tiles, nc),
        in_specs=[
            pl.BlockSpec((tm, d), lambda t, c: (t, 0)),
            pl.BlockSpec((d, FF_CHUNK), lambda t, c: (0, c)),
            pl.BlockSpec((d, FF_CHUNK), lambda t, c: (0, nc + c)),
            pl.BlockSpec((FF_CHUNK, d), lambda t, c: (c, 0)),
            pl.BlockSpec((tm, d), lambda t, c: (t, 0)),
            pl.BlockSpec((None, None, 1, d), lambda t, c: (grp(t), 5, 0, 0)),
            pl.BlockSpec((1, d), lambda t, c: (0, 0)),
            pl.BlockSpec((1, d), lambda t, c: (0, 0)),
        ],
        out_specs=pl.BlockSpec((tm, d), lambda t, c: (t, 0)),
        out_shape=jax.ShapeDtypeStruct((t_rows, d), F32),
        scratch_shapes=[pltpu.VMEM((tm, d), F32)],
        compiler_params=pltpu.CompilerParams(vmem_limit_bytes=VMEM_LIMIT),
    )(h2, w_up, w_up, w_down, xs, mods_l, ln_g.reshape(1, d), ln_b.reshape(1, d))


def _ffn_expert_kernel(te_ref, nu_ref, h_ref, wg_ref, wu_ref, wd_ref, y_ref, acc_ref):
    i = pl.program_id(0)
    c = pl.program_id(1)

    @pl.when(i < nu_ref[0])
    def _():
        _swiglu_acc(_unpack_bf16_pairs(h_ref[...]).astype(BF16), wg_ref, wu_ref, wd_ref, acc_ref, c)

    @pl.when(c == pl.num_programs(1) - 1)
    def _():
        y_ref[...] = _pack_bf16_pairs(acc_ref[...])


def _ffn_experts(hs, w_up, w_down, layer, tile_expert, n_used):
    p_rows = hs.shape[0]
    d = w_down.shape[-1]
    tm = MOE_TILE
    n_tiles = p_rows // tm
    ef = w_down.shape[2]
    nc = ef // EXPERT_FF_CHUNK
    chunk = lambda i, c, nu: jnp.where(i < nu[0], c, nc - 1)
    grid_spec = pltpu.PrefetchScalarGridSpec(
        num_scalar_prefetch=2,
        grid=(n_tiles, nc),
        in_specs=[
            pl.BlockSpec((tm, d // 2), lambda i, c, te, nu: (i, 0)),
            pl.BlockSpec((None, None, d, EXPERT_FF_CHUNK),
                         lambda i, c, te, nu: (layer, te[i], 0, chunk(i, c, nu))),
            pl.BlockSpec((None, None, d, EXPERT_FF_CHUNK),
                         lambda i, c, te, nu: (layer, te[i], 0, nc + chunk(i, c, nu))),
            pl.BlockSpec((None, None, EXPERT_FF_CHUNK, d),
                         lambda i, c, te, nu: (layer, te[i], chunk(i, c, nu), 0)),
        ],
        out_specs=pl.BlockSpec((tm, d // 2), lambda i, c, te, nu: (i, 0)),
        scratch_shapes=[pltpu.VMEM((tm, d), F32)],
    )
    return pl.pallas_call(
        _ffn_expert_kernel,
        grid_spec=grid_spec,
        out_shape=jax.ShapeDtypeStruct((p_rows, d // 2), F32),
        compiler_params=pltpu.CompilerParams(vmem_limit_bytes=VMEM_LIMIT),
    )(tile_expert, n_used, hs, w_up, w_up, w_down)


def _combine_kernel(y0_ref, y1_ref, r_ref, x_ref, g_ref, lng_ref, lnb_ref, xo_ref):
    r = r_ref[...]
    f = r[:, 2:3] * _unpack_bf16_pairs(y0_ref[...]) + r[:, 3:4] * _unpack_bf16_pairs(y1_ref[...])
    z = DN_ALPHA * x_ref[...] + g_ref[...] * f
    xo_ref[...] = _layernorm_rows(z, lng_ref[...], lnb_ref[...])


def _moe_combine(y0, y1, route, xs, mods_l, ln_g, ln_b, n_tiles, tiles_per_batch, n_batch):
    d = xs.shape[1]
    tm = ROW_TILE
    t_rows = n_tiles * tm
    grp = lambda t: jnp.minimum(t // tiles_per_batch, n_batch)
    row = pl.BlockSpec((tm, d), lambda t: (t, 0))
    packed = pl.BlockSpec((tm, d // 2), lambda t: (t, 0))
    return pl.pallas_call(
        _combine_kernel,
        grid=(n_tiles,),
        in_specs=[packed, packed, pl.BlockSpec((tm, LANES), lambda t: (t, 0)), row,
                  pl.BlockSpec((None, None, 1, d), lambda t: (grp(t), 5, 0, 0)),
                  pl.BlockSpec((1, d), lambda t: (0, 0)), pl.BlockSpec((1, d), lambda t: (0, 0))],
        out_specs=row,
        out_shape=jax.ShapeDtypeStruct((t_rows, d), F32),
        compiler_params=pltpu.CompilerParams(vmem_limit_bytes=VMEM_LIMIT),
    )(y0, y1, route, xs, mods_l, ln_g.reshape(1, d), ln_b.reshape(1, d))


def _routing_plan(route, n_rows):
    tm = MOE_TILE
    e_idx = route[:n_rows, 0:2].astype(jnp.int32).reshape(-1)
    onehot = (e_idx[:, None] == jnp.arange(N_EXPERTS, dtype=jnp.int32)[None, :]).astype(jnp.int32)
    csum = jnp.cumsum(onehot, axis=0)
    counts = csum[-1]
    rank = jnp.sum((csum - onehot) * onehot, axis=1)
    padded = ((counts + tm - 1) // tm) * tm
    ends = jnp.cumsum(padded)
    starts = ends - padded
    dest = starts[e_idx] + rank
    n_tiles = (2 * n_rows + N_EXPERTS * (tm - 1)) // tm
    p_rows = n_tiles * tm
    row_token = jnp.zeros((p_rows,), jnp.int32).at[dest].set(jnp.arange(2 * n_rows, dtype=jnp.int32) // 2)
    tile_start = jnp.arange(n_tiles, dtype=jnp.int32) * tm
    tile_expert = jnp.minimum(jnp.sum((tile_start[:, None] >= ends[None, :]).astype(jnp.int32), axis=1),
                              N_EXPERTS - 1)
    n_used = (ends[-1] // tm).astype(jnp.int32).reshape(1)
    last = tile_expert[jnp.maximum(n_used[0] - 1, 0)]
    tile_expert = jnp.where(jnp.arange(n_tiles) < n_used[0], tile_expert, last).astype(jnp.int32)
    return row_token, dest.reshape(n_rows, 2), tile_expert, n_used


def _rope_tables(n_lat):
    t = jnp.arange(n_lat, dtype=jnp.int32)
    row = (t // GRID_W).astype(F32)
    col = (t % GRID_W).astype(F32)
    inv = ROPE_THETA ** (-jnp.arange(0, ROPE_AXIS_DIM, 2, dtype=F32) / ROPE_AXIS_DIM)
    ar = row[:, None] * inv[None, :]
    ac = col[:, None] * inv[None, :]
    ang = jnp.concatenate([ar, ar, ac, ac], axis=-1)
    cos = jnp.tile(jnp.cos(ang), (1, LANES // HEAD_DIM))
    sin = jnp.tile(jnp.sin(ang), (1, LANES // HEAD_DIM))
    sign = jnp.where((jnp.arange(LANES) % 32) < 16, -1.0, 1.0).astype(F32)
    cos = jnp.concatenate([cos, jnp.ones((ROW_TILE, LANES), F32)], axis=0)
    sin = jnp.concatenate([sin * sign[None, :], jnp.zeros((ROW_TILE, LANES), F32)], axis=0)
    return cos, sin


def _lambda_init(layer):
    return 0.8 - 0.6 * math.exp(-0.3 * layer)


def kernel(x, c, ctx, c_ctx, w_mod, b_mod, ln_g, ln_b, w_in_ab, w_out_ab, sink_a, rpb_b, w_in_c, w_out_c,
           lam_c, subln_c, w_ffn_up, w_ffn_down, w_router, w_exp_up, w_exp_down):
    n_batch, n_lat, d = x.shape
    n_ctx = ctx.shape[1]
    assert d == D_MODEL and n_batch * n_ctx == ROW_TILE and n_lat % ROW_TILE == 0
    assert n_batch + 1 <= 8 and n_lat % C_KCHUNK == 0 and n_lat % C_QTILE == 0 and n_ctx % LANES == 0
    rows = n_lat // GRID_W
    assert rows >= 12 and rows % B_QROWS == 0
    tiles_per_batch = n_lat // ROW_TILE
    lat_tiles = n_batch * tiles_per_batch
    lat_rows = n_batch * n_lat

    xs = jnp.concatenate([x.reshape(lat_rows, d), ctx.reshape(n_batch * n_ctx, d)], axis=0)
    cond = jnp.zeros((8, d), F32).at[:n_batch].set(c).at[n_batch].set(c_ctx)
    mods = _mod_vectors(cond, w_mod, b_mod).reshape(DEPTH, 8, 6, 1, d)
    cos_t, sin_t = _rope_tables(n_lat)
    w_exp_b = None

    perm = np.array([(j + (A_HEADS // 2) * half) * HEAD_DIM + dd
                     for j in range(A_HEADS // 2) for half in range(2) for dd in range(HEAD_DIM)])

    for l in range(DEPTH):
        last = l == DEPTH - 1
        i = l // 2
        all_tiles = lat_tiles + 1
        n_tiles = lat_tiles if last else all_tiles
        mods_l = mods[l]
        if l % 2 == 0:
            w_in = w_in_ab[i]
            w_in = jnp.concatenate([w_in[:, :A_Q][:, perm], w_in[:, A_Q:]], axis=1).astype(BF16)
            q_scale = ATTN_SCALE * math.log2(math.e)
            segs = [(A_Q, True, q_scale, False), (A_KV, True, 1.0, False), (A_KV, False, 1.0, True),
                    (B_W, False, q_scale, False), (B_W, False, 1.0, False), (B_W, False, 1.0, True)]
            qa, ka, va, qb, kb, vb = _in_proj(xs, mods_l, cos_t, sin_t, w_in, segs, all_tiles,
                                              tiles_per_batch, n_batch)
            sink = sink_a[i].astype(F32) * math.log2(math.e)
            oa = _attn_a(qa, ka, va, sink, n_batch, n_lat, n_ctx, not last)
            ob = _attn_b(qb, kb, vb, _nbr_bias_table(rpb_b[i], rows), n_batch, n_lat, n_ctx, not last)
            w_out = w_out_ab[i]
            o_list = [oa, ob]
            w_list = [w_out[:A_Q][perm].astype(BF16), w_out[A_Q:].astype(BF16)]
        else:
            segs = [(C_QK, True, ATTN_SCALE * math.log2(math.e), False), (C_QK, True, 1.0, False),
                    (C_OUT, False, 1.0, True)]
            qc, kc, vc = _in_proj(xs, mods_l, cos_t, sin_t, w_in_c[i].astype(BF16), segs, all_tiles,
                                  tiles_per_batch, n_batch)
            jobs = () if w_exp_b is not None else (w_exp_up.reshape(-1, w_exp_up.shape[-1]),
                                                   w_exp_down.reshape(-1, w_exp_down.shape[-1]))
            oc, casted = _attn_c(qc, kc, vc, lam_c[i].astype(F32), subln_c[i].astype(F32), _lambda_init(l),
                                 n_batch, n_lat, n_ctx, not last, cast_jobs=jobs)
            if w_exp_b is None:
                w_exp_b = (casted[0].reshape(w_exp_up.shape), casted[1].reshape(w_exp_down.shape))
            o_list = [oc]
            w_list = [w_out_c[i].astype(BF16)]

        if l % 2 == 0:
            xs, h2 = _out_proj(o_list, w_list, xs, mods_l, ln_g[l, 0], ln_b[l, 0], n_tiles,
                               tiles_per_batch, n_batch)
            xs = _ffn_dense(h2, w_ffn_up[i].astype(BF16), w_ffn_down[i].astype(BF16), xs, mods_l,
                            ln_g[l, 1], ln_b[l, 1], n_tiles, tiles_per_batch, n_batch)
        else:
            wr = jnp.zeros((d, LANES), F32).at[:, :N_EXPERTS].set(w_router[i])
            wr_top = _bf16_truncate(wr)
            wr = jnp.concatenate([wr_top.astype(BF16), (wr - wr_top).astype(BF16)], axis=1)
            xs, h2, route = _out_proj(o_list, w_list, xs, mods_l, ln_g[l, 0], ln_b[l, 0], n_tiles,
                                      tiles_per_batch, n_batch, w_router=wr)
            n_rows = n_tiles * ROW_TILE
            row_token, dest, tile_expert, n_used = _routing_plan(route, n_rows)
            hs = jnp.take(h2, row_token, axis=0, mode="clip")
            ys = _ffn_experts(hs, w_exp_b[0], w_exp_b[1], i, tile_expert, n_used)
            y0 = jnp.take(ys, dest[:, 0], axis=0, mode="clip")
            y1 = jnp.take(ys, dest[:, 1], axis=0, mode="clip")
            xs = _moe_combine(y0, y1, route, xs, mods_l, ln_g[l, 1], ln_b[l, 1], n_tiles,
                              tiles_per_batch, n_batch)
    return xs[:lat_rows].reshape(n_batch, n_lat, d)
```

```python
import functools
import math

import jax
import jax.numpy as jnp
import numpy as np
from jax import lax
from jax.experimental import pallas as pl
from jax.experimental.pallas import tpu as pltpu

F32 = jnp.float32
BF16 = jnp.bfloat16

D_MODEL = 1024
DEPTH = 4
GRID_W = 64
HEAD_DIM = 64
LANES = 128
ATTN_SCALE = HEAD_DIM ** -0.5
A_BLOCK = 128
A_WINDOW = 128
A_HEADS = 8
A_KV_HEADS = 2
B_HEADS = 8
NA_KH = 8
NA_KW = 16
C_HEADS = 8
ROPE_THETA = 10000.0
ROPE_AXIS_DIM = HEAD_DIM // 2
FF_DIM = 2816
N_EXPERTS = 8
EXPERT_FF = 3584
LN_EPS = 1e-5
DN_ALPHA = (2 * DEPTH) ** 0.25
MASK_VALUE = -1e30
A_Q = A_HEADS * HEAD_DIM
A_KV = A_KV_HEADS * HEAD_DIM
B_W = B_HEADS * HEAD_DIM
C_QK = 1024
C_OUT = 1024

ROW_TILE = 512
MOE_TILE = 512
FF_CHUNK = 2816
EXPERT_FF_CHUNK = 1792
B_QROWS = 2
B_KROWS = B_QROWS + NA_KH
B_PAIRS = 4
C_QTILE = 256
C_KCHUNK = 512
C_UNROLL = 6
C_STEP_HEADS = 2
VMEM_LIMIT = 56 * 1024 * 1024


def _lane_iota(shape):
    return lax.broadcasted_iota(jnp.int32, shape, len(shape) - 1)


def _layernorm_rows(z, g, b):
    mu = jnp.mean(z, axis=-1, keepdims=True)
    zc = z - mu
    var = jnp.mean(zc * zc, axis=-1, keepdims=True)
    return zc * lax.rsqrt(var + LN_EPS) * g + b


def _mod_kernel(c_ref, w_ref, b_ref, o_ref):
    c = c_ref[...]
    s = c * (1.0 / (1.0 + jnp.exp(-c)))
    o_ref[0] = jnp.dot(s, w_ref[0], preferred_element_type=F32,
                       precision=lax.Precision.HIGHEST) + b_ref[0]


def _mod_vectors(cond, w_mod, b_mod):
    depth, d, n6 = w_mod.shape
    tn = 1536
    return pl.pallas_call(
        _mod_kernel,
        grid=(depth, n6 // tn),
        in_specs=[
            pl.BlockSpec((8, d), lambda l, j: (0, 0)),
            pl.BlockSpec((1, d, tn), lambda l, j: (l, 0, j)),
            pl.BlockSpec((1, 1, tn), lambda l, j: (l, 0, j)),
        ],
        out_specs=pl.BlockSpec((1, 8, tn), lambda l, j: (l, 0, j)),
        out_shape=jax.ShapeDtypeStruct((depth, 8, n6), F32),
        compiler_params=pltpu.CompilerParams(vmem_limit_bytes=VMEM_LIMIT),
    )(cond, w_mod, b_mod.reshape(depth, 1, n6))


def _rope_slab(a, cos, sin_signed, low16):
    fwd = pltpu.roll(a, LANES - 16, axis=1)
    bwd = pltpu.roll(a, 16, axis=1)
    return a * cos + jnp.where(low16, fwd, bwd) * sin_signed


def _proj_kernel(x_ref, sc_ref, sh_ref, cos_ref, sin_ref, w_ref, *out_refs, segments):
    x = x_ref[...]
    h = (x * (1.0 + sc_ref[...]) + sh_ref[...]).astype(BF16)
    cos = cos_ref[...]
    sin = sin_ref[...]
    low16 = (_lane_iota(cos.shape) % 32) < 16
    col = 0
    for o_ref, (width, rope, scale, transposed) in zip(out_refs, segments):
        for c0 in range(0, width, 256):
            cw = min(256, width - c0)
            acc = jnp.dot(h, w_ref[:, col + c0:col + c0 + cw], preferred_element_type=F32)
            if transposed:
                o_ref[c0:c0 + cw, :] = acc.T.astype(BF16)
                continue
            slabs = []
            for s0 in range(0, cw, LANES):
                a = acc[:, s0:s0 + LANES]
                if rope:
                    a = _rope_slab(a, cos, sin, low16)
                if scale != 1.0:
                    a = a * scale
                slabs.append(a.astype(BF16))
            o_ref[:, c0:c0 + cw] = slabs[0] if len(slabs) == 1 else jnp.concatenate(slabs, axis=1)
        col += width


def _in_proj(xs, mods_l, cos_t, sin_t, w, segments, n_tiles, tiles_per_batch, n_batch):
    t_rows, d = xs.shape
    tm = ROW_TILE
    grp = lambda t: jnp.minimum(t // tiles_per_batch, n_batch)
    pos = lambda t: jnp.where(t < n_batch * tiles_per_batch, t % tiles_per_batch, tiles_per_batch)
    out_shape = [jax.ShapeDtypeStruct((s[0], t_rows) if s[3] else (t_rows, s[0]), BF16) for s in segments]
    out_specs = [pl.BlockSpec((s[0], tm), lambda t: (0, t)) if s[3] else pl.BlockSpec((tm, s[0]), lambda t: (t, 0))
                 for s in segments]
    return pl.pallas_call(
        functools.partial(_proj_kernel, segments=tuple(segments)),
        grid=(n_tiles,),
        in_specs=[
            pl.BlockSpec((tm, d), lambda t: (t, 0)),
            pl.BlockSpec((None, None, 1, d), lambda t: (grp(t), 1, 0, 0)),
            pl.BlockSpec((None, None, 1, d), lambda t: (grp(t), 0, 0, 0)),
            pl.BlockSpec((tm, LANES), lambda t: (pos(t), 0)),
            pl.BlockSpec((tm, LANES), lambda t: (pos(t), 0)),
            pl.BlockSpec(w.shape, lambda t: (0, 0)),
        ],
        out_specs=out_specs,
        out_shape=out_shape,
        compiler_params=pltpu.CompilerParams(vmem_limit_bytes=VMEM_LIMIT),
    )(xs, mods_l, mods_l, cos_t, sin_t, w)


def _attn_a_kernel(sink_ref, q_ref, k_ref, vt_ref, kc_ref, vct_ref, o_ref, *, n_blocks, n_lat):
    n = pl.program_id(1)
    blk = A_BLOCK
    win = 3 * blk
    start = pl.multiple_of(jnp.clip((n - 1) * blk, 0, n_lat - win), blk)
    keys = jnp.concatenate([k_ref[pl.ds(start, win), :], kc_ref[...]], axis=0)
    vals_t = jnp.concatenate([vt_ref[:, pl.ds(start, win)], vct_ref[...]], axis=1)
    nk = keys.shape[0]
    qbase = jnp.where(n < n_blocks, n * blk, -(1 << 20))
    kidx = lax.broadcasted_iota(jnp.int32, (nk, 2 * blk), 0)
    qidx = lax.broadcasted_iota(jnp.int32, (nk, 2 * blk), 1)
    qpos = qbase + jnp.where(qidx >= blk, qidx - blk, qidx)
    in_band = jnp.abs(start + kidx - qpos) <= A_WINDOW
    bias = jnp.where((kidx >= win) | in_band, 0.0, MASK_VALUE).astype(F32)
    lo = _lane_iota((blk, LANES)) < HEAD_DIM
    zero = jnp.zeros((blk, LANES), BF16)
    col_hi = _lane_iota((1, 2 * blk)) >= blk
    dn = (((1,), (1,)), ((), ()))
    scores = []
    for j in range(A_HEADS // 2):
        qc = q_ref[:, j * LANES:(j + 1) * LANES]
        qq = jnp.concatenate([jnp.where(lo, qc, zero), jnp.where(lo, zero, qc)], axis=0)
        scores.append(lax.dot_general(keys, qq, dn, preferred_element_type=F32) + bias)
    for j, s in enumerate(scores):
        sink = jnp.where(col_hi, sink_ref[j + A_HEADS // 2], sink_ref[j])
        m = jnp.maximum(jnp.max(s, axis=0, keepdims=True), sink)
        p = jnp.exp2(s - m)
        l = jnp.sum(p, axis=0, keepdims=True) + jnp.exp2(sink - m)
        ot = jnp.dot(vals_t, p.astype(BF16), preferred_element_type=F32) * (1.0 / l)
        o = jnp.concatenate([ot[:HEAD_DIM, :blk], ot[HEAD_DIM:, blk:]], axis=0)
        o_ref[:, j * LANES:(j + 1) * LANES] = o.T.astype(BF16)


def _attn_a(q, k, vt, sink, n_batch, n_lat, n_ctx, with_ctx):
    t_rows = q.shape[0]
    blk = A_BLOCK
    nb = n_lat // blk
    ncb = n_ctx // blk if with_ctx else 0
    qrow = lambda b, n: jnp.where(n < nb, b * nb + n, n_batch * nb + b * (n_ctx // blk) + (n - nb))
    ctx_blk = lambda b: (n_batch * n_lat) // n_ctx + b
    grid_spec = pltpu.PrefetchScalarGridSpec(
        num_scalar_prefetch=1,
        grid=(n_batch, nb + ncb),
        in_specs=[
            pl.BlockSpec((blk, A_Q), lambda b, n, s: (qrow(b, n), 0)),
            pl.BlockSpec((n_lat, A_KV), lambda b, n, s: (b, 0)),
            pl.BlockSpec((A_KV, n_lat), lambda b, n, s: (0, b)),
            pl.BlockSpec((n_ctx, A_KV), lambda b, n, s: (ctx_blk(b), 0)),
            pl.BlockSpec((A_KV, n_ctx), lambda b, n, s: (0, ctx_blk(b))),
        ],
        out_specs=pl.BlockSpec((blk, A_Q), lambda b, n, s: (qrow(b, n), 0)),
    )
    return pl.pallas_call(
        functools.partial(_attn_a_kernel, n_blocks=nb, n_lat=n_lat),
        grid_spec=grid_spec,
        out_shape=jax.ShapeDtypeStruct((t_rows if with_ctx else n_batch * n_lat, A_Q), BF16),
        compiler_params=pltpu.CompilerParams(vmem_limit_bytes=VMEM_LIMIT),
    )(sink, q, k, vt, k, vt)


def _attn_b_kernel(q_ref, k_ref, vt_ref, kc_ref, vct_ref, bias_ref, o_ref, *, rows, n_steps):
    r = pl.program_id(2)
    nq = B_QROWS * GRID_W
    nk = B_KROWS * GRID_W
    r0 = jnp.where(r < n_steps, r * B_QROWS, 0)
    ws = jnp.clip(r0 - NA_KH // 2, 0, rows - B_KROWS)
    start = pl.multiple_of(ws * GRID_W, LANES)
    lo = _lane_iota((nq, LANES)) < HEAD_DIM
    zero = jnp.zeros((nq, LANES), BF16)
    dn = (((1,), (1,)), ((), ()))
    scores = []
    for g in range(B_PAIRS):
        cols = slice(g * LANES, (g + 1) * LANES)
        q = q_ref[:, cols]
        qq = jnp.concatenate([jnp.where(lo, q, zero), jnp.where(lo, zero, q)], axis=0)
        s_loc = lax.dot_general(k_ref[pl.ds(start, nk), cols], qq, dn,
                                preferred_element_type=F32) + bias_ref[g]
        s_ctx = lax.dot_general(kc_ref[:, cols], qq, dn, preferred_element_type=F32)
        scores.append((s_loc, s_ctx))
    for g, (s_loc, s_ctx) in enumerate(scores):
        cols = slice(g * LANES, (g + 1) * LANES)
        m = jnp.maximum(jnp.max(s_loc, axis=0, keepdims=True), jnp.max(s_ctx, axis=0, keepdims=True))
        p_loc = jnp.exp2(s_loc - m)
        p_ctx = jnp.exp2(s_ctx - m)
        l = jnp.sum(p_loc, axis=0, keepdims=True) + jnp.sum(p_ctx, axis=0, keepdims=True)
        ot = (jnp.dot(vt_ref[cols, pl.ds(start, nk)], p_loc.astype(BF16), preferred_element_type=F32)
              + jnp.dot(vct_ref[cols, :], p_ctx.astype(BF16), preferred_element_type=F32)) * (1.0 / l)
        o = jnp.concatenate([ot[:HEAD_DIM, :nq], ot[HEAD_DIM:, nq:]], axis=0)
        o_ref[:, cols] = o.T.astype(BF16)


def _nbr_bias_table(rpb, rows):
    n_steps = rows // B_QROWS
    steps = [min(2, n_steps - 1), 0, 1, n_steps - 2, n_steps - 1]
    w = GRID_W
    n_heads = rpb.shape[0]
    c = np.arange(w)[None, :]
    kc = np.arange(w)[:, None]
    cs = np.clip(c - NA_KW // 2, 0, w - NA_KW)
    col_ok = (kc >= cs) & (kc < cs + NA_KW)
    onehot = (((kc - c + NA_KW - 1)[None] == np.arange(2 * NA_KW - 1)[:, None, None]) & col_ok[None])
    toe = jnp.einsum("hrd,dkc->hrkc", rpb.astype(F32), jnp.asarray(onehot, F32),
                     precision=lax.Precision.HIGHEST)
    toe = jnp.where(col_ok[None, None], toe * math.log2(math.e), MASK_VALUE)
    masked = jnp.full((n_heads, w, w), MASK_VALUE, F32)
    tabs = []
    for st in steps:
        r0 = st * B_QROWS
        ws = int(np.clip(r0 - NA_KH // 2, 0, rows - B_KROWS))
        q_cols = []
        for rq in range(B_QROWS):
            r = r0 + rq
            rs = int(np.clip(r - NA_KH // 2, 0, rows - NA_KH))
            blocks = [toe[:, ws + ki - r + NA_KH - 1] if rs <= ws + ki < rs + NA_KH else masked
                      for ki in range(B_KROWS)]
            q_cols.append(jnp.concatenate(blocks, axis=1))
        tabs.append(jnp.concatenate(q_cols, axis=2))
    tabs.append(jnp.full_like(tabs[0], MASK_VALUE))
    tab = jnp.stack(tabs)
    n_var, _, nk, nq = tab.shape
    tab = tab.reshape(n_var, n_heads // 2, 2, nk, nq)
    return jnp.transpose(tab, (0, 1, 3, 2, 4)).reshape(n_var, n_heads // 2, nk, 2 * nq)


def _attn_b(q, k, vt, bias_tab, n_batch, n_lat, n_ctx, with_ctx):
    t_rows = q.shape[0]
    rows = n_lat // GRID_W
    nq = B_QROWS * GRID_W
    nk = B_KROWS * GRID_W
    gw = B_PAIRS * LANES
    n_steps = rows // B_QROWS
    n_cstep = n_ctx // nq if with_ctx else 0
    qrow = lambda b, r: jnp.where(r < n_steps, b * n_steps + r,
                                  n_batch * n_steps + b * (n_ctx // nq) + (r - n_steps))
    ctx_blk = lambda b: (n_batch * n_lat) // n_ctx + b

    def variant(r):
        v = jnp.where(r == 0, 1, 0)
        v = jnp.where(r == 1, 2, v)
        v = jnp.where(r == n_steps - 2, 3, v)
        v = jnp.where(r == n_steps - 1, 4, v)
        return jnp.where(r >= n_steps, 5, v)

    return pl.pallas_call(
        functools.partial(_attn_b_kernel, rows=rows, n_steps=n_steps),
        grid=(n_batch, B_HEADS // (2 * B_PAIRS), n_steps + n_cstep),
        in_specs=[
            pl.BlockSpec((nq, gw), lambda b, j, r: (qrow(b, r), j)),
            pl.BlockSpec((n_lat, gw), lambda b, j, r: (b, j)),
            pl.BlockSpec((gw, n_lat), lambda b, j, r: (j, b)),
            pl.BlockSpec((n_ctx, gw), lambda b, j, r: (ctx_blk(b), j)),
            pl.BlockSpec((gw, n_ctx), lambda b, j, r: (j, ctx_blk(b))),
            pl.BlockSpec((None, B_PAIRS, nk, 2 * nq), lambda b, j, r: (variant(r), j, 0, 0)),
        ],
        out_specs=pl.BlockSpec((nq, gw), lambda b, j, r: (qrow(b, r), j)),
        out_shape=jax.ShapeDtypeStruct((t_rows if with_ctx else n_batch * n_lat, B_W), BF16),
        compiler_params=pltpu.CompilerParams(vmem_limit_bytes=VMEM_LIMIT),
    )(q, k, vt, k, vt, bias_tab)


def _attn_c_kernel(*refs, latent, n_kchunks, lam_init, n_cast=0):
    if latent:
        lam_ref, q_ref, k_ref, vt_ref, kc_ref, vct_ref, g_ref = refs[:7]
        cast_in = refs[7:7 + n_cast]
        o_ref = refs[7 + n_cast]
        cast_out = refs[8 + n_cast:8 + 2 * n_cast]
        s_a, mx_a, s_b, mx_b, s_c, mx_c, m_scr, l_scr, acc_scr = refs[8 + 2 * n_cast:]
        slots = ((s_a, mx_a), (s_b, mx_b))
        for src, dst in zip(cast_in, cast_out):
            dst[...] = src[...].astype(BF16)
    else:
        lam_ref, q_ref, kc_ref, vct_ref, g_ref, o_prev_ref, o_ref, s_c, mx_c, m_scr, l_scr, acc_scr = refs
    slot_c = (s_c, mx_c)
    tq = q_ref.shape[0]
    tk = C_KCHUNK
    lo = _lane_iota((tq, LANES)) < HEAD_DIM
    zero = jnp.zeros((tq, LANES), BF16)
    q_maps = []
    for hh in range(C_STEP_HEADS):
        q = q_ref[:, hh * LANES:(hh + 1) * LANES]
        q_maps += [jnp.where(lo, q, zero), jnp.where(lo, zero, q)]
    n_chains = len(q_maps)
    dn = (((1,), (1,)), ((), ()))

    def head_cols(ci):
        return slice((ci // 2) * LANES, (ci // 2 + 1) * LANES)

    def scores(kk_of, slot):
        s_ref, mx_ref = slot
        for ci in range(n_chains):
            s = lax.dot_general(kk_of(head_cols(ci)), q_maps[ci], dn, preferred_element_type=F32)
            s_ref[ci] = s
            mx_ref[ci] = jnp.max(s, axis=0, keepdims=True)

    def softmax_pv(slot, vvt_of):
        s_ref, mx_ref = slot
        for ci in range(n_chains):
            s = s_ref[ci]
            m = m_scr[ci]
            m_new = jnp.maximum(m, mx_ref[ci])
            alpha = jnp.exp2(m - m_new)
            p = jnp.exp2(s - m_new)
            l_scr[ci] = alpha * l_scr[ci] + jnp.sum(p, axis=0, keepdims=True)
            acc_scr[ci] = alpha * acc_scr[ci] + jnp.dot(vvt_of(head_cols(ci)), p.astype(BF16),
                                                        preferred_element_type=F32)
            m_scr[ci] = m_new

    def chunk_slice(c):
        return pl.ds(c * tk if isinstance(c, int) else pl.multiple_of(c * tk, tk), tk)

    def k_chunk(c):
        return lambda cols: k_ref[chunk_slice(c), cols]

    def vt_chunk(c):
        return lambda cols: vt_ref[cols, chunk_slice(c)]

    k_ctx = lambda cols: kc_ref[:, cols]
    vt_ctx = lambda cols: vct_ref[cols, :]

    m_scr[...] = jnp.full(m_scr.shape, -jnp.inf, F32)
    l_scr[...] = jnp.zeros(l_scr.shape, F32)
    acc_scr[...] = jnp.zeros(acc_scr.shape, F32)

    def finish():
        softmax_pv(slot_c, vt_ctx)
        lp = lam_ref[...]
        lam = (jnp.exp(jnp.sum(lp[0:1] * lp[1:2], axis=-1, keepdims=True))
               - jnp.exp(jnp.sum(lp[2:3] * lp[3:4], axis=-1, keepdims=True)) + lam_init)
        for hh in range(C_STEP_HEADS):
            c1, c2 = 2 * hh, 2 * hh + 1
            od = acc_scr[c1] * (1.0 / l_scr[c1]) - lam * (acc_scr[c2] * (1.0 / l_scr[c2]))
            ms = jnp.mean(od * od, axis=0, keepdims=True)
            on = (od * lax.rsqrt(ms + LN_EPS)).T
            o_ref[:, hh * LANES:(hh + 1) * LANES] = ((on * g_ref[...]) * (1.0 - lam_init)).astype(BF16)

    if not latent:
        scores(k_ctx, slot_c)
        finish()
        return

    scores(k_chunk(0), slots[0])

    def body(it, carry):
        c0 = C_UNROLL * it
        for u in range(C_UNROLL):
            scores(k_chunk(c0 + u + 1), slots[(u + 1) % 2])
            softmax_pv(slots[u % 2], vt_chunk(c0 + u))
        return carry

    n_it = (n_kchunks - 1) // C_UNROLL
    lax.fori_loop(0, n_it, body, 0)
    for c in range(C_UNROLL * n_it, n_kchunks):
        if c + 1 < n_kchunks:
            scores(k_chunk(c + 1), slots[(c + 1) % 2])
        else:
            scores(k_ctx, slot_c)
        softmax_pv(slots[c % 2], vt_chunk(c))
    finish()


def _attn_c(q, k, vt, lam_p, subln, lam_init, n_batch, n_lat, n_ctx, with_ctx, cast_jobs=()):
    t_rows = q.shape[0]
    hw = C_STEP_HEADS * LANES
    nch = 2 * C_STEP_HEADS
    ctx_blk = lambda b: (n_batch * n_lat) // n_ctx + b
    out_shape = jax.ShapeDtypeStruct((t_rows if with_ctx else n_batch * n_lat, C_OUT), BF16)
    params = pltpu.CompilerParams(vmem_limit_bytes=VMEM_LIMIT)
    stats = lambda tq: [pltpu.VMEM((nch, 1, tq), F32), pltpu.VMEM((nch, 1, tq), F32),
                        pltpu.VMEM((nch, LANES, tq), F32)]
    slot = lambda nk, tq: [pltpu.VMEM((nch, nk, tq), F32), pltpu.VMEM((nch, 1, tq), F32)]
    lam_spec = pl.BlockSpec((4, HEAD_DIM), lambda b, h, i: (0, 0))
    kc_spec = pl.BlockSpec((n_ctx, hw), lambda b, h, i: (ctx_blk(b), h))
    vct_spec = pl.BlockSpec((hw, n_ctx), lambda b, h, i: (h, ctx_blk(b)))
    g_spec = pl.BlockSpec((1, LANES), lambda b, h, i: (0, 0))
    g = subln.reshape(1, LANES)

    tq = C_QTILE
    n_qlat = n_lat // tq
    n_hgrp = C_HEADS // C_STEP_HEADS
    n_steps = n_batch * n_hgrp * n_qlat
    step = lambda b, h, i: (b * n_hgrp + h) * n_qlat + i
    cast_specs = []
    for w in cast_jobs:
        assert w.shape[0] % n_steps == 0 and (w.shape[0] // n_steps) % 16 == 0
        cast_specs.append(pl.BlockSpec((w.shape[0] // n_steps, w.shape[1]), lambda b, h, i: (step(b, h, i), 0)))
    outs = pl.pallas_call(
        functools.partial(_attn_c_kernel, latent=True, n_kchunks=n_lat // C_KCHUNK, lam_init=lam_init,
                          n_cast=len(cast_jobs)),
        grid=(n_batch, n_hgrp, n_qlat),
        in_specs=[
            lam_spec,
            pl.BlockSpec((tq, hw), lambda b, h, i: (b * n_qlat + i, h)),
            pl.BlockSpec((n_lat, hw), lambda b, h, i: (b, h)),
            pl.BlockSpec((hw, n_lat), lambda b, h, i: (h, b)),
            kc_spec, vct_spec, g_spec,
        ] + cast_specs,
        out_specs=[pl.BlockSpec((tq, hw), lambda b, h, i: (b * n_qlat + i, h))] + cast_specs,
        out_shape=[out_shape] + [jax.ShapeDtypeStruct(w.shape, BF16) for w in cast_jobs],
        scratch_shapes=(slot(C_KCHUNK, tq) + slot(C_KCHUNK, tq) + slot(n_ctx, tq) + stats(tq)),
        compiler_params=params,
    )(lam_p, q, k, vt, k, vt, g, *cast_jobs)
    o, casted = outs[0], list(outs[1:])
    if not with_ctx:
        return o, casted
    o = pl.pallas_call(
        functools.partial(_attn_c_kernel, latent=False, n_kchunks=0, lam_init=lam_init),
        grid=(n_batch, C_HEADS // C_STEP_HEADS, 1),
        in_specs=[
            lam_spec,
            pl.BlockSpec((n_ctx, hw), lambda b, h, i: (ctx_blk(b), h)),
            kc_spec, vct_spec, g_spec,
            pl.BlockSpec(memory_space=pl.ANY),
        ],
        out_specs=pl.BlockSpec((n_ctx, hw), lambda b, h, i: (ctx_blk(b), h)),
        out_shape=out_shape,
        scratch_shapes=slot(n_ctx, n_ctx) + stats(n_ctx),
        input_output_aliases={5: 0},
        compiler_params=params,
    )(lam_p, q, k, vt, g, o)
    return o, casted


def _bf16_truncate(x):
    bits = lax.bitcast_convert_type(x, jnp.uint32) & jnp.uint32(0xFFFF0000)
    return lax.bitcast_convert_type(bits, F32)


def _top2_router(h, wr_ref):
    h_top = _bf16_truncate(h)
    h_hi = h_top.astype(BF16)
    h_lo = (h - h_top).astype(BF16)
    a = jnp.dot(h_hi, wr_ref[...], preferred_element_type=F32)
    b = jnp.dot(h_lo, wr_ref[:, :LANES], preferred_element_type=F32)
    lg = a[:, :LANES] + a[:, LANES:] + b
    lane = _lane_iota(lg.shape)
    lanef = lane.astype(F32)
    lg = jnp.where(lane < N_EXPERTS, lg, -jnp.inf)
    m1 = jnp.max(lg, axis=-1, keepdims=True)
    i1 = jnp.min(jnp.where(lg == m1, lanef, float(LANES)), axis=-1, keepdims=True)
    lg2 = jnp.where(lanef == i1, -jnp.inf, lg)
    m2 = jnp.max(lg2, axis=-1, keepdims=True)
    i2 = jnp.min(jnp.where(lg2 == m2, lanef, float(LANES)), axis=-1, keepdims=True)
    e = jnp.exp(m2 - m1)
    w1 = 1.0 / (1.0 + e)
    w2 = e / (1.0 + e)
    out = jnp.where(lane == 0, i1, 0.0)
    out = jnp.where(lane == 1, i2, out)
    out = jnp.where(lane == 2, w1, out)
    return jnp.where(lane == 3, w2, out)


def _pack_bf16_pairs(h):
    bits = lax.bitcast_convert_type(h, jnp.uint32)
    bits = bits + (jnp.uint32(0x7FFF) + ((bits >> 16) & jnp.uint32(1)))
    half = h.shape[1] // 2
    packed = (bits[:, half:] & jnp.uint32(0xFFFF0000)) | (bits[:, :half] >> 16)
    return lax.bitcast_convert_type(packed, F32)


def _unpack_bf16_pairs(words):
    p = lax.bitcast_convert_type(words, jnp.uint32)
    lo = lax.bitcast_convert_type(p << 16, F32)
    hi = lax.bitcast_convert_type(p & jnp.uint32(0xFFFF0000), F32)
    return jnp.concatenate([lo, hi], axis=1)


def _outproj_kernel(*refs, n_in, router):
    o_refs = refs[:n_in]
    w_refs = refs[n_in:2 * n_in]
    x_ref, g_ref, lng_ref, lnb_ref, sc_ref, sh_ref = refs[2 * n_in:2 * n_in + 6]
    rest = refs[2 * n_in + 6:]
    y = jnp.dot(o_refs[0][...], w_refs[0][...], preferred_element_type=F32)
    for o_r, w_r in zip(o_refs[1:], w_refs[1:]):
        y = y + jnp.dot(o_r[...], w_r[...], preferred_element_type=F32)
    xn = _layernorm_rows(DN_ALPHA * x_ref[...] + g_ref[...] * y, lng_ref[...], lnb_ref[...])
    h2 = xn * (1.0 + sc_ref[...]) + sh_ref[...]
    if router:
        wr_ref, xo_ref, h_ref, r_ref = rest
        r_ref[...] = _top2_router(h2, wr_ref)
        h_ref[...] = _pack_bf16_pairs(h2)
    else:
        xo_ref, h_ref = rest
        h_ref[...] = h2.astype(BF16)
    xo_ref[...] = xn


def _out_proj(o_list, w_list, xs, mods_l, ln_g, ln_b, n_tiles, tiles_per_batch, n_batch, w_router=None):
    d = xs.shape[1]
    tm = ROW_TILE
    t_rows = n_tiles * tm
    grp = lambda t: jnp.minimum(t // tiles_per_batch, n_batch)
    mod = lambda k: pl.BlockSpec((None, None, 1, d), lambda t: (grp(t), k, 0, 0))
    in_specs = [pl.BlockSpec((tm, o.shape[1]), lambda t: (t, 0)) for o in o_list]
    in_specs += [pl.BlockSpec(w.shape, lambda t: (0, 0)) for w in w_list]
    in_specs += [pl.BlockSpec((tm, d), lambda t: (t, 0)), mod(2),
                 pl.BlockSpec((1, d), lambda t: (0, 0)), pl.BlockSpec((1, d), lambda t: (0, 0)),
                 mod(4), mod(3)]
    args = list(o_list) + list(w_list) + [xs, mods_l, ln_g.reshape(1, d), ln_b.reshape(1, d), mods_l, mods_l]
    out_shape = [jax.ShapeDtypeStruct((t_rows, d), F32), jax.ShapeDtypeStruct((t_rows, d), BF16)]
    out_specs = [pl.BlockSpec((tm, d), lambda t: (t, 0)), pl.BlockSpec((tm, d), lambda t: (t, 0))]
    if w_router is not None:
        out_shape[1] = jax.ShapeDtypeStruct((t_rows, d // 2), F32)
        out_specs[1] = pl.BlockSpec((tm, d // 2), lambda t: (t, 0))
        in_specs.append(pl.BlockSpec(w_router.shape, lambda t: (0, 0)))
        args.append(w_router)
        out_shape.append(jax.ShapeDtypeStruct((t_rows, LANES), F32))
        out_specs.append(pl.BlockSpec((tm, LANES), lambda t: (t, 0)))
    return pl.pallas_call(
        functools.partial(_outproj_kernel, n_in=len(o_list), router=w_router is not None),
        grid=(n_tiles,),
        in_specs=in_specs,
        out_specs=out_specs,
        out_shape=out_shape,
        compiler_params=pltpu.CompilerParams(vmem_limit_bytes=VMEM_LIMIT),
    )(*args)


def _swiglu_acc(h, wg_ref, wu_ref, wd_ref, acc_ref, c):
    g = jnp.dot(h, wg_ref[...], preferred_element_type=F32)
    u = jnp.dot(h, wu_ref[...], preferred_element_type=F32)
    a = (g * (1.0 / (1.0 + jnp.exp(-g))) * u).astype(BF16)
    part = jnp.dot(a, wd_ref[...], preferred_element_type=F32)

    @pl.when(c == 0)
    def _():
        acc_ref[...] = part

    @pl.when(c > 0)
    def _():
        acc_ref[...] += part


def _ffn_dense_kernel(h_ref, wg_ref, wu_ref, wd_ref, x_ref, g_ref, lng_ref, lnb_ref, xo_ref, acc_ref):
    c = pl.program_id(1)
    _swiglu_acc(h_ref[...], wg_ref, wu_ref, wd_ref, acc_ref, c)

    @pl.when(c == pl.num_programs(1) - 1)
    def _():
        z = DN_ALPHA * x_ref[...] + g_ref[...] * acc_ref[...]
        xo_ref[...] = _layernorm_rows(z, lng_ref[...], lnb_ref[...])


def _ffn_dense(h2, w_up, w_down, xs, mods_l, ln_g, ln_b, n_tiles, tiles_per_batch, n_batch):
    d = xs.shape[1]
    tm = ROW_TILE
    t_rows = n_tiles * tm
    ff = w_down.shape[0]
    nc = ff // FF_CHUNK
    w_mode = pl.Buffered(1) if nc == 1 else None
    grp = lambda t: jnp.minimum(t // tiles_per_batch, n_batch)
    return pl.pallas_call(
        _ffn_dense_kernel,
        grid=(n_tiles, nc),
        in_specs=[
            pl.BlockSpec((tm, d), lambda t, c: (t, 0)),
            pl.BlockSpec((d, FF_CHUNK), lambda t, c: (0, c), pipeline_mode=w_mode),
            pl.BlockSpec((d, FF_CHUNK), lambda t, c: (0, nc + c), pipeline_mode=w_mode),
            pl.BlockSpec((FF_CHUNK, d), lambda t, c: (c, 0), pipeline_mode=w_mode),
            pl.BlockSpec((tm, d), lambda t, c: (t, 0)),
            pl.BlockSpec((None, None, 1, d), lambda t, c: (grp(t), 5, 0, 0)),
            pl.BlockSpec((1, d), lambda t, c: (0, 0)),
            pl.BlockSpec((1, d), lambda t, c: (0, 0)),
        ],
        out_specs=pl.BlockSpec((tm, d), lambda t, c: (t, 0)),
        out_shape=jax.ShapeDtypeStruct((t_rows, d), F32),
        scratch_shapes=[pltpu.VMEM((tm, d), F32)],
        compiler_params=pltpu.CompilerParams(vmem_limit_bytes=VMEM_LIMIT),
    )(h2, w_up, w_up, w_down, xs, mods_l, ln_g.reshape(1, d), ln_b.reshape(1, d))


def _ffn_expert_kernel(te_ref, nu_ref, h_ref, wg_ref, wu_ref, wd_ref, y_ref, acc_ref):
    i = pl.program_id(0)
    c = pl.program_id(1)

    @pl.when(i < nu_ref[0])
    def _():
        _swiglu_acc(_unpack_bf16_pairs(h_ref[...]).astype(BF16), wg_ref, wu_ref, wd_ref, acc_ref, c)

    @pl.when(c == pl.num_programs(1) - 1)
    def _():
        y_ref[...] = _pack_bf16_pairs(acc_ref[...])


def _ffn_experts(hs, w_up, w_down, layer, tile_expert, n_used):
    p_rows = hs.shape[0]
    d = w_down.shape[-1]
    tm = MOE_TILE
    n_tiles = p_rows // tm
    ef = w_down.shape[2]
    nc = ef // EXPERT_FF_CHUNK
    chunk = lambda i, c, nu: jnp.where(i < nu[0], c, nc - 1)
    grid_spec = pltpu.PrefetchScalarGridSpec(
        num_scalar_prefetch=2,
        grid=(n_tiles, nc),
        in_specs=[
            pl.BlockSpec((tm, d // 2), lambda i, c, te, nu: (i, 0)),
            pl.BlockSpec((None, None, d, EXPERT_FF_CHUNK),
                         lambda i, c, te, nu: (layer, te[i], 0, chunk(i, c, nu))),
            pl.BlockSpec((None, None, d, EXPERT_FF_CHUNK),
                         lambda i, c, te, nu: (layer, te[i], 0, nc + chunk(i, c, nu))),
            pl.BlockSpec((None, None, EXPERT_FF_CHUNK, d),
                         lambda i, c, te, nu: (layer, te[i], chunk(i, c, nu), 0)),
        ],
        out_specs=pl.BlockSpec((tm, d // 2), lambda i, c, te, nu: (i, 0)),
        scratch_shapes=[pltpu.VMEM((tm, d), F32)],
    )
    return pl.pallas_call(
        _ffn_expert_kernel,
        grid_spec=grid_spec,
        out_shape=jax.ShapeDtypeStruct((p_rows, d // 2), F32),
        compiler_params=pltpu.CompilerParams(vmem_limit_bytes=VMEM_LIMIT),
    )(tile_expert, n_used, hs, w_up, w_up, w_down)


def _combine_kernel(y0_ref, y1_ref, r_ref, x_ref, g_ref, lng_ref, lnb_ref, xo_ref):
    r = r_ref[...]
    f = r[:, 2:3] * _unpack_bf16_pairs(y0_ref[...]) + r[:, 3:4] * _unpack_bf16_pairs(y1_ref[...])
    z = DN_ALPHA * x_ref[...] + g_ref[...] * f
    xo_ref[...] = _layernorm_rows(z, lng_ref[...], lnb_ref[...])


def _moe_combine(y0, y1, route, xs, mods_l, ln_g, ln_b, n_tiles, tiles_per_batch, n_batch):
    d = xs.shape[1]
    tm = ROW_TILE
    t_rows = n_tiles * tm
    grp = lambda t: jnp.minimum(t // tiles_per_batch, n_batch)
    row = pl.BlockSpec((tm, d), lambda t: (t, 0))
    packed = pl.BlockSpec((tm, d // 2), lambda t: (t, 0))
    return pl.pallas_call(
        _combine_kernel,
        grid=(n_tiles,),
        in_specs=[packed, packed, pl.BlockSpec((tm, LANES), lambda t: (t, 0)), row,
                  pl.BlockSpec((None, None, 1, d), lambda t: (grp(t), 5, 0, 0)),
                  pl.BlockSpec((1, d), lambda t: (0, 0)), pl.BlockSpec((1, d), lambda t: (0, 0))],
        out_specs=row,
        out_shape=jax.ShapeDtypeStruct((t_rows, d), F32),
        compiler_params=pltpu.CompilerParams(vmem_limit_bytes=VMEM_LIMIT),
    )(y0, y1, route, xs, mods_l, ln_g.reshape(1, d), ln_b.reshape(1, d))


def _routing_plan(route, n_rows):
    tm = MOE_TILE
    e_idx = route[:n_rows, 0:2].astype(jnp.int32).reshape(-1)
    onehot = (e_idx[:, None] == jnp.arange(N_EXPERTS, dtype=jnp.int32)[None, :]).astype(jnp.int32)
    csum = jnp.cumsum(onehot, axis=0)
    counts = csum[-1]
    rank = jnp.sum((csum - onehot) * onehot, axis=1)
    padded = ((counts + tm - 1) // tm) * tm
    ends = jnp.cumsum(padded)
    starts = ends - padded
    dest = starts[e_idx] + rank
    n_tiles = (2 * n_rows + N_EXPERTS * (tm - 1)) // tm
    p_rows = n_tiles * tm
    row_token = jnp.zeros((p_rows,), jnp.int32).at[dest].set(jnp.arange(2 * n_rows, dtype=jnp.int32) // 2)
    tile_start = jnp.arange(n_tiles, dtype=jnp.int32) * tm
    tile_expert = jnp.minimum(jnp.sum((tile_start[:, None] >= ends[None, :]).astype(jnp.int32), axis=1),
                              N_EXPERTS - 1)
    n_used = (ends[-1] // tm).astype(jnp.int32).reshape(1)
    last = tile_expert[jnp.maximum(n_used[0] - 1, 0)]
    tile_expert = jnp.where(jnp.arange(n_tiles) < n_used[0], tile_expert, last).astype(jnp.int32)
    return row_token, dest.reshape(n_rows, 2), tile_expert, n_used


def _rope_tables(n_lat):
    t = jnp.arange(n_lat, dtype=jnp.int32)
    row = (t // GRID_W).astype(F32)
    col = (t % GRID_W).astype(F32)
    inv = ROPE_THETA ** (-jnp.arange(0, ROPE_AXIS_DIM, 2, dtype=F32) / ROPE_AXIS_DIM)
    ar = row[:, None] * inv[None, :]
    ac = col[:, None] * inv[None, :]
    ang = jnp.concatenate([ar, ar, ac, ac], axis=-1)
    cos = jnp.tile(jnp.cos(ang), (1, LANES // HEAD_DIM))
    sin = jnp.tile(jnp.sin(ang), (1, LANES // HEAD_DIM))
    sign = jnp.where((jnp.arange(LANES) % 32) < 16, -1.0, 1.0).astype(F32)
    cos = jnp.concatenate([cos, jnp.ones((ROW_TILE, LANES), F32)], axis=0)
    sin = jnp.concatenate([sin * sign[None, :], jnp.zeros((ROW_TILE, LANES), F32)], axis=0)
    return cos, sin


def _lambda_init(layer):
    return 0.8 - 0.6 * math.exp(-0.3 * layer)


def kernel(x, c, ctx, c_ctx, w_mod, b_mod, ln_g, ln_b, w_in_ab, w_out_ab, sink_a, rpb_b, w_in_c, w_out_c,
           lam_c, subln_c, w_ffn_up, w_ffn_down, w_router, w_exp_up, w_exp_down):
    n_batch, n_lat, d = x.shape
    n_ctx = ctx.shape[1]
    assert d == D_MODEL and n_batch * n_ctx == ROW_TILE and n_lat % ROW_TILE == 0
    assert n_batch + 1 <= 8 and n_lat % C_KCHUNK == 0 and n_lat % C_QTILE == 0 and n_ctx % LANES == 0
    rows = n_lat // GRID_W
    assert rows >= 12 and rows % B_QROWS == 0
    tiles_per_batch = n_lat // ROW_TILE
    lat_tiles = n_batch * tiles_per_batch
    lat_rows = n_batch * n_lat

    xs = jnp.concatenate([x.reshape(lat_rows, d), ctx.reshape(n_batch * n_ctx, d)], axis=0)
    cond = jnp.zeros((8, d), F32).at[:n_batch].set(c).at[n_batch].set(c_ctx)
    mods = _mod_vectors(cond, w_mod, b_mod).reshape(DEPTH, 8, 6, 1, d)
    cos_t, sin_t = _rope_tables(n_lat)
    w_exp_b = None

    perm = np.array([(j + (A_HEADS // 2) * half) * HEAD_DIM + dd
                     for j in range(A_HEADS // 2) for half in range(2) for dd in range(HEAD_DIM)])

    for l in range(DEPTH):
        last = l == DEPTH - 1
        i = l // 2
        all_tiles = lat_tiles + 1
        n_tiles = lat_tiles if last else all_tiles
        mods_l = mods[l]
        if l % 2 == 0:
            w_in = w_in_ab[i]
            w_in = jnp.concatenate([w_in[:, :A_Q][:, perm], w_in[:, A_Q:]], axis=1).astype(BF16)
            q_scale = ATTN_SCALE * math.log2(math.e)
            segs = [(A_Q, True, q_scale, False), (A_KV, True, 1.0, False), (A_KV, False, 1.0, True),
                    (B_W, False, q_scale, False), (B_W, False, 1.0, False), (B_W, False, 1.0, True)]
            qa, ka, va, qb, kb, vb = _in_proj(xs, mods_l, cos_t, sin_t, w_in, segs, all_tiles,
                                              tiles_per_batch, n_batch)
            sink = sink_a[i].astype(F32) * math.log2(math.e)
            oa = _attn_a(qa, ka, va, sink, n_batch, n_lat, n_ctx, not last)
            ob = _attn_b(qb, kb, vb, _nbr_bias_table(rpb_b[i], rows), n_batch, n_lat, n_ctx, not last)
            w_out = w_out_ab[i]
            o_list = [oa, ob]
            w_list = [w_out[:A_Q][perm].astype(BF16), w_out[A_Q:].astype(BF16)]
        else:
            segs = [(C_QK, True, ATTN_SCALE * math.log2(math.e), False), (C_QK, True, 1.0, False),
                    (C_OUT, False, 1.0, True)]
            qc, kc, vc = _in_proj(xs, mods_l, cos_t, sin_t, w_in_c[i].astype(BF16), segs, all_tiles,
                                  tiles_per_batch, n_batch)
            jobs = () if w_exp_b is not None else (w_exp_up.reshape(-1, w_exp_up.shape[-1]),
                                                   w_exp_down.reshape(-1, w_exp_down.shape[-1]))
            oc, casted = _attn_c(qc, kc, vc, lam_c[i].astype(F32), subln_c[i].astype(F32), _lambda_init(l),
                                 n_batch, n_lat, n_ctx, not last, cast_jobs=jobs)
            if w_exp_b is None:
                w_exp_b = (casted[0].reshape(w_exp_up.shape), casted[1].reshape(w_exp_down.shape))
            o_list = [oc]
            w_list = [w_out_c[i].astype(BF16)]

        if l % 2 == 0:
            xs, h2 = _out_proj(o_list, w_list, xs, mods_l, ln_g[l, 0], ln_b[l, 0], n_tiles,
                               tiles_per_batch, n_batch)
            xs = _ffn_dense(h2, w_ffn_up[i].astype(BF16), w_ffn_down[i].astype(BF16), xs, mods_l,
                            ln_g[l, 1], ln_b[l, 1], n_tiles, tiles_per_batch, n_batch)
        else:
            wr = jnp.zeros((d, LANES), F32).at[:, :N_EXPERTS].set(w_router[i])
            wr_top = _bf16_truncate(wr)
            wr = jnp.concatenate([wr_top.astype(BF16), (wr - wr_top).astype(BF16)], axis=1)
            xs, h2, route = _out_proj(o_list, w_list, xs, mods_l, ln_g[l, 0], ln_b[l, 0], n_tiles,
                                      tiles_per_batch, n_batch, w_router=wr)
            n_rows = n_tiles * ROW_TILE
            row_token, dest, tile_expert, n_used = _routing_plan(route, n_rows)
            hs = jnp.take(h2, row_token, axis=0, mode="clip")
            ys = _ffn_experts(hs, w_exp_b[0], w_exp_b[1], i, tile_expert, n_used)
            y0 = jnp.take(ys, dest[:, 0], axis=0, mode="clip")
            y1 = jnp.take(ys, dest[:, 1], axis=0, mode="clip")
            xs = _moe_combine(y0, y1, route, xs, mods_l, ln_g[l, 1], ln_b[l, 1], n_tiles,
                              tiles_per_batch, n_batch)
    return xs[:lat_rows].reshape(n_batch, n_lat, d)
```

```python
import functools
import math

import jax
import jax.numpy as jnp
import numpy as np
from jax import lax
from jax.experimental import pallas as pl
from jax.experimental.pallas import tpu as pltpu

F32 = jnp.float32
BF16 = jnp.bfloat16

D_MODEL = 1024
DEPTH = 4
GRID_W = 64
HEAD_DIM = 64
LANES = 128
ATTN_SCALE = HEAD_DIM ** -0.5
A_BLOCK = 128
A_WINDOW = 128
A_HEADS = 8
A_KV_HEADS = 2
B_HEADS = 8
NA_KH = 8
NA_KW = 16
C_HEADS = 8
ROPE_THETA = 10000.0
ROPE_AXIS_DIM = HEAD_DIM // 2
FF_DIM = 2816
N_EXPERTS = 8
EXPERT_FF = 3584
LN_EPS = 1e-5
DN_ALPHA = (2 * DEPTH) ** 0.25
MASK_VALUE = -1e30
A_Q = A_HEADS * HEAD_DIM
A_KV = A_KV_HEADS * HEAD_DIM
B_W = B_HEADS * HEAD_DIM
C_QK = 1024
C_OUT = 1024

ROW_TILE = 512
MOE_TILE = 512
FF_CHUNK = 2816
EXPERT_FF_CHUNK = 1792
B_QROWS = 2
B_KROWS = B_QROWS + NA_KH
B_PAIRS = 4
C_QTILE = 256
C_KCHUNK = 512
C_UNROLL = 6
C_STEP_HEADS = 2
VMEM_LIMIT = 56 * 1024 * 1024


def _lane_iota(shape):
    return lax.broadcasted_iota(jnp.int32, shape, len(shape) - 1)


def _layernorm_rows(z, g, b):
    mu = jnp.mean(z, axis=-1, keepdims=True)
    zc = z - mu
    var = jnp.mean(zc * zc, axis=-1, keepdims=True)
    return zc * lax.rsqrt(var + LN_EPS) * g + b


def _mod_kernel(c_ref, w_ref, b_ref, o_ref):
    c = c_ref[...]
    s = c * (1.0 / (1.0 + jnp.exp(-c)))
    o_ref[0] = jnp.dot(s, w_ref[0], preferred_element_type=F32,
                       precision=lax.Precision.HIGHEST) + b_ref[0]


def _mod_vectors(cond, w_mod, b_mod):
    depth, d, n6 = w_mod.shape
    tn = 1536
    return pl.pallas_call(
        _mod_kernel,
        grid=(depth, n6 // tn),
        in_specs=[
            pl.BlockSpec((8, d), lambda l, j: (0, 0)),
            pl.BlockSpec((1, d, tn), lambda l, j: (l, 0, j)),
            pl.BlockSpec((1, 1, tn), lambda l, j: (l, 0, j)),
        ],
        out_specs=pl.BlockSpec((1, 8, tn), lambda l, j: (l, 0, j)),
        out_shape=jax.ShapeDtypeStruct((depth, 8, n6), F32),
        compiler_params=pltpu.CompilerParams(vmem_limit_bytes=VMEM_LIMIT),
    )(cond, w_mod, b_mod.reshape(depth, 1, n6))


def _rope_slab(a, cos, sin_signed, low16):
    fwd = pltpu.roll(a, LANES - 16, axis=1)
    bwd = pltpu.roll(a, 16, axis=1)
    return a * cos + jnp.where(low16, fwd, bwd) * sin_signed


def _proj_kernel(x_ref, sc_ref, sh_ref, cos_ref, sin_ref, w_ref, *out_refs, segments):
    x = x_ref[...]
    h = (x * (1.0 + sc_ref[...]) + sh_ref[...]).astype(BF16)
    cos = cos_ref[...]
    sin = sin_ref[...]
    low16 = (_lane_iota(cos.shape) % 32) < 16
    col = 0
    for o_ref, (width, rope, scale, transposed) in zip(out_refs, segments):
        for c0 in range(0, width, 256):
            cw = min(256, width - c0)
            acc = jnp.dot(h, w_ref[:, col + c0:col + c0 + cw], preferred_element_type=F32)
            if transposed:
                o_ref[c0:c0 + cw, :] = acc.T.astype(BF16)
                continue
            slabs = []
            for s0 in range(0, cw, LANES):
                a = acc[:, s0:s0 + LANES]
                if rope:
                    a = _rope_slab(a, cos, sin, low16)
                if scale != 1.0:
                    a = a * scale
                slabs.append(a.astype(BF16))
            o_ref[:, c0:c0 + cw] = slabs[0] if len(slabs) == 1 else jnp.concatenate(slabs, axis=1)
        col += width


def _in_proj(xs, mods_l, cos_t, sin_t, w, segments, n_tiles, tiles_per_batch, n_batch):
    t_rows, d = xs.shape
    tm = ROW_TILE
    grp = lambda t: jnp.minimum(t // tiles_per_batch, n_batch)
    pos = lambda t: jnp.where(t < n_batch * tiles_per_batch, t % tiles_per_batch, tiles_per_batch)
    out_shape = [jax.ShapeDtypeStruct((s[0], t_rows) if s[3] else (t_rows, s[0]), BF16) for s in segments]
    out_specs = [pl.BlockSpec((s[0], tm), lambda t: (0, t)) if s[3] else pl.BlockSpec((tm, s[0]), lambda t: (t, 0))
                 for s in segments]
    return pl.pallas_call(
        functools.partial(_proj_kernel, segments=tuple(segments)),
        grid=(n_tiles,),
        in_specs=[
            pl.BlockSpec((tm, d), lambda t: (t, 0)),
            pl.BlockSpec((None, None, 1, d), lambda t: (grp(t), 1, 0, 0)),
            pl.BlockSpec((None, None, 1, d), lambda t: (grp(t), 0, 0, 0)),
            pl.BlockSpec((tm, LANES), lambda t: (pos(t), 0)),
            pl.BlockSpec((tm, LANES), lambda t: (pos(t), 0)),
            pl.BlockSpec(w.shape, lambda t: (0, 0)),
        ],
        out_specs=out_specs,
        out_shape=out_shape,
        compiler_params=pltpu.CompilerParams(vmem_limit_bytes=VMEM_LIMIT),
    )(xs, mods_l, mods_l, cos_t, sin_t, w)


def _attn_a_kernel(sink_ref, q_ref, k_ref, vt_ref, kc_ref, vct_ref, o_ref, *, n_blocks, n_lat):
    n = pl.program_id(1)
    blk = A_BLOCK
    win = 3 * blk
    start = pl.multiple_of(jnp.clip((n - 1) * blk, 0, n_lat - win), blk)
    keys = jnp.concatenate([k_ref[pl.ds(start, win), :], kc_ref[...]], axis=0)
    vals_t = jnp.concatenate([vt_ref[:, pl.ds(start, win)], vct_ref[...]], axis=1)
    nk = keys.shape[0]
    qbase = jnp.where(n < n_blocks, n * blk, -(1 << 20))
    kidx = lax.broadcasted_iota(jnp.int32, (nk, 2 * blk), 0)
    qidx = lax.broadcasted_iota(jnp.int32, (nk, 2 * blk), 1)
    qpos = qbase + jnp.where(qidx >= blk, qidx - blk, qidx)
    in_band = jnp.abs(start + kidx - qpos) <= A_WINDOW
    bias = jnp.where((kidx >= win) | in_band, 0.0, MASK_VALUE).astype(F32)
    lo = _lane_iota((blk, LANES)) < HEAD_DIM
    zero = jnp.zeros((blk, LANES), BF16)
    col_hi = _lane_iota((1, 2 * blk)) >= blk
    dn = (((1,), (1,)), ((), ()))
    scores = []
    for j in range(A_HEADS // 2):
        qc = q_ref[:, j * LANES:(j + 1) * LANES]
        qq = jnp.concatenate([jnp.where(lo, qc, zero), jnp.where(lo, zero, qc)], axis=0)
        scores.append(lax.dot_general(keys, qq, dn, preferred_element_type=F32) + bias)
    for j, s in enumerate(scores):
        sink = jnp.where(col_hi, sink_ref[j + A_HEADS // 2], sink_ref[j])
        m = jnp.maximum(jnp.max(s, axis=0, keepdims=True), sink)
        p = jnp.exp2(s - m)
        l = jnp.sum(p, axis=0, keepdims=True) + jnp.exp2(sink - m)
        ot = jnp.dot(vals_t, p.astype(BF16), preferred_element_type=F32) * (1.0 / l)
        o = jnp.concatenate([ot[:HEAD_DIM, :blk], ot[HEAD_DIM:, blk:]], axis=0)
        o_ref[:, j * LANES:(j + 1) * LANES] = o.T.astype(BF16)


def _attn_a(q, k, vt, sink, n_batch, n_lat, n_ctx, with_ctx):
    t_rows = q.shape[0]
    blk = A_BLOCK
    nb = n_lat // blk
    ncb = n_ctx // blk if with_ctx else 0
    qrow = lambda b, n: jnp.where(n < nb, b * nb + n, n_batch * nb + b * (n_ctx // blk) + (n - nb))
    ctx_blk = lambda b: (n_batch * n_lat) // n_ctx + b
    grid_spec = pltpu.PrefetchScalarGridSpec(
        num_scalar_prefetch=1,
        grid=(n_batch, nb + ncb),
        in_specs=[
            pl.BlockSpec((blk, A_Q), lambda b, n, s: (qrow(b, n), 0)),
            pl.BlockSpec((n_lat, A_KV), lambda b, n, s: (b, 0)),
            pl.BlockSpec((A_KV, n_lat), lambda b, n, s: (0, b)),
            pl.BlockSpec((n_ctx, A_KV), lambda b, n, s: (ctx_blk(b), 0)),
            pl.BlockSpec((A_KV, n_ctx), lambda b, n, s: (0, ctx_blk(b))),
        ],
        out_specs=pl.BlockSpec((blk, A_Q), lambda b, n, s: (qrow(b, n), 0)),
    )
    return pl.pallas_call(
        functools.partial(_attn_a_kernel, n_blocks=nb, n_lat=n_lat),
        grid_spec=grid_spec,
        out_shape=jax.ShapeDtypeStruct((t_rows if with_ctx else n_batch * n_lat, A_Q), BF16),
        compiler_params=pltpu.CompilerParams(vmem_limit_bytes=VMEM_LIMIT),
    )(sink, q, k, vt, k, vt)


def _attn_b_kernel(q_ref, k_ref, vt_ref, kc_ref, vct_ref, bias_ref, o_ref, *, rows, n_steps):
    r = pl.program_id(2)
    nq = B_QROWS * GRID_W
    nk = B_KROWS * GRID_W
    r0 = jnp.where(r < n_steps, r * B_QROWS, 0)
    ws = jnp.clip(r0 - NA_KH // 2, 0, rows - B_KROWS)
    start = pl.multiple_of(ws * GRID_W, LANES)
    lo = _lane_iota((nq, LANES)) < HEAD_DIM
    zero = jnp.zeros((nq, LANES), BF16)
    dn = (((1,), (1,)), ((), ()))
    scores = []
    for g in range(B_PAIRS):
        cols = slice(g * LANES, (g + 1) * LANES)
        q = q_ref[:, cols]
        qq = jnp.concatenate([jnp.where(lo, q, zero), jnp.where(lo, zero, q)], axis=0)
        s_loc = lax.dot_general(k_ref[pl.ds(start, nk), cols], qq, dn,
                                preferred_element_type=F32) + bias_ref[g]
        s_ctx = lax.dot_general(kc_ref[:, cols], qq, dn, preferred_element_type=F32)
        scores.append((s_loc, s_ctx))
    for g, (s_loc, s_ctx) in enumerate(scores):
        cols = slice(g * LANES, (g + 1) * LANES)
        m = jnp.maximum(jnp.max(s_loc, axis=0, keepdims=True), jnp.max(s_ctx, axis=0, keepdims=True))
        p_loc = jnp.exp2(s_loc - m)
        p_ctx = jnp.exp2(s_ctx - m)
        l = jnp.sum(p_loc, axis=0, keepdims=True) + jnp.sum(p_ctx, axis=0, keepdims=True)
        ot = (jnp.dot(vt_ref[cols, pl.ds(start, nk)], p_loc.astype(BF16), preferred_element_type=F32)
              + jnp.dot(vct_ref[cols, :], p_ctx.astype(BF16), preferred_element_type=F32)) * (1.0 / l)
        o = jnp.concatenate([ot[:HEAD_DIM, :nq], ot[HEAD_DIM:, nq:]], axis=0)
        o_ref[:, cols] = o.T.astype(BF16)


def _nbr_bias_table(rpb, rows):
    n_steps = rows // B_QROWS
    steps = [min(2, n_steps - 1), 0, 1, n_steps - 2, n_steps - 1]
    w = GRID_W
    n_heads = rpb.shape[0]
    c = np.arange(w)[None, :]
    kc = np.arange(w)[:, None]
    cs = np.clip(c - NA_KW // 2, 0, w - NA_KW)
    col_ok = (kc >= cs) & (kc < cs + NA_KW)
    onehot = (((kc - c + NA_KW - 1)[None] == np.arange(2 * NA_KW - 1)[:, None, None]) & col_ok[None])
    toe = jnp.einsum("hrd,dkc->hrkc", rpb.astype(F32), jnp.asarray(onehot, F32),
                     precision=lax.Precision.HIGHEST)
    toe = jnp.where(col_ok[None, None], toe * math.log2(math.e), MASK_VALUE)
    masked = jnp.full((n_heads, w, w), MASK_VALUE, F32)
    tabs = []
    for st in steps:
        r0 = st * B_QROWS
        ws = int(np.clip(r0 - NA_KH // 2, 0, rows - B_KROWS))
        q_cols = []
        for rq in range(B_QROWS):
            r = r0 + rq
            rs = int(np.clip(r - NA_KH // 2, 0, rows - NA_KH))
            blocks = [toe[:, ws + ki - r + NA_KH - 1] if rs <= ws + ki < rs + NA_KH else masked
                      for ki in range(B_KROWS)]
            q_cols.append(jnp.concatenate(blocks, axis=1))
        tabs.append(jnp.concatenate(q_cols, axis=2))
    tabs.append(jnp.full_like(tabs[0], MASK_VALUE))
    tab = jnp.stack(tabs)
    n_var, _, nk, nq = tab.shape
    tab = tab.reshape(n_var, n_heads // 2, 2, nk, nq)
    return jnp.transpose(tab, (0, 1, 3, 2, 4)).reshape(n_var, n_heads // 2, nk, 2 * nq)


def _attn_b(q, k, vt, bias_tab, n_batch, n_lat, n_ctx, with_ctx):
    t_rows = q.shape[0]
    rows = n_lat // GRID_W
    nq = B_QROWS * GRID_W
    nk = B_KROWS * GRID_W
    gw = B_PAIRS * LANES
    n_steps = rows // B_QROWS
    n_cstep = n_ctx // nq if with_ctx else 0
    qrow = lambda b, r: jnp.where(r < n_steps, b * n_steps + r,
                                  n_batch * n_steps + b * (n_ctx // nq) + (r - n_steps))
    ctx_blk = lambda b: (n_batch * n_lat) // n_ctx + b

    def variant(r):
        v = jnp.where(r == 0, 1, 0)
        v = jnp.where(r == 1, 2, v)
        v = jnp.where(r == n_steps - 2, 3, v)
        v = jnp.where(r == n_steps - 1, 4, v)
        return jnp.where(r >= n_steps, 5, v)

    return pl.pallas_call(
        functools.partial(_attn_b_kernel, rows=rows, n_steps=n_steps),
        grid=(n_batch, B_HEADS // (2 * B_PAIRS), n_steps + n_cstep),
        in_specs=[
            pl.BlockSpec((nq, gw), lambda b, j, r: (qrow(b, r), j)),
            pl.BlockSpec((n_lat, gw), lambda b, j, r: (b, j)),
            pl.BlockSpec((gw, n_lat), lambda b, j, r: (j, b)),
            pl.BlockSpec((n_ctx, gw), lambda b, j, r: (ctx_blk(b), j)),
            pl.BlockSpec((gw, n_ctx), lambda b, j, r: (j, ctx_blk(b))),
            pl.BlockSpec((None, B_PAIRS, nk, 2 * nq), lambda b, j, r: (variant(r), j, 0, 0)),
        ],
        out_specs=pl.BlockSpec((nq, gw), lambda b, j, r: (qrow(b, r), j)),
        out_shape=jax.ShapeDtypeStruct((t_rows if with_ctx else n_batch * n_lat, B_W), BF16),
        compiler_params=pltpu.CompilerParams(vmem_limit_bytes=VMEM_LIMIT),
    )(q, k, vt, k, vt, bias_tab)


def _attn_c_kernel(*refs, latent, n_kchunks, lam_init, n_cast=0):
    if latent:
        lam_ref, q_ref, k_ref, vt_ref, kc_ref, vct_ref, g_ref = refs[:7]
        cast_in = refs[7:7 + n_cast]
        o_ref = refs[7 + n_cast]
        cast_out = refs[8 + n_cast:8 + 2 * n_cast]
        s_a, mx_a, s_b, mx_b, s_c, mx_c, m_scr, l_scr, acc_scr = refs[8 + 2 * n_cast:]
        slots = ((s_a, mx_a), (s_b, mx_b))
        for src, dst in zip(cast_in, cast_out):
            dst[...] = src[...].astype(BF16)
    else:
        lam_ref, q_ref, kc_ref, vct_ref, g_ref, o_prev_ref, o_ref, s_c, mx_c, m_scr, l_scr, acc_scr = refs
    slot_c = (s_c, mx_c)
    tq = q_ref.shape[0]
    tk = C_KCHUNK
    lo = _lane_iota((tq, LANES)) < HEAD_DIM
    zero = jnp.zeros((tq, LANES), BF16)
    q_maps = []
    for hh in range(C_STEP_HEADS):
        q = q_ref[:, hh * LANES:(hh + 1) * LANES]
        q_maps += [jnp.where(lo, q, zero), jnp.where(lo, zero, q)]
    n_chains = len(q_maps)
    dn = (((1,), (1,)), ((), ()))

    def head_cols(ci):
        return slice((ci // 2) * LANES, (ci // 2 + 1) * LANES)

    def scores(kk_of, slot):
        s_ref, mx_ref = slot
        for ci in range(n_chains):
            s = lax.dot_general(kk_of(head_cols(ci)), q_maps[ci], dn, preferred_element_type=F32)
            s_ref[ci] = s
            mx_ref[ci] = jnp.max(s, axis=0, keepdims=True)

    def softmax_pv(slot, vvt_of):
        s_ref, mx_ref = slot
        for ci in range(n_chains):
            s = s_ref[ci]
            m = m_scr[ci]
            m_new = jnp.maximum(m, mx_ref[ci])
            alpha = jnp.exp2(m - m_new)
            p = jnp.exp2(s - m_new)
            l_scr[ci] = alpha * l_scr[ci] + jnp.sum(p, axis=0, keepdims=True)
            acc_scr[ci] = alpha * acc_scr[ci] + jnp.dot(vvt_of(head_cols(ci)), p.astype(BF16),
                                                        preferred_element_type=F32)
            m_scr[ci] = m_new

    def chunk_slice(c):
        return pl.ds(c * tk if isinstance(c, int) else pl.multiple_of(c * tk, tk), tk)

    def k_chunk(c):
        return lambda cols: k_ref[chunk_slice(c), cols]

    def vt_chunk(c):
        return lambda cols: vt_ref[cols, chunk_slice(c)]

    k_ctx = lambda cols: kc_ref[:, cols]
    vt_ctx = lambda cols: vct_ref[cols, :]

    m_scr[...] = jnp.full(m_scr.shape, -jnp.inf, F32)
    l_scr[...] = jnp.zeros(l_scr.shape, F32)
    acc_scr[...] = jnp.zeros(acc_scr.shape, F32)

    def finish():
        softmax_pv(slot_c, vt_ctx)
        lp = lam_ref[...]
        lam = (jnp.exp(jnp.sum(lp[0:1] * lp[1:2], axis=-1, keepdims=True))
               - jnp.exp(jnp.sum(lp[2:3] * lp[3:4], axis=-1, keepdims=True)) + lam_init)
        for hh in range(C_STEP_HEADS):
            c1, c2 = 2 * hh, 2 * hh + 1
            od = acc_scr[c1] * (1.0 / l_scr[c1]) - lam * (acc_scr[c2] * (1.0 / l_scr[c2]))
            ms = jnp.mean(od * od, axis=0, keepdims=True)
            on = (od * lax.rsqrt(ms + LN_EPS)).T
            o_ref[:, hh * LANES:(hh + 1) * LANES] = ((on * g_ref[...]) * (1.0 - lam_init)).astype(BF16)

    if not latent:
        scores(k_ctx, slot_c)
        finish()
        return

    scores(k_chunk(0), slots[0])

    def body(it, carry):
        c0 = C_UNROLL * it
        for u in range(C_UNROLL):
            scores(k_chunk(c0 + u + 1), slots[(u + 1) % 2])
            softmax_pv(slots[u % 2], vt_chunk(c0 + u))
        return carry

    n_it = (n_kchunks - 1) // C_UNROLL
    lax.fori_loop(0, n_it, body, 0)
    for c in range(C_UNROLL * n_it, n_kchunks):
        if c + 1 < n_kchunks:
            scores(k_chunk(c + 1), slots[(c + 1) % 2])
        else:
            scores(k_ctx, slot_c)
        softmax_pv(slots[c % 2], vt_chunk(c))
    finish()


def _attn_c(q, k, vt, lam_p, subln, lam_init, n_batch, n_lat, n_ctx, with_ctx, cast_jobs=()):
    t_rows = q.shape[0]
    hw = C_STEP_HEADS * LANES
    nch = 2 * C_STEP_HEADS
    ctx_blk = lambda b: (n_batch * n_lat) // n_ctx + b
    out_shape = jax.ShapeDtypeStruct((t_rows if with_ctx else n_batch * n_lat, C_OUT), BF16)
    params = pltpu.CompilerParams(vmem_limit_bytes=VMEM_LIMIT)
    stats = lambda tq: [pltpu.VMEM((nch, 1, tq), F32), pltpu.VMEM((nch, 1, tq), F32),
                        pltpu.VMEM((nch, LANES, tq), F32)]
    slot = lambda nk, tq: [pltpu.VMEM((nch, nk, tq), F32), pltpu.VMEM((nch, 1, tq), F32)]
    lam_spec = pl.BlockSpec((4, HEAD_DIM), lambda b, h, i: (0, 0))
    kc_spec = pl.BlockSpec((n_ctx, hw), lambda b, h, i: (ctx_blk(b), h))
    vct_spec = pl.BlockSpec((hw, n_ctx), lambda b, h, i: (h, ctx_blk(b)))
    g_spec = pl.BlockSpec((1, LANES), lambda b, h, i: (0, 0))
    g = subln.reshape(1, LANES)

    tq = C_QTILE
    n_qlat = n_lat // tq
    n_hgrp = C_HEADS // C_STEP_HEADS
    n_steps = n_batch * n_hgrp * n_qlat
    step = lambda b, h, i: (b * n_hgrp + h) * n_qlat + i
    cast_specs = []
    for w in cast_jobs:
        assert w.shape[0] % n_steps == 0 and (w.shape[0] // n_steps) % 16 == 0
        cast_specs.append(pl.BlockSpec((w.shape[0] // n_steps, w.shape[1]), lambda b, h, i: (step(b, h, i), 0)))
    outs = pl.pallas_call(
        functools.partial(_attn_c_kernel, latent=True, n_kchunks=n_lat // C_KCHUNK, lam_init=lam_init,
                          n_cast=len(cast_jobs)),
        grid=(n_batch, n_hgrp, n_qlat),
        in_specs=[
            lam_spec,
            pl.BlockSpec((tq, hw), lambda b, h, i: (b * n_qlat + i, h)),
            pl.BlockSpec((n_lat, hw), lambda b, h, i: (b, h)),
            pl.BlockSpec((hw, n_lat), lambda b, h, i: (h, b)),
            kc_spec, vct_spec, g_spec,
        ] + cast_specs,
        out_specs=[pl.BlockSpec((tq, hw), lambda b, h, i: (b * n_qlat + i, h))] + cast_specs,
        out_shape=[out_shape] + [jax.ShapeDtypeStruct(w.shape, BF16) for w in cast_jobs],
        scratch_shapes=(slot(C_KCHUNK, tq) + slot(C_KCHUNK, tq) + slot(n_ctx, tq) + stats(tq)),
        compiler_params=params,
    )(lam_p, q, k, vt, k, vt, g, *cast_jobs)
    o, casted = outs[0], list(outs[1:])
    if not with_ctx:
        return o, casted
    o = pl.pallas_call(
        functools.partial(_attn_c_kernel, latent=False, n_kchunks=0, lam_init=lam_init),
        grid=(n_batch, C_HEADS // C_STEP_HEADS, 1),
        in_specs=[
            lam_spec,
            pl.BlockSpec((n_ctx, hw), lambda b, h, i: (ctx_blk(b), h)),
            kc_spec, vct_spec, g_spec,
            pl.BlockSpec(memory_space=pl.ANY),
        ],
        out_specs=pl.BlockSpec((n_ctx, hw), lambda b, h, i: (ctx_blk(b), h)),
        out_shape=out_shape,
        scratch_shapes=slot(n_ctx, n_ctx) + stats(n_ctx),
        input_output_aliases={5: 0},
        compiler_params=params,
    )(lam_p, q, k, vt, g, o)
    return o, casted


def _bf16_truncate(x):
    bits = lax.bitcast_convert_type(x, jnp.uint32) & jnp.uint32(0xFFFF0000)
    return lax.bitcast_convert_type(bits, F32)


def _top2_router(h, wr_ref):
    h_top = _bf16_truncate(h)
    h_hi = h_top.astype(BF16)
    h_lo = (h - h_top).astype(BF16)
    a = jnp.dot(h_hi, wr_ref[...], preferred_element_type=F32)
    b = jnp.dot(h_lo, wr_ref[:, :LANES], preferred_element_type=F32)
    lg = a[:, :LANES] + a[:, LANES:] + b
    lane = _lane_iota(lg.shape)
    lanef = lane.astype(F32)
    lg = jnp.where(lane < N_EXPERTS, lg, -jnp.inf)
    m1 = jnp.max(lg, axis=-1, keepdims=True)
    i1 = jnp.min(jnp.where(lg == m1, lanef, float(LANES)), axis=-1, keepdims=True)
    lg2 = jnp.where(lanef == i1, -jnp.inf, lg)
    m2 = jnp.max(lg2, axis=-1, keepdims=True)
    i2 = jnp.min(jnp.where(lg2 == m2, lanef, float(LANES)), axis=-1, keepdims=True)
    e = jnp.exp(m2 - m1)
    w1 = 1.0 / (1.0 + e)
    w2 = e / (1.0 + e)
    out = jnp.where(lane == 0, i1, 0.0)
    out = jnp.where(lane == 1, i2, out)
    out = jnp.where(lane == 2, w1, out)
    return jnp.where(lane == 3, w2, out)


def _pack_bf16_pairs(h):
    bits = lax.bitcast_convert_type(h, jnp.uint32)
    bits = bits + (jnp.uint32(0x7FFF) + ((bits >> 16) & jnp.uint32(1)))
    half = h.shape[1] // 2
    packed = (bits[:, half:] & jnp.uint32(0xFFFF0000)) | (bits[:, :half] >> 16)
    return lax.bitcast_convert_type(packed, F32)


def _unpack_bf16_pairs(words):
    p = lax.bitcast_convert_type(words, jnp.uint32)
    lo = lax.bitcast_convert_type(p << 16, F32)
    hi = lax.bitcast_convert_type(p & jnp.uint32(0xFFFF0000), F32)
    return jnp.concatenate([lo, hi], axis=1)


def _outproj_kernel(*refs, n_in, router):
    o_refs = refs[:n_in]
    w_refs = refs[n_in:2 * n_in]
    x_ref, g_ref, lng_ref, lnb_ref, sc_ref, sh_ref = refs[2 * n_in:2 * n_in + 6]
    rest = refs[2 * n_in + 6:]
    y = jnp.dot(o_refs[0][...], w_refs[0][...], preferred_element_type=F32)
    for o_r, w_r in zip(o_refs[1:], w_refs[1:]):
        y = y + jnp.dot(o_r[...], w_r[...], preferred_element_type=F32)
    xn = _layernorm_rows(DN_ALPHA * x_ref[...] + g_ref[...] * y, lng_ref[...], lnb_ref[...])
    h2 = xn * (1.0 + sc_ref[...]) + sh_ref[...]
    if router:
        wr_ref, xo_ref, h_ref, r_ref = rest
        r_ref[...] = _top2_router(h2, wr_ref)
        h_ref[...] = _pack_bf16_pairs(h2)
    else:
        xo_ref, h_ref = rest
        h_ref[...] = h2.astype(BF16)
    xo_ref[...] = xn


def _out_proj(o_list, w_list, xs, mods_l, ln_g, ln_b, n_tiles, tiles_per_batch, n_batch, w_router=None):
    d = xs.shape[1]
    tm = ROW_TILE
    t_rows = n_tiles * tm
    grp = lambda t: jnp.minimum(t // tiles_per_batch, n_batch)
    mod = lambda k: pl.BlockSpec((None, None, 1, d), lambda t: (grp(t), k, 0, 0))
    in_specs = [pl.BlockSpec((tm, o.shape[1]), lambda t: (t, 0)) for o in o_list]
    in_specs += [pl.BlockSpec(w.shape, lambda t: (0, 0)) for w in w_list]
    in_specs += [pl.BlockSpec((tm, d), lambda t: (t, 0)), mod(2),
                 pl.BlockSpec((1, d), lambda t: (0, 0)), pl.BlockSpec((1, d), lambda t: (0, 0)),
                 mod(4), mod(3)]
    args = list(o_list) + list(w_list) + [xs, mods_l, ln_g.reshape(1, d), ln_b.reshape(1, d), mods_l, mods_l]
    out_shape = [jax.ShapeDtypeStruct((t_rows, d), F32), jax.ShapeDtypeStruct((t_rows, d), BF16)]
    out_specs = [pl.BlockSpec((tm, d), lambda t: (t, 0)), pl.BlockSpec((tm, d), lambda t: (t, 0))]
    if w_router is not None:
        out_shape[1] = jax.ShapeDtypeStruct((t_rows, d // 2), F32)
        out_specs[1] = pl.BlockSpec((tm, d // 2), lambda t: (t, 0))
        in_specs.append(pl.BlockSpec(w_router.shape, lambda t: (0, 0)))
        args.append(w_router)
        out_shape.append(jax.ShapeDtypeStruct((t_rows, LANES), F32))
        out_specs.append(pl.BlockSpec((tm, LANES), lambda t: (t, 0)))
    return pl.pallas_call(
        functools.partial(_outproj_kernel, n_in=len(o_list), router=w_router is not None),
        grid=(n_tiles,),
        in_specs=in_specs,
        out_specs=out_specs,
        out_shape=out_shape,
        compiler_params=pltpu.CompilerParams(vmem_limit_bytes=VMEM_LIMIT),
    )(*args)


def _swiglu_acc(h, wg_ref, wu_ref, wd_ref, acc_ref, c):
    g = jnp.dot(h, wg_ref[...], preferred_element_type=F32)
    u = jnp.dot(h, wu_ref[...], preferred_element_type=F32)
    a = (g * (1.0 / (1.0 + jnp.exp(-g))) * u).astype(BF16)
    part = jnp.dot(a, wd_ref[...], preferred_element_type=F32)

    @pl.when(c == 0)
    def _():
        acc_ref[...] = part

    @pl.when(c > 0)
    def _():
        acc_ref[...] += part


def _ffn_dense_kernel(h_ref, wg_ref, wu_ref, wd_ref, x_ref, g_ref, lng_ref, lnb_ref, xo_ref, acc_ref):
    c = pl.program_id(1)
    _swiglu_acc(h_ref[...], wg_ref, wu_ref, wd_ref, acc_ref, c)

    @pl.when(c == pl.num_programs(1) - 1)
    def _():
        z = DN_ALPHA * x_ref[...] + g_ref[...] * acc_ref[...]
        xo_ref[...] = _layernorm_rows(z, lng_ref[...], lnb_ref[...])


def _ffn_dense(h2, w_up, w_down, xs, mods_l, ln_g, ln_b, n_tiles, tiles_per_batch, n_batch):
    d = xs.shape[1]
    tm = ROW_TILE
    t_rows = n_tiles * tm
    ff = w_down.shape[0]
    nc = ff // FF_CHUNK
    w_mode = pl.Buffered(1) if nc == 1 else None
    grp = lambda t: jnp.minimum(t // tiles_per_batch, n_batch)
    return pl.pallas_call(
        _ffn_dense_kernel,
        grid=(n_tiles, nc),
        in_specs=[
            pl.BlockSpec((tm, d), lambda t, c: (t, 0)),
            pl.BlockSpec((d, FF_CHUNK), lambda t, c: (0, c), pipeline_mode=w_mode),
            pl.BlockSpec((d, FF_CHUNK), lambda t, c: (0, nc + c), pipeline_mode=w_mode),
            pl.BlockSpec((FF_CHUNK, d), lambda t, c: (c, 0), pipeline_mode=w_mode),
            pl.BlockSpec((tm, d), lambda t, c: (t, 0)),
            pl.BlockSpec((None, None, 1, d), lambda t, c: (grp(t), 5, 0, 0)),
            pl.BlockSpec((1, d), lambda t, c: (0, 0)),
            pl.BlockSpec((1, d), lambda t, c: (0, 0)),
        ],
        out_specs=pl.BlockSpec((tm, d), lambda t, c: (t, 0)),
        out_shape=jax.ShapeDtypeStruct((t_rows, d), F32),
        scratch_shapes=[pltpu.VMEM((tm, d), F32)],
        compiler_params=pltpu.CompilerParams(vmem_limit_bytes=VMEM_LIMIT),
    )(h2, w_up, w_up, w_down, xs, mods_l, ln_g.reshape(1, d), ln_b.reshape(1, d))


def _outproj_ffn_kernel(oa_ref, ob_ref, w1_ref, w2_ref, x_ref, g1_ref, lng1_ref, lnb1_ref, sc2_ref, sh2_ref,
                        wg_ref, wu_ref, wd_ref, g2_ref, lng2_ref, lnb2_ref, xo_ref):
    y = (jnp.dot(oa_ref[...], w1_ref[...], preferred_element_type=F32)
         + jnp.dot(ob_ref[...], w2_ref[...], preferred_element_type=F32))
    x1 = _layernorm_rows(DN_ALPHA * x_ref[...] + g1_ref[...] * y, lng1_ref[...], lnb1_ref[...])
    h2 = (x1 * (1.0 + sc2_ref[...]) + sh2_ref[...]).astype(BF16)
    g = jnp.dot(h2, wg_ref[...], preferred_element_type=F32)
    u = jnp.dot(h2, wu_ref[...], preferred_element_type=F32)
    a = (g * (1.0 / (1.0 + jnp.exp(-g))) * u).astype(BF16)
    f = jnp.dot(a, wd_ref[...], preferred_element_type=F32)
    xo_ref[...] = _layernorm_rows(DN_ALPHA * x1 + g2_ref[...] * f, lng2_ref[...], lnb2_ref[...])


def _out_proj_ffn(oa, ob, w1, w2, w_up, w_down, xs, mods_l, ln_g, ln_b, n_tiles, tiles_per_batch, n_batch):
    d = xs.shape[1]
    tm = ROW_TILE
    ff = w_down.shape[0]
    grp = lambda t: jnp.minimum(t // tiles_per_batch, n_batch)
    mod = lambda k: pl.BlockSpec((None, None, 1, d), lambda t: (grp(t), k, 0, 0))
    vec = pl.BlockSpec((1, d), lambda t: (0, 0))
    const = lambda shape, col=0: pl.BlockSpec(shape, lambda t: (0, col), pipeline_mode=pl.Buffered(1))
    return pl.pallas_call(
        _outproj_ffn_kernel,
        grid=(n_tiles,),
        in_specs=[
            pl.BlockSpec((tm, oa.shape[1]), lambda t: (t, 0)), pl.BlockSpec((tm, ob.shape[1]), lambda t: (t, 0)),
            const(w1.shape), const(w2.shape),
            pl.BlockSpec((tm, d), lambda t: (t, 0)), mod(2), vec, vec, mod(4), mod(3),
            const((d, ff)), const((d, ff), 1), const((ff, d)),
            mod(5), vec, vec,
        ],
        out_specs=pl.BlockSpec((tm, d), lambda t: (t, 0)),
        out_shape=jax.ShapeDtypeStruct((n_tiles * tm, d), F32),
        compiler_params=pltpu.CompilerParams(vmem_limit_bytes=VMEM_LIMIT),
    )(oa, ob, w1, w2, xs, mods_l, ln_g[0].reshape(1, d), ln_b[0].reshape(1, d), mods_l, mods_l,
      w_up, w_up, w_down, mods_l, ln_g[1].reshape(1, d), ln_b[1].reshape(1, d))


def _ffn_expert_kernel(te_ref, nu_ref, h_ref, wg_ref, wu_ref, wd_ref, y_ref, acc_ref):
    i = pl.program_id(0)
    c = pl.program_id(1)

    @pl.when(i < nu_ref[0])
    def _():
        _swiglu_acc(_unpack_bf16_pairs(h_ref[...]).astype(BF16), wg_ref, wu_ref, wd_ref, acc_ref, c)

    @pl.when(c == pl.num_programs(1) - 1)
    def _():
        y_ref[...] = _pack_bf16_pairs(acc_ref[...])


def _ffn_experts(hs, w_up, w_down, layer, tile_expert, n_used):
    p_rows = hs.shape[0]
    d = w_down.shape[-1]
    tm = MOE_TILE
    n_tiles = p_rows // tm
    ef = w_down.shape[2]
    nc = ef // EXPERT_FF_CHUNK
    chunk = lambda i, c, nu: jnp.where(i < nu[0], c, nc - 1)
    grid_spec = pltpu.PrefetchScalarGridSpec(
        num_scalar_prefetch=2,
        grid=(n_tiles, nc),
        in_specs=[
            pl.BlockSpec((tm, d // 2), lambda i, c, te, nu: (i, 0)),
            pl.BlockSpec((None, None, d, EXPERT_FF_CHUNK),
                         lambda i, c, te, nu: (layer, te[i], 0, chunk(i, c, nu))),
            pl.BlockSpec((None, None, d, EXPERT_FF_CHUNK),
                         lambda i, c, te, nu: (layer, te[i], 0, nc + chunk(i, c, nu))),
            pl.BlockSpec((None, None, EXPERT_FF_CHUNK, d),
                         lambda i, c, te, nu: (layer, te[i], chunk(i, c, nu), 0)),
        ],
        out_specs=pl.BlockSpec((tm, d // 2), lambda i, c, te, nu: (i, 0)),
        scratch_shapes=[pltpu.VMEM((tm, d), F32)],
    )
    return pl.pallas_call(
        _ffn_expert_kernel,
        grid_spec=grid_spec,
        out_shape=jax.ShapeDtypeStruct((p_rows, d // 2), F32),
        compiler_params=pltpu.CompilerParams(vmem_limit_bytes=VMEM_LIMIT),
    )(tile_expert, n_used, hs, w_up, w_up, w_down)


def _combine_kernel(y0_ref, y1_ref, r_ref, x_ref, g_ref, lng_ref, lnb_ref, xo_ref):
    r = r_ref[...]
    f = r[:, 2:3] * _unpack_bf16_pairs(y0_ref[...]) + r[:, 3:4] * _unpack_bf16_pairs(y1_ref[...])
    z = DN_ALPHA * x_ref[...] + g_ref[...] * f
    xo_ref[...] = _layernorm_rows(z, lng_ref[...], lnb_ref[...])


def _moe_combine(y0, y1, route, xs, mods_l, ln_g, ln_b, n_tiles, tiles_per_batch, n_batch):
    d = xs.shape[1]
    tm = ROW_TILE
    t_rows = n_tiles * tm
    grp = lambda t: jnp.minimum(t // tiles_per_batch, n_batch)
    row = pl.BlockSpec((tm, d), lambda t: (t, 0))
    packed = pl.BlockSpec((tm, d // 2), lambda t: (t, 0))
    return pl.pallas_call(
        _combine_kernel,
        grid=(n_tiles,),
        in_specs=[packed, packed, pl.BlockSpec((tm, LANES), lambda t: (t, 0)), row,
                  pl.BlockSpec((None, None, 1, d), lambda t: (grp(t), 5, 0, 0)),
                  pl.BlockSpec((1, d), lambda t: (0, 0)), pl.BlockSpec((1, d), lambda t: (0, 0))],
        out_specs=row,
        out_shape=jax.ShapeDtypeStruct((t_rows, d), F32),
        compiler_params=pltpu.CompilerParams(vmem_limit_bytes=VMEM_LIMIT),
    )(y0, y1, route, xs, mods_l, ln_g.reshape(1, d), ln_b.reshape(1, d))


def _routing_plan(route, n_rows):
    tm = MOE_TILE
    e_idx = route[:n_rows, 0:2].astype(jnp.int32).reshape(-1)
    onehot = (e_idx[:, None] == jnp.arange(N_EXPERTS, dtype=jnp.int32)[None, :]).astype(jnp.int32)
    csum = jnp.cumsum(onehot, axis=0)
    counts = csum[-1]
    rank = jnp.sum((csum - onehot) * onehot, axis=1)
    padded = ((counts + tm - 1) // tm) * tm
    ends = jnp.cumsum(padded)
    starts = ends - padded
    dest = starts[e_idx] + rank
    n_tiles = (2 * n_rows + N_EXPERTS * (tm - 1)) // tm
    p_rows = n_tiles * tm
    row_token = jnp.zeros((p_rows,), jnp.int32).at[dest].set(jnp.arange(2 * n_rows, dtype=jnp.int32) // 2)
    tile_start = jnp.arange(n_tiles, dtype=jnp.int32) * tm
    tile_expert = jnp.minimum(jnp.sum((tile_start[:, None] >= ends[None, :]).astype(jnp.int32), axis=1),
                              N_EXPERTS - 1)
    n_used = (ends[-1] // tm).astype(jnp.int32).reshape(1)
    last = tile_expert[jnp.maximum(n_used[0] - 1, 0)]
    tile_expert = jnp.where(jnp.arange(n_tiles) < n_used[0], tile_expert, last).astype(jnp.int32)
    return row_token, dest.reshape(n_rows, 2), tile_expert, n_used


def _rope_tables(n_lat):
    t = jnp.arange(n_lat, dtype=jnp.int32)
    row = (t // GRID_W).astype(F32)
    col = (t % GRID_W).astype(F32)
    inv = ROPE_THETA ** (-jnp.arange(0, ROPE_AXIS_DIM, 2, dtype=F32) / ROPE_AXIS_DIM)
    ar = row[:, None] * inv[None, :]
    ac = col[:, None] * inv[None, :]
    ang = jnp.concatenate([ar, ar, ac, ac], axis=-1)
    cos = jnp.tile(jnp.cos(ang), (1, LANES // HEAD_DIM))
    sin = jnp.tile(jnp.sin(ang), (1, LANES // HEAD_DIM))
    sign = jnp.where((jnp.arange(LANES) % 32) < 16, -1.0, 1.0).astype(F32)
    cos = jnp.concatenate([cos, jnp.ones((ROW_TILE, LANES), F32)], axis=0)
    sin = jnp.concatenate([sin * sign[None, :], jnp.zeros((ROW_TILE, LANES), F32)], axis=0)
    return cos, sin


def _lambda_init(layer):
    return 0.8 - 0.6 * math.exp(-0.3 * layer)


def kernel(x, c, ctx, c_ctx, w_mod, b_mod, ln_g, ln_b, w_in_ab, w_out_ab, sink_a, rpb_b, w_in_c, w_out_c,
           lam_c, subln_c, w_ffn_up, w_ffn_down, w_router, w_exp_up, w_exp_down):
    n_batch, n_lat, d = x.shape
    n_ctx = ctx.shape[1]
    assert d == D_MODEL and n_batch * n_ctx == ROW_TILE and n_lat % ROW_TILE == 0
    assert n_batch + 1 <= 8 and n_lat % C_KCHUNK == 0 and n_lat % C_QTILE == 0 and n_ctx % LANES == 0
    rows = n_lat // GRID_W
    assert rows >= 12 and rows % B_QROWS == 0
    tiles_per_batch = n_lat // ROW_TILE
    lat_tiles = n_batch * tiles_per_batch
    lat_rows = n_batch * n_lat

    xs = jnp.concatenate([x.reshape(lat_rows, d), ctx.reshape(n_batch * n_ctx, d)], axis=0)
    cond = jnp.zeros((8, d), F32).at[:n_batch].set(c).at[n_batch].set(c_ctx)
    mods = _mod_vectors(cond, w_mod, b_mod).reshape(DEPTH, 8, 6, 1, d)
    cos_t, sin_t = _rope_tables(n_lat)
    w_exp_b = None

    perm = np.array([(j + (A_HEADS // 2) * half) * HEAD_DIM + dd
                     for j in range(A_HEADS // 2) for half in range(2) for dd in range(HEAD_DIM)])

    for l in range(DEPTH):
        last = l == DEPTH - 1
        i = l // 2
        all_tiles = lat_tiles + 1
        n_tiles = lat_tiles if last else all_tiles
        mods_l = mods[l]
        if l % 2 == 0:
            w_in = w_in_ab[i]
            w_in = jnp.concatenate([w_in[:, :A_Q][:, perm], w_in[:, A_Q:]], axis=1).astype(BF16)
            q_scale = ATTN_SCALE * math.log2(math.e)
            segs = [(A_Q, True, q_scale, False), (A_KV, True, 1.0, False), (A_KV, False, 1.0, True),
                    (B_W, False, q_scale, False), (B_W, False, 1.0, False), (B_W, False, 1.0, True)]
            qa, ka, va, qb, kb, vb = _in_proj(xs, mods_l, cos_t, sin_t, w_in, segs, all_tiles,
                                              tiles_per_batch, n_batch)
            sink = sink_a[i].astype(F32) * math.log2(math.e)
            oa = _attn_a(qa, ka, va, sink, n_batch, n_lat, n_ctx, not last)
            ob = _attn_b(qb, kb, vb, _nbr_bias_table(rpb_b[i], rows), n_batch, n_lat, n_ctx, not last)
            w_out = w_out_ab[i]
            o_list = [oa, ob]
            w_list = [w_out[:A_Q][perm].astype(BF16), w_out[A_Q:].astype(BF16)]
        else:
            segs = [(C_QK, True, ATTN_SCALE * math.log2(math.e), False), (C_QK, True, 1.0, False),
                    (C_OUT, False, 1.0, True)]
            qc, kc, vc = _in_proj(xs, mods_l, cos_t, sin_t, w_in_c[i].astype(BF16), segs, all_tiles,
                                  tiles_per_batch, n_batch)
            jobs = () if w_exp_b is not None else (w_exp_up.reshape(-1, w_exp_up.shape[-1]),
                                                   w_exp_down.reshape(-1, w_exp_down.shape[-1]))
            oc, casted = _attn_c(qc, kc, vc, lam_c[i].astype(F32), subln_c[i].astype(F32), _lambda_init(l),
                                 n_batch, n_lat, n_ctx, not last, cast_jobs=jobs)
            if w_exp_b is None:
                w_exp_b = (casted[0].reshape(w_exp_up.shape), casted[1].reshape(w_exp_down.shape))
            o_list = [oc]
            w_list = [w_out_c[i].astype(BF16)]

        if l % 2 == 0:
            xs = _out_proj_ffn(o_list[0], o_list[1], w_list[0], w_list[1], w_ffn_up[i].astype(BF16),
                               w_ffn_down[i].astype(BF16), xs, mods_l, ln_g[l], ln_b[l], n_tiles,
                               tiles_per_batch, n_batch)
        else:
            wr = jnp.zeros((d, LANES), F32).at[:, :N_EXPERTS].set(w_router[i])
            wr_top = _bf16_truncate(wr)
            wr = jnp.concatenate([wr_top.astype(BF16), (wr - wr_top).astype(BF16)], axis=1)
            xs, h2, route = _out_proj(o_list, w_list, xs, mods_l, ln_g[l, 0], ln_b[l, 0], n_tiles,
                                      tiles_per_batch, n_batch, w_router=wr)
            n_rows = n_tiles * ROW_TILE
            row_token, dest, tile_expert, n_used = _routing_plan(route, n_rows)
            hs = jnp.take(h2, row_token, axis=0, mode="clip")
            ys = _ffn_experts(hs, w_exp_b[0], w_exp_b[1], i, tile_expert, n_used)
            y0 = jnp.take(ys, dest[:, 0], axis=0, mode="clip")
            y1 = jnp.take(ys, dest[:, 1], axis=0, mode="clip")
            xs = _moe_combine(y0, y1, route, xs, mods_l, ln_g[l, 1], ln_b[l, 1], n_tiles,
                              tiles_per_batch, n_batch)
    return xs[:lat_rows].reshape(n_batch, n_lat, d)
```

```python
import functools
import math

import jax
import jax.numpy as jnp
import numpy as np
from jax import lax
from jax.experimental import pallas as pl
from jax.experimental.pallas import tpu as pltpu

F32 = jnp.float32
BF16 = jnp.bfloat16

D_MODEL = 1024
DEPTH = 4
GRID_W = 64
HEAD_DIM = 64
LANES = 128
ATTN_SCALE = HEAD_DIM ** -0.5
A_BLOCK = 128
A_WINDOW = 128
A_HEADS = 8
A_KV_HEADS = 2
B_HEADS = 8
NA_KH = 8
NA_KW = 16
C_HEADS = 8
ROPE_THETA = 10000.0
ROPE_AXIS_DIM = HEAD_DIM // 2
FF_DIM = 2816
N_EXPERTS = 8
EXPERT_FF = 3584
LN_EPS = 1e-5
DN_ALPHA = (2 * DEPTH) ** 0.25
MASK_VALUE = -1e30
A_Q = A_HEADS * HEAD_DIM
A_KV = A_KV_HEADS * HEAD_DIM
B_W = B_HEADS * HEAD_DIM
C_QK = 1024
C_OUT = 1024

ROW_TILE = 512
MOE_TILE = 512
EXPERT_FF_CHUNK = 1792
B_QROWS = 2
B_KROWS = B_QROWS + NA_KH
B_PAIRS = 4
C_QTILE = 256
C_KCHUNK = 512
C_UNROLL = 6
C_STEP_HEADS = 2
VMEM_LIMIT = 56 * 1024 * 1024


def _lane_iota(shape):
    return lax.broadcasted_iota(jnp.int32, shape, len(shape) - 1)


def _layernorm_rows(z, g, b):
    mu = jnp.mean(z, axis=-1, keepdims=True)
    zc = z - mu
    var = jnp.mean(zc * zc, axis=-1, keepdims=True)
    return zc * lax.rsqrt(var + LN_EPS) * g + b


def _mod_kernel(c_ref, w_ref, b_ref, o_ref):
    c = c_ref[...]
    s = c * (1.0 / (1.0 + jnp.exp(-c)))
    o_ref[0] = jnp.dot(s, w_ref[0], preferred_element_type=F32,
                       precision=lax.Precision.HIGHEST) + b_ref[0]


def _mod_vectors(cond, w_mod, b_mod):
    depth, d, n6 = w_mod.shape
    tn = 1536
    return pl.pallas_call(
        _mod_kernel,
        grid=(depth, n6 // tn),
        in_specs=[
            pl.BlockSpec((8, d), lambda l, j: (0, 0)),
            pl.BlockSpec((1, d, tn), lambda l, j: (l, 0, j)),
            pl.BlockSpec((1, 1, tn), lambda l, j: (l, 0, j)),
        ],
        out_specs=pl.BlockSpec((1, 8, tn), lambda l, j: (l, 0, j)),
        out_shape=jax.ShapeDtypeStruct((depth, 8, n6), F32),
        compiler_params=pltpu.CompilerParams(vmem_limit_bytes=VMEM_LIMIT),
    )(cond, w_mod, b_mod.reshape(depth, 1, n6))


def _rope_slab(a, cos, sin_signed, low16):
    fwd = pltpu.roll(a, LANES - 16, axis=1)
    bwd = pltpu.roll(a, 16, axis=1)
    return a * cos + jnp.where(low16, fwd, bwd) * sin_signed


def _proj_kernel(x_ref, sc_ref, sh_ref, cos_ref, sin_ref, w_ref, *out_refs, segments):
    x = x_ref[...]
    h = (x * (1.0 + sc_ref[...]) + sh_ref[...]).astype(BF16)
    cos = cos_ref[...]
    sin = sin_ref[...]
    low16 = (_lane_iota(cos.shape) % 32) < 16
    col = 0
    for o_ref, (width, rope, scale, transposed) in zip(out_refs, segments):
        for c0 in range(0, width, 256):
            cw = min(256, width - c0)
            acc = jnp.dot(h, w_ref[:, col + c0:col + c0 + cw], preferred_element_type=F32)
            if transposed:
                o_ref[c0:c0 + cw, :] = acc.T.astype(BF16)
                continue
            slabs = []
            for s0 in range(0, cw, LANES):
                a = acc[:, s0:s0 + LANES]
                if rope:
                    a = _rope_slab(a, cos, sin, low16)
                if scale != 1.0:
                    a = a * scale
                slabs.append(a.astype(BF16))
            o_ref[:, c0:c0 + cw] = slabs[0] if len(slabs) == 1 else jnp.concatenate(slabs, axis=1)
        col += width


def _in_proj(xs, mods_l, cos_t, sin_t, w, segments, n_tiles, tiles_per_batch, n_batch):
    t_rows, d = xs.shape
    tm = ROW_TILE
    grp = lambda t: jnp.minimum(t // tiles_per_batch, n_batch)
    pos = lambda t: jnp.where(t < n_batch * tiles_per_batch, t % tiles_per_batch, tiles_per_batch)
    out_shape = [jax.ShapeDtypeStruct((s[0], t_rows) if s[3] else (t_rows, s[0]), BF16) for s in segments]
    out_specs = [pl.BlockSpec((s[0], tm), lambda t: (0, t)) if s[3] else pl.BlockSpec((tm, s[0]), lambda t: (t, 0))
                 for s in segments]
    return pl.pallas_call(
        functools.partial(_proj_kernel, segments=tuple(segments)),
        grid=(n_tiles,),
        in_specs=[
            pl.BlockSpec((tm, d), lambda t: (t, 0)),
            pl.BlockSpec((None, None, 1, d), lambda t: (grp(t), 1, 0, 0)),
            pl.BlockSpec((None, None, 1, d), lambda t: (grp(t), 0, 0, 0)),
            pl.BlockSpec((tm, LANES), lambda t: (pos(t), 0)),
            pl.BlockSpec((tm, LANES), lambda t: (pos(t), 0)),
            pl.BlockSpec(w.shape, lambda t: (0, 0)),
        ],
        out_specs=out_specs,
        out_shape=out_shape,
        compiler_params=pltpu.CompilerParams(vmem_limit_bytes=VMEM_LIMIT),
    )(xs, mods_l, mods_l, cos_t, sin_t, w)


def _attn_a_kernel(sink_ref, q_ref, k_ref, vt_ref, kc_ref, vct_ref, o_ref, *, n_blocks, n_lat):
    n = pl.program_id(1)
    blk = A_BLOCK
    win = 3 * blk
    start = pl.multiple_of(jnp.clip((n - 1) * blk, 0, n_lat - win), blk)
    keys = jnp.concatenate([k_ref[pl.ds(start, win), :], kc_ref[...]], axis=0)
    vals_t = jnp.concatenate([vt_ref[:, pl.ds(start, win)], vct_ref[...]], axis=1)
    nk = keys.shape[0]
    qbase = jnp.where(n < n_blocks, n * blk, -(1 << 20))
    kidx = lax.broadcasted_iota(jnp.int32, (nk, 2 * blk), 0)
    qidx = lax.broadcasted_iota(jnp.int32, (nk, 2 * blk), 1)
    qpos = qbase + jnp.where(qidx >= blk, qidx - blk, qidx)
    in_band = jnp.abs(start + kidx - qpos) <= A_WINDOW
    bias = jnp.where((kidx >= win) | in_band, 0.0, MASK_VALUE).astype(F32)
    lo = _lane_iota((blk, LANES)) < HEAD_DIM
    zero = jnp.zeros((blk, LANES), BF16)
    col_hi = _lane_iota((1, 2 * blk)) >= blk
    dn = (((1,), (1,)), ((), ()))
    scores = []
    for j in range(A_HEADS // 2):
        qc = q_ref[:, j * LANES:(j + 1) * LANES]
        qq = jnp.concatenate([jnp.where(lo, qc, zero), jnp.where(lo, zero, qc)], axis=0)
        scores.append(lax.dot_general(keys, qq, dn, preferred_element_type=F32) + bias)
    for j, s in enumerate(scores):
        sink = jnp.where(col_hi, sink_ref[j + A_HEADS // 2], sink_ref[j])
        m = jnp.maximum(jnp.max(s, axis=0, keepdims=True), sink)
        p = jnp.exp2(s - m)
        l = jnp.sum(p, axis=0, keepdims=True) + jnp.exp2(sink - m)
        ot = jnp.dot(vals_t, p.astype(BF16), preferred_element_type=F32) * (1.0 / l)
        o = jnp.concatenate([ot[:HEAD_DIM, :blk], ot[HEAD_DIM:, blk:]], axis=0)
        o_ref[:, j * LANES:(j + 1) * LANES] = o.T.astype(BF16)


def _attn_a(q, k, vt, sink, n_batch, n_lat, n_ctx, with_ctx):
    t_rows = q.shape[0]
    blk = A_BLOCK
    nb = n_lat // blk
    ncb = n_ctx // blk if with_ctx else 0
    qrow = lambda b, n: jnp.where(n < nb, b * nb + n, n_batch * nb + b * (n_ctx // blk) + (n - nb))
    ctx_blk = lambda b: (n_batch * n_lat) // n_ctx + b
    grid_spec = pltpu.PrefetchScalarGridSpec(
        num_scalar_prefetch=1,
        grid=(n_batch, nb + ncb),
        in_specs=[
            pl.BlockSpec((blk, A_Q), lambda b, n, s: (qrow(b, n), 0)),
            pl.BlockSpec((n_lat, A_KV), lambda b, n, s: (b, 0)),
            pl.BlockSpec((A_KV, n_lat), lambda b, n, s: (0, b)),
            pl.BlockSpec((n_ctx, A_KV), lambda b, n, s: (ctx_blk(b), 0)),
            pl.BlockSpec((A_KV, n_ctx), lambda b, n, s: (0, ctx_blk(b))),
        ],
        out_specs=pl.BlockSpec((blk, A_Q), lambda b, n, s: (qrow(b, n), 0)),
    )
    return pl.pallas_call(
        functools.partial(_attn_a_kernel, n_blocks=nb, n_lat=n_lat),
        grid_spec=grid_spec,
        out_shape=jax.ShapeDtypeStruct((t_rows if with_ctx else n_batch * n_lat, A_Q), BF16),
        compiler_params=pltpu.CompilerParams(vmem_limit_bytes=VMEM_LIMIT),
    )(sink, q, k, vt, k, vt)


def _attn_b_kernel(q_ref, k_ref, vt_ref, kc_ref, vct_ref, bias_ref, o_ref, *, rows, n_steps):
    r = pl.program_id(2)
    nq = B_QROWS * GRID_W
    nk = B_KROWS * GRID_W
    r0 = jnp.where(r < n_steps, r * B_QROWS, 0)
    ws = jnp.clip(r0 - NA_KH // 2, 0, rows - B_KROWS)
    start = pl.multiple_of(ws * GRID_W, LANES)
    lo = _lane_iota((nq, LANES)) < HEAD_DIM
    zero = jnp.zeros((nq, LANES), BF16)
    dn = (((1,), (1,)), ((), ()))
    scores = []
    for g in range(B_PAIRS):
        cols = slice(g * LANES, (g + 1) * LANES)
        q = q_ref[:, cols]
        qq = jnp.concatenate([jnp.where(lo, q, zero), jnp.where(lo, zero, q)], axis=0)
        s_loc = lax.dot_general(k_ref[pl.ds(start, nk), cols], qq, dn,
                                preferred_element_type=F32) + bias_ref[g]
        s_ctx = lax.dot_general(kc_ref[:, cols], qq, dn, preferred_element_type=F32)
        scores.append((s_loc, s_ctx))
    for g, (s_loc, s_ctx) in enumerate(scores):
        cols = slice(g * LANES, (g + 1) * LANES)
        m = jnp.maximum(jnp.max(s_loc, axis=0, keepdims=True), jnp.max(s_ctx, axis=0, keepdims=True))
        p_loc = jnp.exp2(s_loc - m)
        p_ctx = jnp.exp2(s_ctx - m)
        l = jnp.sum(p_loc, axis=0, keepdims=True) + jnp.sum(p_ctx, axis=0, keepdims=True)
        ot = (jnp.dot(vt_ref[cols, pl.ds(start, nk)], p_loc.astype(BF16), preferred_element_type=F32)
              + jnp.dot(vct_ref[cols, :], p_ctx.astype(BF16), preferred_element_type=F32)) * (1.0 / l)
        o = jnp.concatenate([ot[:HEAD_DIM, :nq], ot[HEAD_DIM:, nq:]], axis=0)
        o_ref[:, cols] = o.T.astype(BF16)


def _nbr_bias_table(rpb, rows):
    n_steps = rows // B_QROWS
    steps = [min(2, n_steps - 1), 0, 1, n_steps - 2, n_steps - 1]
    w = GRID_W
    n_heads = rpb.shape[0]
    c = np.arange(w)[None, :]
    kc = np.arange(w)[:, None]
    cs = np.clip(c - NA_KW // 2, 0, w - NA_KW)
    col_ok = (kc >= cs) & (kc < cs + NA_KW)
    onehot = (((kc - c + NA_KW - 1)[None] == np.arange(2 * NA_KW - 1)[:, None, None]) & col_ok[None])
    toe = jnp.einsum("hrd,dkc->hrkc", rpb.astype(F32), jnp.asarray(onehot, F32),
                     precision=lax.Precision.HIGHEST)
    toe = jnp.where(col_ok[None, None], toe * math.log2(math.e), MASK_VALUE)
    masked = jnp.full((n_heads, w, w), MASK_VALUE, F32)
    tabs = []
    for st in steps:
        r0 = st * B_QROWS
        ws = int(np.clip(r0 - NA_KH // 2, 0, rows - B_KROWS))
        q_cols = []
        for rq in range(B_QROWS):
            r = r0 + rq
            rs = int(np.clip(r - NA_KH // 2, 0, rows - NA_KH))
            blocks = [toe[:, ws + ki - r + NA_KH - 1] if rs <= ws + ki < rs + NA_KH else masked
                      for ki in range(B_KROWS)]
            q_cols.append(jnp.concatenate(blocks, axis=1))
        tabs.append(jnp.concatenate(q_cols, axis=2))
    tabs.append(jnp.full_like(tabs[0], MASK_VALUE))
    tab = jnp.stack(tabs)
    n_var, _, nk, nq = tab.shape
    tab = tab.reshape(n_var, n_heads // 2, 2, nk, nq)
    return jnp.transpose(tab, (0, 1, 3, 2, 4)).reshape(n_var, n_heads // 2, nk, 2 * nq)


def _attn_b(q, k, vt, bias_tab, n_batch, n_lat, n_ctx, with_ctx):
    t_rows = q.shape[0]
    rows = n_lat // GRID_W
    nq = B_QROWS * GRID_W
    nk = B_KROWS * GRID_W
    gw = B_PAIRS * LANES
    n_steps = rows // B_QROWS
    n_cstep = n_ctx // nq if with_ctx else 0
    qrow = lambda b, r: jnp.where(r < n_steps, b * n_steps + r,
                                  n_batch * n_steps + b * (n_ctx // nq) + (r - n_steps))
    ctx_blk = lambda b: (n_batch * n_lat) // n_ctx + b

    def variant(r):
        v = jnp.where(r == 0, 1, 0)
        v = jnp.where(r == 1, 2, v)
        v = jnp.where(r == n_steps - 2, 3, v)
        v = jnp.where(r == n_steps - 1, 4, v)
        return jnp.where(r >= n_steps, 5, v)

    return pl.pallas_call(
        functools.partial(_attn_b_kernel, rows=rows, n_steps=n_steps),
        grid=(n_batch, B_HEADS // (2 * B_PAIRS), n_steps + n_cstep),
        in_specs=[
            pl.BlockSpec((nq, gw), lambda b, j, r: (qrow(b, r), j)),
            pl.BlockSpec((n_lat, gw), lambda b, j, r: (b, j)),
            pl.BlockSpec((gw, n_lat), lambda b, j, r: (j, b)),
            pl.BlockSpec((n_ctx, gw), lambda b, j, r: (ctx_blk(b), j)),
            pl.BlockSpec((gw, n_ctx), lambda b, j, r: (j, ctx_blk(b))),
            pl.BlockSpec((None, B_PAIRS, nk, 2 * nq), lambda b, j, r: (variant(r), j, 0, 0)),
        ],
        out_specs=pl.BlockSpec((nq, gw), lambda b, j, r: (qrow(b, r), j)),
        out_shape=jax.ShapeDtypeStruct((t_rows if with_ctx else n_batch * n_lat, B_W), BF16),
        compiler_params=pltpu.CompilerParams(vmem_limit_bytes=VMEM_LIMIT),
    )(q, k, vt, k, vt, bias_tab)


def _attn_c_kernel(*refs, latent, n_kchunks, lam_init, n_cast=0):
    if latent:
        lam_ref, q_ref, k_ref, vt_ref, kc_ref, vct_ref, g_ref = refs[:7]
        cast_in = refs[7:7 + n_cast]
        o_ref = refs[7 + n_cast]
        cast_out = refs[8 + n_cast:8 + 2 * n_cast]
        s_a, mx_a, s_b, mx_b, s_c, mx_c, m_scr, l_scr, acc_scr = refs[8 + 2 * n_cast:]
        slots = ((s_a, mx_a), (s_b, mx_b))
        for src, dst in zip(cast_in, cast_out):
            dst[...] = src[...].astype(BF16)
    else:
        lam_ref, q_ref, kc_ref, vct_ref, g_ref, o_prev_ref, o_ref, s_c, mx_c, m_scr, l_scr, acc_scr = refs
    slot_c = (s_c, mx_c)
    tq = q_ref.shape[0]
    tk = C_KCHUNK
    lo = _lane_iota((tq, LANES)) < HEAD_DIM
    zero = jnp.zeros((tq, LANES), BF16)
    q_maps = []
    for hh in range(C_STEP_HEADS):
        q = q_ref[:, hh * LANES:(hh + 1) * LANES]
        q_maps += [jnp.where(lo, q, zero), jnp.where(lo, zero, q)]
    n_chains = len(q_maps)
    dn = (((1,), (1,)), ((), ()))

    def head_cols(ci):
        return slice((ci // 2) * LANES, (ci // 2 + 1) * LANES)

    def scores(kk_of, slot):
        s_ref, mx_ref = slot
        for ci in range(n_chains):
            s = lax.dot_general(kk_of(head_cols(ci)), q_maps[ci], dn, preferred_element_type=F32)
            s_ref[ci] = s
            mx_ref[ci] = jnp.max(s, axis=0, keepdims=True)

    def softmax_pv(slot, vvt_of):
        s_ref, mx_ref = slot
        for ci in range(n_chains):
            s = s_ref[ci]
            m = m_scr[ci]
            m_new = jnp.maximum(m, mx_ref[ci])
            alpha = jnp.exp2(m - m_new)
            p = jnp.exp2(s - m_new)
            l_scr[ci] = alpha * l_scr[ci] + jnp.sum(p, axis=0, keepdims=True)
            acc_scr[ci] = alpha * acc_scr[ci] + jnp.dot(vvt_of(head_cols(ci)), p.astype(BF16),
                                                        preferred_element_type=F32)
            m_scr[ci] = m_new

    def chunk_slice(c):
        return pl.ds(c * tk if isinstance(c, int) else pl.multiple_of(c * tk, tk), tk)

    def k_chunk(c):
        return lambda cols: k_ref[chunk_slice(c), cols]

    def vt_chunk(c):
        return lambda cols: vt_ref[cols, chunk_slice(c)]

    k_ctx = lambda cols: kc_ref[:, cols]
    vt_ctx = lambda cols: vct_ref[cols, :]

    m_scr[...] = jnp.full(m_scr.shape, -jnp.inf, F32)
    l_scr[...] = jnp.zeros(l_scr.shape, F32)
    acc_scr[...] = jnp.zeros(acc_scr.shape, F32)

    def finish():
        softmax_pv(slot_c, vt_ctx)
        lp = lam_ref[...]
        lam = (jnp.exp(jnp.sum(lp[0:1] * lp[1:2], axis=-1, keepdims=True))
               - jnp.exp(jnp.sum(lp[2:3] * lp[3:4], axis=-1, keepdims=True)) + lam_init)
        for hh in range(C_STEP_HEADS):
            c1, c2 = 2 * hh, 2 * hh + 1
            od = acc_scr[c1] * (1.0 / l_scr[c1]) - lam * (acc_scr[c2] * (1.0 / l_scr[c2]))
            ms = jnp.mean(od * od, axis=0, keepdims=True)
            on = (od * lax.rsqrt(ms + LN_EPS)).T
            o_ref[:, hh * LANES:(hh + 1) * LANES] = ((on * g_ref[...]) * (1.0 - lam_init)).astype(BF16)

    if not latent:
        scores(k_ctx, slot_c)
        finish()
        return

    scores(k_chunk(0), slots[0])

    def body(it, carry):
        c0 = C_UNROLL * it
        for u in range(C_UNROLL):
            scores(k_chunk(c0 + u + 1), slots[(u + 1) % 2])
            softmax_pv(slots[u % 2], vt_chunk(c0 + u))
        return carry

    n_it = (n_kchunks - 1) // C_UNROLL
    lax.fori_loop(0, n_it, body, 0)
    for c in range(C_UNROLL * n_it, n_kchunks):
        if c + 1 < n_kchunks:
            scores(k_chunk(c + 1), slots[(c + 1) % 2])
        else:
            scores(k_ctx, slot_c)
        softmax_pv(slots[c % 2], vt_chunk(c))
    finish()


def _attn_c(q, k, vt, lam_p, subln, lam_init, n_batch, n_lat, n_ctx, with_ctx, cast_jobs=()):
    t_rows = q.shape[0]
    hw = C_STEP_HEADS * LANES
    nch = 2 * C_STEP_HEADS
    ctx_blk = lambda b: (n_batch * n_lat) // n_ctx + b
    out_shape = jax.ShapeDtypeStruct((t_rows if with_ctx else n_batch * n_lat, C_OUT), BF16)
    params = pltpu.CompilerParams(vmem_limit_bytes=VMEM_LIMIT)
    stats = lambda tq: [pltpu.VMEM((nch, 1, tq), F32), pltpu.VMEM((nch, 1, tq), F32),
                        pltpu.VMEM((nch, LANES, tq), F32)]
    slot = lambda nk, tq: [pltpu.VMEM((nch, nk, tq), F32), pltpu.VMEM((nch, 1, tq), F32)]
    lam_spec = pl.BlockSpec((4, HEAD_DIM), lambda b, h, i: (0, 0))
    kc_spec = pl.BlockSpec((n_ctx, hw), lambda b, h, i: (ctx_blk(b), h))
    vct_spec = pl.BlockSpec((hw, n_ctx), lambda b, h, i: (h, ctx_blk(b)))
    g_spec = pl.BlockSpec((1, LANES), lambda b, h, i: (0, 0))
    g = subln.reshape(1, LANES)

    tq = C_QTILE
    n_qlat = n_lat // tq
    n_hgrp = C_HEADS // C_STEP_HEADS
    n_steps = n_batch * n_hgrp * n_qlat
    step = lambda b, h, i: (b * n_hgrp + h) * n_qlat + i
    cast_specs = []
    for w in cast_jobs:
        assert w.shape[0] % n_steps == 0 and (w.shape[0] // n_steps) % 16 == 0
        cast_specs.append(pl.BlockSpec((w.shape[0] // n_steps, w.shape[1]), lambda b, h, i: (step(b, h, i), 0)))
    outs = pl.pallas_call(
        functools.partial(_attn_c_kernel, latent=True, n_kchunks=n_lat // C_KCHUNK, lam_init=lam_init,
                          n_cast=len(cast_jobs)),
        grid=(n_batch, n_hgrp, n_qlat),
        in_specs=[
            lam_spec,
            pl.BlockSpec((tq, hw), lambda b, h, i: (b * n_qlat + i, h)),
            pl.BlockSpec((n_lat, hw), lambda b, h, i: (b, h)),
            pl.BlockSpec((hw, n_lat), lambda b, h, i: (h, b)),
            kc_spec, vct_spec, g_spec,
        ] + cast_specs,
        out_specs=[pl.BlockSpec((tq, hw), lambda b, h, i: (b * n_qlat + i, h))] + cast_specs,
        out_shape=[out_shape] + [jax.ShapeDtypeStruct(w.shape, BF16) for w in cast_jobs],
        scratch_shapes=(slot(C_KCHUNK, tq) + slot(C_KCHUNK, tq) + slot(n_ctx, tq) + stats(tq)),
        compiler_params=params,
    )(lam_p, q, k, vt, k, vt, g, *cast_jobs)
    o, casted = outs[0], list(outs[1:])
    if not with_ctx:
        return o, casted
    o = pl.pallas_call(
        functools.partial(_attn_c_kernel, latent=False, n_kchunks=0, lam_init=lam_init),
        grid=(n_batch, C_HEADS // C_STEP_HEADS, 1),
        in_specs=[
            lam_spec,
            pl.BlockSpec((n_ctx, hw), lambda b, h, i: (ctx_blk(b), h)),
            kc_spec, vct_spec, g_spec,
            pl.BlockSpec(memory_space=pl.ANY),
        ],
        out_specs=pl.BlockSpec((n_ctx, hw), lambda b, h, i: (ctx_blk(b), h)),
        out_shape=out_shape,
        scratch_shapes=slot(n_ctx, n_ctx) + stats(n_ctx),
        input_output_aliases={5: 0},
        compiler_params=params,
    )(lam_p, q, k, vt, g, o)
    return o, casted


def _bf16_truncate(x):
    bits = lax.bitcast_convert_type(x, jnp.uint32) & jnp.uint32(0xFFFF0000)
    return lax.bitcast_convert_type(bits, F32)


def _top2_router(h, wr_ref):
    h_top = _bf16_truncate(h)
    h_hi = h_top.astype(BF16)
    h_lo = (h - h_top).astype(BF16)
    a = jnp.dot(h_hi, wr_ref[...], preferred_element_type=F32)
    b = jnp.dot(h_lo, wr_ref[:, :LANES], preferred_element_type=F32)
    lg = a[:, :LANES] + a[:, LANES:] + b
    lane = _lane_iota(lg.shape)
    lanef = lane.astype(F32)
    lg = jnp.where(lane < N_EXPERTS, lg, -jnp.inf)
    m1 = jnp.max(lg, axis=-1, keepdims=True)
    i1 = jnp.min(jnp.where(lg == m1, lanef, float(LANES)), axis=-1, keepdims=True)
    lg2 = jnp.where(lanef == i1, -jnp.inf, lg)
    m2 = jnp.max(lg2, axis=-1, keepdims=True)
    i2 = jnp.min(jnp.where(lg2 == m2, lanef, float(LANES)), axis=-1, keepdims=True)
    e = jnp.exp(m2 - m1)
    w1 = 1.0 / (1.0 + e)
    w2 = e / (1.0 + e)
    out = jnp.where(lane == 0, i1, 0.0)
    out = jnp.where(lane == 1, i2, out)
    out = jnp.where(lane == 2, w1, out)
    return jnp.where(lane == 3, w2, out)


def _pack_bf16_pairs(h):
    bits = lax.bitcast_convert_type(h, jnp.uint32)
    bits = bits + (jnp.uint32(0x7FFF) + ((bits >> 16) & jnp.uint32(1)))
    half = h.shape[1] // 2
    packed = (bits[:, half:] & jnp.uint32(0xFFFF0000)) | (bits[:, :half] >> 16)
    return lax.bitcast_convert_type(packed, F32)


def _unpack_bf16_pairs(words):
    p = lax.bitcast_convert_type(words, jnp.uint32)
    lo = lax.bitcast_convert_type(p << 16, F32)
    hi = lax.bitcast_convert_type(p & jnp.uint32(0xFFFF0000), F32)
    return jnp.concatenate([lo, hi], axis=1)


def _outproj_kernel(*refs, n_in, router):
    o_refs = refs[:n_in]
    w_refs = refs[n_in:2 * n_in]
    x_ref, g_ref, lng_ref, lnb_ref, sc_ref, sh_ref = refs[2 * n_in:2 * n_in + 6]
    rest = refs[2 * n_in + 6:]
    y = jnp.dot(o_refs[0][...], w_refs[0][...], preferred_element_type=F32)
    for o_r, w_r in zip(o_refs[1:], w_refs[1:]):
        y = y + jnp.dot(o_r[...], w_r[...], preferred_element_type=F32)
    xn = _layernorm_rows(DN_ALPHA * x_ref[...] + g_ref[...] * y, lng_ref[...], lnb_ref[...])
    h2 = xn * (1.0 + sc_ref[...]) + sh_ref[...]
    if router:
        wr_ref, xo_ref, h_ref, r_ref = rest
        r_ref[...] = _top2_router(h2, wr_ref)
        h_ref[...] = _pack_bf16_pairs(h2)
    else:
        xo_ref, h_ref = rest
        h_ref[...] = h2.astype(BF16)
    xo_ref[...] = xn


def _out_proj(o_list, w_list, xs, mods_l, ln_g, ln_b, n_tiles, tiles_per_batch, n_batch, w_router=None):
    d = xs.shape[1]
    tm = ROW_TILE
    t_rows = n_tiles * tm
    grp = lambda t: jnp.minimum(t // tiles_per_batch, n_batch)
    mod = lambda k: pl.BlockSpec((None, None, 1, d), lambda t: (grp(t), k, 0, 0))
    in_specs = [pl.BlockSpec((tm, o.shape[1]), lambda t: (t, 0)) for o in o_list]
    in_specs += [pl.BlockSpec(w.shape, lambda t: (0, 0)) for w in w_list]
    in_specs += [pl.BlockSpec((tm, d), lambda t: (t, 0)), mod(2),
                 pl.BlockSpec((1, d), lambda t: (0, 0)), pl.BlockSpec((1, d), lambda t: (0, 0)),
                 mod(4), mod(3)]
    args = list(o_list) + list(w_list) + [xs, mods_l, ln_g.reshape(1, d), ln_b.reshape(1, d), mods_l, mods_l]
    out_shape = [jax.ShapeDtypeStruct((t_rows, d), F32), jax.ShapeDtypeStruct((t_rows, d), BF16)]
    out_specs = [pl.BlockSpec((tm, d), lambda t: (t, 0)), pl.BlockSpec((tm, d), lambda t: (t, 0))]
    if w_router is not None:
        out_shape[1] = jax.ShapeDtypeStruct((t_rows, d // 2), F32)
        out_specs[1] = pl.BlockSpec((tm, d // 2), lambda t: (t, 0))
        in_specs.append(pl.BlockSpec(w_router.shape, lambda t: (0, 0)))
        args.append(w_router)
        out_shape.append(jax.ShapeDtypeStruct((t_rows, LANES), F32))
        out_specs.append(pl.BlockSpec((tm, LANES), lambda t: (t, 0)))
    return pl.pallas_call(
        functools.partial(_outproj_kernel, n_in=len(o_list), router=w_router is not None),
        grid=(n_tiles,),
        in_specs=in_specs,
        out_specs=out_specs,
        out_shape=out_shape,
        compiler_params=pltpu.CompilerParams(vmem_limit_bytes=VMEM_LIMIT),
    )(*args)


def _swiglu_acc(h, wg_ref, wu_ref, wd_ref, acc_ref, c):
    g = jnp.dot(h, wg_ref[...], preferred_element_type=F32)
    u = jnp.dot(h, wu_ref[...], preferred_element_type=F32)
    a = (g * (1.0 / (1.0 + jnp.exp(-g))) * u).astype(BF16)
    part = jnp.dot(a, wd_ref[...], preferred_element_type=F32)

    @pl.when(c == 0)
    def _():
        acc_ref[...] = part

    @pl.when(c > 0)
    def _():
        acc_ref[...] += part


def _outproj_ffn_kernel(oa_ref, ob_ref, w1_ref, w2_ref, x_ref, g1_ref, lng1_ref, lnb1_ref, sc2_ref, sh2_ref,
                        wg_ref, wu_ref, wd_ref, g2_ref, lng2_ref, lnb2_ref, xo_ref):
    y = (jnp.dot(oa_ref[...], w1_ref[...], preferred_element_type=F32)
         + jnp.dot(ob_ref[...], w2_ref[...], preferred_element_type=F32))
    x1 = _layernorm_rows(DN_ALPHA * x_ref[...] + g1_ref[...] * y, lng1_ref[...], lnb1_ref[...])
    h2 = (x1 * (1.0 + sc2_ref[...]) + sh2_ref[...]).astype(BF16)
    g = jnp.dot(h2, wg_ref[...], preferred_element_type=F32)
    u = jnp.dot(h2, wu_ref[...], preferred_element_type=F32)
    a = (g * (1.0 / (1.0 + jnp.exp(-g))) * u).astype(BF16)
    f = jnp.dot(a, wd_ref[...], preferred_element_type=F32)
    xo_ref[...] = _layernorm_rows(DN_ALPHA * x1 + g2_ref[...] * f, lng2_ref[...], lnb2_ref[...])


def _out_proj_ffn(oa, ob, w1, w2, w_up, w_down, xs, mods_l, ln_g, ln_b, n_tiles, tiles_per_batch, n_batch):
    d = xs.shape[1]
    tm = ROW_TILE
    ff = w_down.shape[0]
    grp = lambda t: jnp.minimum(t // tiles_per_batch, n_batch)
    mod = lambda k: pl.BlockSpec((None, None, 1, d), lambda t: (grp(t), k, 0, 0))
    vec = pl.BlockSpec((1, d), lambda t: (0, 0))
    const = lambda shape, col=0: pl.BlockSpec(shape, lambda t: (0, col), pipeline_mode=pl.Buffered(1))
    return pl.pallas_call(
        _outproj_ffn_kernel,
        grid=(n_tiles,),
        in_specs=[
            pl.BlockSpec((tm, oa.shape[1]), lambda t: (t, 0)), pl.BlockSpec((tm, ob.shape[1]), lambda t: (t, 0)),
            const(w1.shape), const(w2.shape),
            pl.BlockSpec((tm, d), lambda t: (t, 0)), mod(2), vec, vec, mod(4), mod(3),
            const((d, ff)), const((d, ff), 1), const((ff, d)),
            mod(5), vec, vec,
        ],
        out_specs=pl.BlockSpec((tm, d), lambda t: (t, 0)),
        out_shape=jax.ShapeDtypeStruct((n_tiles * tm, d), F32),
        compiler_params=pltpu.CompilerParams(vmem_limit_bytes=VMEM_LIMIT),
    )(oa, ob, w1, w2, xs, mods_l, ln_g[0].reshape(1, d), ln_b[0].reshape(1, d), mods_l, mods_l,
      w_up, w_up, w_down, mods_l, ln_g[1].reshape(1, d), ln_b[1].reshape(1, d))


def _ffn_expert_kernel(te_ref, nu_ref, h_ref, wg_ref, wu_ref, wd_ref, y_ref, acc_ref):
    i = pl.program_id(0)
    c = pl.program_id(1)

    @pl.when(i < nu_ref[0])
    def _():
        _swiglu_acc(_unpack_bf16_pairs(h_ref[...]).astype(BF16), wg_ref, wu_ref, wd_ref, acc_ref, c)

    @pl.when(c == pl.num_programs(1) - 1)
    def _():
        y_ref[...] = _pack_bf16_pairs(acc_ref[...])


def _ffn_experts(hs, w_up, w_down, layer, tile_expert, n_used):
    p_rows = hs.shape[0]
    d = w_down.shape[-1]
    tm = MOE_TILE
    n_tiles = p_rows // tm
    ef = w_down.shape[2]
    nc = ef // EXPERT_FF_CHUNK
    chunk = lambda i, c, nu: jnp.where(i < nu[0], c, nc - 1)
    grid_spec = pltpu.PrefetchScalarGridSpec(
        num_scalar_prefetch=2,
        grid=(n_tiles, nc),
        in_specs=[
            pl.BlockSpec((tm, d // 2), lambda i, c, te, nu: (i, 0)),
            pl.BlockSpec((None, None, d, EXPERT_FF_CHUNK),
                         lambda i, c, te, nu: (layer, te[i], 0, chunk(i, c, nu))),
            pl.BlockSpec((None, None, d, EXPERT_FF_CHUNK),
                         lambda i, c, te, nu: (layer, te[i], 0, nc + chunk(i, c, nu))),
            pl.BlockSpec((None, None, EXPERT_FF_CHUNK, d),
                         lambda i, c, te, nu: (layer, te[i], chunk(i, c, nu), 0)),
        ],
        out_specs=pl.BlockSpec((tm, d // 2), lambda i, c, te, nu: (i, 0)),
        scratch_shapes=[pltpu.VMEM((tm, d), F32)],
    )
    return pl.pallas_call(
        _ffn_expert_kernel,
        grid_spec=grid_spec,
        out_shape=jax.ShapeDtypeStruct((p_rows, d // 2), F32),
        compiler_params=pltpu.CompilerParams(vmem_limit_bytes=VMEM_LIMIT),
    )(tile_expert, n_used, hs, w_up, w_up, w_down)


def _combine_kernel(y0_ref, y1_ref, r_ref, x_ref, g_ref, lng_ref, lnb_ref, xo_ref):
    r = r_ref[...]
    f = r[:, 2:3] * _unpack_bf16_pairs(y0_ref[...]) + r[:, 3:4] * _unpack_bf16_pairs(y1_ref[...])
    z = DN_ALPHA * x_ref[...] + g_ref[...] * f
    xo_ref[...] = _layernorm_rows(z, lng_ref[...], lnb_ref[...])


def _moe_combine(y0, y1, route, xs, mods_l, ln_g, ln_b, n_tiles, tiles_per_batch, n_batch):
    d = xs.shape[1]
    tm = ROW_TILE
    t_rows = n_tiles * tm
    grp = lambda t: jnp.minimum(t // tiles_per_batch, n_batch)
    row = pl.BlockSpec((tm, d), lambda t: (t, 0))
    packed = pl.BlockSpec((tm, d // 2), lambda t: (t, 0))
    return pl.pallas_call(
        _combine_kernel,
        grid=(n_tiles,),
        in_specs=[packed, packed, pl.BlockSpec((tm, LANES), lambda t: (t, 0)), row,
                  pl.BlockSpec((None, None, 1, d), lambda t: (grp(t), 5, 0, 0)),
                  pl.BlockSpec((1, d), lambda t: (0, 0)), pl.BlockSpec((1, d), lambda t: (0, 0))],
        out_specs=row,
        out_shape=jax.ShapeDtypeStruct((t_rows, d), F32),
        compiler_params=pltpu.CompilerParams(vmem_limit_bytes=VMEM_LIMIT),
    )(y0, y1, route, xs, mods_l, ln_g.reshape(1, d), ln_b.reshape(1, d))


def _routing_plan(route, n_rows):
    tm = MOE_TILE
    e_idx = route[:n_rows, 0:2].astype(jnp.int32).reshape(-1)
    onehot = (e_idx[:, None] == jnp.arange(N_EXPERTS, dtype=jnp.int32)[None, :]).astype(jnp.int32)
    csum = jnp.cumsum(onehot, axis=0)
    counts = csum[-1]
    rank = jnp.sum((csum - onehot) * onehot, axis=1)
    padded = ((counts + tm - 1) // tm) * tm
    ends = jnp.cumsum(padded)
    starts = ends - padded
    dest = starts[e_idx] + rank
    n_tiles = (2 * n_rows + N_EXPERTS * (tm - 1)) // tm
    p_rows = n_tiles * tm
    row_token = jnp.zeros((p_rows,), jnp.int32).at[dest].set(jnp.arange(2 * n_rows, dtype=jnp.int32) // 2)
    tile_start = jnp.arange(n_tiles, dtype=jnp.int32) * tm
    tile_expert = jnp.minimum(jnp.sum((tile_start[:, None] >= ends[None, :]).astype(jnp.int32), axis=1),
                              N_EXPERTS - 1)
    n_used = (ends[-1] // tm).astype(jnp.int32).reshape(1)
    last = tile_expert[jnp.maximum(n_used[0] - 1, 0)]
    tile_expert = jnp.where(jnp.arange(n_tiles) < n_used[0], tile_expert, last).astype(jnp.int32)
    return row_token, dest.reshape(n_rows, 2), tile_expert, n_used


def _rope_tables(n_lat):
    t = jnp.arange(n_lat, dtype=jnp.int32)
    row = (t // GRID_W).astype(F32)
    col = (t % GRID_W).astype(F32)
    inv = ROPE_THETA ** (-jnp.arange(0, ROPE_AXIS_DIM, 2, dtype=F32) / ROPE_AXIS_DIM)
    ar = row[:, None] * inv[None, :]
    ac = col[:, None] * inv[None, :]
    ang = jnp.concatenate([ar, ar, ac, ac], axis=-1)
    cos = jnp.tile(jnp.cos(ang), (1, LANES // HEAD_DIM))
    sin = jnp.tile(jnp.sin(ang), (1, LANES // HEAD_DIM))
    sign = jnp.where((jnp.arange(LANES) % 32) < 16, -1.0, 1.0).astype(F32)
    cos = jnp.concatenate([cos, jnp.ones((ROW_TILE, LANES), F32)], axis=0)
    sin = jnp.concatenate([sin * sign[None, :], jnp.zeros((ROW_TILE, LANES), F32)], axis=0)
    return cos, sin


def _lambda_init(layer):
    return 0.8 - 0.6 * math.exp(-0.3 * layer)


def kernel(x, c, ctx, c_ctx, w_mod, b_mod, ln_g, ln_b, w_in_ab, w_out_ab, sink_a, rpb_b, w_in_c, w_out_c,
           lam_c, subln_c, w_ffn_up, w_ffn_down, w_router, w_exp_up, w_exp_down):
    n_batch, n_lat, d = x.shape
    n_ctx = ctx.shape[1]
    assert d == D_MODEL and n_batch * n_ctx == ROW_TILE and n_lat % ROW_TILE == 0
    assert n_batch + 1 <= 8 and n_lat % C_KCHUNK == 0 and n_lat % C_QTILE == 0 and n_ctx % LANES == 0
    rows = n_lat // GRID_W
    assert rows >= 12 and rows % B_QROWS == 0
    tiles_per_batch = n_lat // ROW_TILE
    lat_tiles = n_batch * tiles_per_batch
    lat_rows = n_batch * n_lat

    xs = jnp.concatenate([x.reshape(lat_rows, d), ctx.reshape(n_batch * n_ctx, d)], axis=0)
    cond = jnp.zeros((8, d), F32).at[:n_batch].set(c).at[n_batch].set(c_ctx)
    mods = _mod_vectors(cond, w_mod, b_mod).reshape(DEPTH, 8, 6, 1, d)
    cos_t, sin_t = _rope_tables(n_lat)
    w_exp_b = None

    perm = np.array([(j + (A_HEADS // 2) * half) * HEAD_DIM + dd
                     for j in range(A_HEADS // 2) for half in range(2) for dd in range(HEAD_DIM)])

    for l in range(DEPTH):
        last = l == DEPTH - 1
        i = l // 2
        all_tiles = lat_tiles + 1
        n_tiles = lat_tiles if last else all_tiles
        mods_l = mods[l]
        if l % 2 == 0:
            w_in = w_in_ab[i]
            w_in = jnp.concatenate([w_in[:, :A_Q][:, perm], w_in[:, A_Q:]], axis=1).astype(BF16)
            q_scale = ATTN_SCALE * math.log2(math.e)
            segs = [(A_Q, True, q_scale, False), (A_KV, True, 1.0, False), (A_KV, False, 1.0, True),
                    (B_W, False, q_scale, False), (B_W, False, 1.0, False), (B_W, False, 1.0, True)]
            qa, ka, va, qb, kb, vb = _in_proj(xs, mods_l, cos_t, sin_t, w_in, segs, all_tiles,
                                              tiles_per_batch, n_batch)
            sink = sink_a[i].astype(F32) * math.log2(math.e)
            oa = _attn_a(qa, ka, va, sink, n_batch, n_lat, n_ctx, not last)
            ob = _attn_b(qb, kb, vb, _nbr_bias_table(rpb_b[i], rows), n_batch, n_lat, n_ctx, not last)
            w_out = w_out_ab[i]
            o_list = [oa, ob]
            w_list = [w_out[:A_Q][perm].astype(BF16), w_out[A_Q:].astype(BF16)]
        else:
            segs = [(C_QK, True, ATTN_SCALE * math.log2(math.e), False), (C_QK, True, 1.0, False),
                    (C_OUT, False, 1.0, True)]
            qc, kc, vc = _in_proj(xs, mods_l, cos_t, sin_t, w_in_c[i].astype(BF16), segs, all_tiles,
                                  tiles_per_batch, n_batch)
            jobs = () if w_exp_b is not None else (w_exp_up.reshape(-1, w_exp_up.shape[-1]),
                                                   w_exp_down.reshape(-1, w_exp_down.shape[-1]))
            oc, casted = _attn_c(qc, kc, vc, lam_c[i].astype(F32), subln_c[i].astype(F32), _lambda_init(l),
                                 n_batch, n_lat, n_ctx, not last, cast_jobs=jobs)
            if w_exp_b is None:
                w_exp_b = (casted[0].reshape(w_exp_up.shape), casted[1].reshape(w_exp_down.shape))
            o_list = [oc]
            w_list = [w_out_c[i].astype(BF16)]

        if l % 2 == 0:
            xs = _out_proj_ffn(o_list[0], o_list[1], w_list[0], w_list[1], w_ffn_up[i].astype(BF16),
                               w_ffn_down[i].astype(BF16), xs, mods_l, ln_g[l], ln_b[l], n_tiles,
                               tiles_per_batch, n_batch)
        else:
            wr = jnp.zeros((d, LANES), F32).at[:, :N_EXPERTS].set(w_router[i])
            wr_top = _bf16_truncate(wr)
            wr = jnp.concatenate([wr_top.astype(BF16), (wr - wr_top).astype(BF16)], axis=1)
            xs, h2, route = _out_proj(o_list, w_list, xs, mods_l, ln_g[l, 0], ln_b[l, 0], n_tiles,
                                      tiles_per_batch, n_batch, w_router=wr)
            n_rows = n_tiles * ROW_TILE
            row_token, dest, tile_expert, n_used = _routing_plan(route, n_rows)
            hs = jnp.take(h2, row_token, axis=0, mode="clip")
            ys = _ffn_experts(hs, w_exp_b[0], w_exp_b[1], i, tile_expert, n_used)
            y0 = jnp.take(ys, dest[:, 0], axis=0, mode="clip")
            y1 = jnp.take(ys, dest[:, 1], axis=0, mode="clip")
            xs = _moe_combine(y0, y1, route, xs, mods_l, ln_g[l, 1], ln_b[l, 1], n_tiles,
                              tiles_per_batch, n_batch)
    return xs[:lat_rows].reshape(n_batch, n_lat, d)
```
